```python
import math
import jax, jax.numpy as jnp
from jax import lax
import numpy as np

D_MODEL = 1024
BATCH = 2
SEQ = 8192
DEPTH = 1
DEC_BATCH = 128
DEC_SEQ = 4
PAST_LEN = 16384
PAGE_SIZE = 128

N_HEADS = 8
N_KV_HEADS = 2
HEAD_DIM = 64
GQA_GROUP = N_HEADS // N_KV_HEADS
ATTN_WIDTH = N_HEADS * HEAD_DIM
KV_WIDTH = N_KV_HEADS * HEAD_DIM
WINDOW = 128
CHUNK = 128
N_GATE_GROUPS = 4
GMLP_WIDTH = D_MODEL - ATTN_WIDTH
GROUP_CH = GMLP_WIDTH // N_GATE_GROUPS
PROJ_WIDTH = ATTN_WIDTH + 2 * KV_WIDTH + 2 * GMLP_WIDTH
D_FF = 2816
RMS_EPS = 1e-6
FFN_RESIDUAL = 0.5

kernel_name = "hymba_gmlp_swa_sink_macaron_step"


def _rmsnorm(x, g):
    xf = x.astype(jnp.float32)
    y = xf * lax.rsqrt(jnp.mean(xf * xf, axis=-1, keepdims=True) + RMS_EPS)
    return (y * g.astype(jnp.float32)).astype(x.dtype)


def _ffn_half(x, norm, wg, wu, wd):
    h = _rmsnorm(x, norm)
    return x + FFN_RESIDUAL * ((jax.nn.silu(h @ wg) * (h @ wu)) @ wd)


def _alibi_slopes():
    s = 2.0 ** (-8.0 * np.arange(1, N_HEADS + 1, dtype=np.float32) / N_HEADS)
    return jnp.asarray(s.astype(np.float32)).reshape(N_KV_HEADS, GQA_GROUP)


def _project(h, w_in):
    p = h @ w_in
    c1 = ATTN_WIDTH
    c2 = c1 + KV_WIDTH
    c3 = c2 + KV_WIDTH
    c4 = c3 + GMLP_WIDTH
    q, k, v, u, g = jnp.split(p, [c1, c2, c3, c4], axis=-1)
    lead = h.shape[:-1]
    q = q.reshape(*lead, N_KV_HEADS, GQA_GROUP, HEAD_DIM)
    k = k.reshape(*lead, N_KV_HEADS, HEAD_DIM)
    v = v.reshape(*lead, N_KV_HEADS, HEAD_DIM)
    u = jax.nn.gelu(u).reshape(*lead, N_GATE_GROUPS, GROUP_CH)
    g = jax.nn.gelu(g).reshape(*lead, N_GATE_GROUPS, GROUP_CH)
    return q, k, v, u, g


def _sink_attention(q, k, v, dist, valid, sinks):
    slopes = _alibi_slopes()
    s = jnp.einsum('...qhgd,...khd->...hgqk', q, k).astype(jnp.float32) / math.sqrt(HEAD_DIM)
    s = s - slopes[:, :, None, None] * dist.astype(jnp.float32)
    s = jnp.where(valid, s, -jnp.inf)
    sink = jnp.broadcast_to(sinks.astype(jnp.float32).reshape(N_KV_HEADS, GQA_GROUP, 1, 1),
                            s.shape[:-1] + (1,))
    p = jax.nn.softmax(jnp.concatenate([s, sink], axis=-1), axis=-1)[..., :-1]
    o = jnp.einsum('...hgqk,...khd->...qhgd', p.astype(v.dtype), v)
    return o.reshape(*o.shape[:-3], ATTN_WIDTH)


def _spatial_gate(u, g, v_norm, w_s, b_s, n):
    g = _rmsnorm(g, v_norm)
    w = jnp.tril(w_s[:, :n, :n])
    mixed = jnp.einsum('gij,...jgc->...igc', w, g) + b_s[:, :n].T[..., None]
    out = u * mixed
    return out.reshape(*u.shape[:-2], GMLP_WIDTH), g


def _merge(attn, gate, attn_out_norm, gmlp_out_norm, w_out):
    return jnp.concatenate([_rmsnorm(attn, attn_out_norm), _rmsnorm(gate, gmlp_out_norm)], axis=-1) @ w_out


def _mixer_prompt(h, w_in, sinks, v_norm, w_s, b_s, attn_out_norm, gmlp_out_norm, w_out):
    B, T, _ = h.shape
    nb = T // WINDOW
    q, k, v, u, g = _project(h, w_in)
    qb = q.reshape(B, nb, WINDOW, N_KV_HEADS, GQA_GROUP, HEAD_DIM)
    kb = k.reshape(B, nb, WINDOW, N_KV_HEADS, HEAD_DIM)
    vb = v.reshape(B, nb, WINDOW, N_KV_HEADS, HEAD_DIM)
    pad = ((0, 0), (1, 0), (0, 0), (0, 0), (0, 0))
    k_band = jnp.concatenate([jnp.pad(kb, pad)[:, :-1], kb], axis=2)
    v_band = jnp.concatenate([jnp.pad(vb, pad)[:, :-1], vb], axis=2)
    i = jnp.arange(WINDOW)[:, None]
    j = jnp.arange(2 * WINDOW)[None, :]
    dist = WINDOW + i - j
    valid = (dist >= 0) & (dist < WINDOW)
    has_prev = (jnp.arange(nb)[:, None, None] > 0) | (j >= WINDOW)[None]
    valid_full = (valid[None] & has_prev)[:, None, None]
    attn = _sink_attention(qb, k_band, v_band, dist, valid_full, sinks).reshape(B, T, ATTN_WIDTH)
    uc = u.reshape(B, T // CHUNK, CHUNK, N_GATE_GROUPS, GROUP_CH)
    gc = g.reshape(B, T // CHUNK, CHUNK, N_GATE_GROUPS, GROUP_CH)
    gate, _ = _spatial_gate(uc, gc, v_norm, w_s, b_s, CHUNK)
    gate = gate.reshape(B, T, GMLP_WIDTH)
    out = _merge(attn, gate, attn_out_norm, gmlp_out_norm, w_out)
    return out, k[:, T - WINDOW:], v[:, T - WINDOW:]


def _mixer_sample(h, cache_k, cache_v, w_in, sinks, v_norm, w_s, b_s, attn_out_norm, gmlp_out_norm, w_out):
    n = h.shape[1]
    L = cache_k.shape[1]
    q, k, v, u, g = _project(h, w_in)
    k_all = jnp.concatenate([cache_k.astype(k.dtype), k], axis=1)
    v_all = jnp.concatenate([cache_v.astype(v.dtype), v], axis=1)
    i = jnp.arange(n)[:, None]
    j = jnp.arange(L + n)[None, :]
    dist = L + i - j
    valid = (dist >= 0) & (dist < WINDOW)
    attn = _sink_attention(q, k_all, v_all, dist, valid, sinks)
    gate, g_n = _spatial_gate(u, g, v_norm, w_s, b_s, n)
    out = _merge(attn, gate, attn_out_norm, gmlp_out_norm, w_out)
    return out, k_all[:, -L:], v_all[:, -L:], g_n.reshape(h.shape[0], n, GMLP_WIDTH)


def setup_inputs(seed: int = 0) -> dict:
    key = jax.random.key(seed)
    ks = jax.random.split(key, 24)
    cache_rows = min(WINDOW, PAST_LEN)

    def nrm(k, shape, scale):
        return jax.random.normal(k, shape, jnp.float32) * scale

    def gain(k, shape):
        return 1.0 + 0.02 * jax.random.normal(k, shape, jnp.float32)

    return {
        "x_prompt": nrm(ks[0], (BATCH, SEQ, D_MODEL), 1.0),
        "x_sample": nrm(ks[1], (DEC_BATCH, DEC_SEQ, D_MODEL), 1.0),
        "cache_k": nrm(ks[2], (DEPTH, DEC_BATCH, cache_rows, N_KV_HEADS, HEAD_DIM), 1.0),
        "cache_v": nrm(ks[3], (DEPTH, DEC_BATCH, cache_rows, N_KV_HEADS, HEAD_DIM), 1.0),
        "ffn1_norm": gain(ks[4], (DEPTH, D_MODEL)),
        "ffn1_w_gate": nrm(ks[5], (DEPTH, D_MODEL, D_FF), D_MODEL ** -0.5),
        "ffn1_w_up": nrm(ks[6], (DEPTH, D_MODEL, D_FF), D_MODEL ** -0.5),
        "ffn1_w_down": nrm(ks[7], (DEPTH, D_FF, D_MODEL), D_FF ** -0.5),
        "mix_norm": gain(ks[8], (DEPTH, D_MODEL)),
        "w_in": nrm(ks[9], (DEPTH, D_MODEL, PROJ_WIDTH), D_MODEL ** -0.5),
        "attn_sinks": nrm(ks[10], (DEPTH, N_HEADS), 1.0),
        "gmlp_v_norm": gain(ks[11], (DEPTH, N_GATE_GROUPS, GROUP_CH)),
        "gmlp_w_spatial": nrm(ks[12], (DEPTH, N_GATE_GROUPS, CHUNK, CHUNK), 0.5 * CHUNK ** -0.5),
        "gmlp_b_spatial": gain(ks[13], (DEPTH, N_GATE_GROUPS, CHUNK)),
        "attn_out_norm": gain(ks[14], (DEPTH, ATTN_WIDTH)),
        "gmlp_out_norm": gain(ks[15], (DEPTH, GMLP_WIDTH)),
        "w_out": nrm(ks[16], (DEPTH, D_MODEL, D_MODEL), D_MODEL ** -0.5),
        "ffn2_norm": gain(ks[17], (DEPTH, D_MODEL)),
        "ffn2_w_gate": nrm(ks[18], (DEPTH, D_MODEL, D_FF), D_MODEL ** -0.5),
        "ffn2_w_up": nrm(ks[19], (DEPTH, D_MODEL, D_FF), D_MODEL ** -0.5),
        "ffn2_w_down": nrm(ks[20], (DEPTH, D_FF, D_MODEL), D_FF ** -0.5),
        "final_norm": gain(ks[21], (D_MODEL,)),
    }


def reference(x_prompt, x_sample, cache_k, cache_v, ffn1_norm, ffn1_w_gate, ffn1_w_up, ffn1_w_down,
              mix_norm, w_in, attn_sinks, gmlp_v_norm, gmlp_w_spatial, gmlp_b_spatial,
              attn_out_norm, gmlp_out_norm, w_out, ffn2_norm, ffn2_w_gate, ffn2_w_up, ffn2_w_down,
              final_norm):
    xp, xs = x_prompt, x_sample
    pk, pv, sk, sv, sc = [], [], [], [], []
    for l in range(DEPTH):
        xp = _ffn_half(xp, ffn1_norm[l], ffn1_w_gate[l], ffn1_w_up[l], ffn1_w_down[l])
        xs = _ffn_half(xs, ffn1_norm[l], ffn1_w_gate[l], ffn1_w_up[l], ffn1_w_down[l])
        mp, kp, vp = _mixer_prompt(_rmsnorm(xp, mix_norm[l]), w_in[l], attn_sinks[l], gmlp_v_norm[l],
                                   gmlp_w_spatial[l], gmlp_b_spatial[l], attn_out_norm[l],
                                   gmlp_out_norm[l], w_out[l])
        ms, ks_, vs_, cs = _mixer_sample(_rmsnorm(xs, mix_norm[l]), cache_k[l], cache_v[l], w_in[l],
                                         attn_sinks[l], gmlp_v_norm[l], gmlp_w_spatial[l],
                                         gmlp_b_spatial[l], attn_out_norm[l], gmlp_out_norm[l], w_out[l])
        xp = xp + mp
        xs = xs + ms
        xp = _ffn_half(xp, ffn2_norm[l], ffn2_w_gate[l], ffn2_w_up[l], ffn2_w_down[l])
        xs = _ffn_half(xs, ffn2_norm[l], ffn2_w_gate[l], ffn2_w_up[l], ffn2_w_down[l])
        pk.append(kp); pv.append(vp); sk.append(ks_); sv.append(vs_); sc.append(cs)
    y_prompt = _rmsnorm(xp, final_norm)
    y_sample = _rmsnorm(xs, final_norm)
    prompt_k = jnp.stack(pk, axis=0)
    prompt_v = jnp.stack(pv, axis=0)
    sample_k = jnp.stack(sk, axis=0)
    sample_v = jnp.stack(sv, axis=0)
    sample_chunk_v = jnp.stack(sc, axis=0)
    return (y_prompt, y_sample, prompt_k, prompt_v, sample_k, sample_v, sample_chunk_v)
```

```python
import functools

import jax
import jax.numpy as jnp
from jax import lax
from jax.experimental import pallas as pl
from jax.experimental.pallas import tpu as pltpu

F32 = jnp.float32
BF16 = jnp.bfloat16

D_MODEL = 1024
D_FF = 2816
N_HEADS = 8
N_KV_HEADS = 2
GQA_GROUP = 4
HEAD_DIM = 64
ATTN_WIDTH = 512
KV_WIDTH = 128
GMLP_WIDTH = 512
N_GATE_GROUPS = 4
GROUP_CH = 128
WINDOW = 128
CHUNK = 128
RMS_EPS = 1e-6
FFN_RESIDUAL = 0.5
ATTN_SCALE = 0.125
ALIBI_SLOPES = tuple(2.0 ** (-(i + 1)) for i in range(N_HEADS))
MASK_VALUE = -1e30

LANES = 128
FFN_ROWS = 512
FFN_CHUNK = 256
MIX_ROWS = 512
SEQ_PER_STEP = 16
SAMPLE_KEYS = 256
VMEM_LIMIT = 56 * 1024 * 1024


def _rms(x, g):
    ms = jnp.mean(x * x, axis=-1, keepdims=True)
    return x * lax.rsqrt(ms + RMS_EPS) * g


def _dot(a, b):
    return jnp.dot(a, b, preferred_element_type=F32)


def _dot_nt(a, b):
    return lax.dot_general(a, b, (((1,), (1,)), ((), ())), preferred_element_type=F32)


def _resident(shape):
    nd = len(shape)
    return pl.BlockSpec(shape, lambda *_: (0,) * nd, pipeline_mode=pl.Buffered(1))


def _smem():
    return pl.BlockSpec(memory_space=pltpu.SMEM)


def _ffn_body(xp_ref, xs_ref, norm_ref, wg_ref, wu_ref, wd_ref, fnorm_ref, op_ref, os_ref, act_ref,
              *, n_prompt_tiles, final_norm):
    i = pl.program_id(0)
    is_prompt = i < n_prompt_tiles
    x = jnp.where(is_prompt, xp_ref[...], xs_ref[...])
    h = _rms(x, norm_ref[...]).astype(BF16)
    for c in range(D_FF // FFN_CHUNK):
        cols = slice(c * FFN_CHUNK, (c + 1) * FFN_CHUNK)
        a = _dot(h, wg_ref[:, cols])
        b = _dot(h, wu_ref[:, cols])
        act_ref[:, cols] = (a * jax.nn.sigmoid(a) * b).astype(BF16)
    y = x + FFN_RESIDUAL * _dot(act_ref[...], wd_ref[...])
    if final_norm:
        y = _rms(y, fnorm_ref[...])

    @pl.when(is_prompt)
    def _():
        op_ref[...] = y

    @pl.when(jnp.logical_not(is_prompt))
    def _():
        os_ref[...] = y


def _ffn_half(xp, xs, norm, wg, wu, wd, fnorm, *, final_norm):
    n_prompt_tiles = xp.shape[0] // FFN_ROWS
    assert xp.shape[0] == n_prompt_tiles * FFN_ROWS and xs.shape[0] == FFN_ROWS
    row_block = (FFN_ROWS, D_MODEL)
    prompt_spec = pl.BlockSpec(row_block, lambda i: (jnp.minimum(i, n_prompt_tiles - 1), 0))
    sample_spec = pl.BlockSpec(row_block, lambda i: (0, 0))
    return pl.pallas_call(
        functools.partial(_ffn_body, n_prompt_tiles=n_prompt_tiles, final_norm=final_norm),
        grid=(n_prompt_tiles + 1,),
        in_specs=[prompt_spec, sample_spec, _resident((1, D_MODEL)), _resident((D_MODEL, D_FF)),
                  _resident((D_MODEL, D_FF)), _resident((D_FF, D_MODEL)), _resident((1, D_MODEL))],
        out_specs=[prompt_spec, sample_spec],
        out_shape=[jax.ShapeDtypeStruct(xp.shape, F32), jax.ShapeDtypeStruct(xs.shape, F32)],
        scratch_shapes=[pltpu.VMEM((FFN_ROWS, D_FF), BF16)],
        compiler_params=pltpu.CompilerParams(dimension_semantics=("arbitrary",), vmem_limit_bytes=VMEM_LIMIT),
        name="ffn_half_final" if final_norm else "ffn_half",
    )(xp, xs, norm, wg, wu, wd, fnorm)


def _pair_heads(t):
    swapped = pltpu.roll(t, HEAD_DIM, 1)
    left = lax.broadcasted_iota(jnp.int32, t.shape, 1) < HEAD_DIM
    return jnp.where(left, t, swapped), jnp.where(left, swapped, t)


def _gate_norm(g, vnorm_ref, gi):
    gg = g[:, gi * GROUP_CH:(gi + 1) * GROUP_CH]
    return _rms(gg, vnorm_ref[gi:gi + 1, :])


def _merge(x, attn, gate, anorm_ref, gnorm_ref, wout_ref):
    cat = jnp.concatenate([_rms(attn, anorm_ref[...]), _rms(gate, gnorm_ref[...])], axis=1)
    return x + _dot(cat.astype(BF16), wout_ref[...])


def _mixer_prompt_body(x_ref, mnorm_ref, win_ref, sinks_ref, vnorm_ref, ws_ref, bst_ref, anorm_ref, gnorm_ref,
                       wout_ref, o_ref, pk_ref, pv_ref, kv_s, q_s, attn_s, gate_s, bias_s, wtril_s):
    b = pl.program_id(0)
    t = pl.program_id(1)
    nblk = MIX_ROWS // WINDOW
    band = 2 * WINDOW
    seg = GQA_GROUP * band

    @pl.when((b == 0) & (t == 0))
    def _init_tables():
        row = lax.broadcasted_iota(jnp.int32, (WINDOW, seg), 0)
        col = lax.broadcasted_iota(jnp.int32, (WINDOW, seg), 1)
        grp = col >> 8
        key = col & (band - 1)
        dist = WINDOW + row - key
        valid = (dist >= 0) & (dist < WINDOW)
        distf = dist.astype(F32)
        for h in range(N_KV_HEADS):
            s = [ALIBI_SLOPES[h * GQA_GROUP + g] for g in range(GQA_GROUP)]
            slope = jnp.where(grp == 0, s[0], jnp.where(grp == 1, s[1], jnp.where(grp == 2, s[2], s[3])))
            bias = jnp.where(valid, -(slope * distf), MASK_VALUE)
            bias_s[0, h] = bias
            bias_s[1, h] = jnp.where(key < WINDOW, MASK_VALUE, bias)
        r = lax.broadcasted_iota(jnp.int32, (CHUNK, CHUNK), 0)
        c = lax.broadcasted_iota(jnp.int32, (CHUNK, CHUNK), 1)
        for gi in range(N_GATE_GROUPS):
            wtril_s[gi] = jnp.where(r >= c, ws_ref[gi], 0.0).astype(BF16)

    @pl.when(t == 0)
    def _no_previous_block():
        kv_s[0:WINDOW, :] = jnp.zeros((WINDOW, 4 * KV_WIDTH), F32)

    x = x_ref[...]
    h_in = _rms(x, mnorm_ref[...]).astype(BF16)
    q_s[...] = (_dot(h_in, win_ref[:, 0:ATTN_WIDTH]) * ATTN_SCALE).astype(BF16)
    pkv = _dot(h_in, win_ref[:, ATTN_WIDTH:ATTN_WIDTH + 2 * KV_WIDTH])
    k_new = pkv[:, 0:KV_WIDTH]
    v_new = pkv[:, KV_WIDTH:2 * KV_WIDTH]
    pk_ref[0] = k_new[MIX_ROWS - WINDOW:, :]
    pv_ref[0] = v_new[MIX_ROWS - WINDOW:, :]
    k0, k1 = _pair_heads(k_new)
    v0, v1 = _pair_heads(v_new)
    kv_s[WINDOW:, :] = jnp.concatenate([k0, k1, v0, v1], axis=1)

    table = jnp.where(t == 0, 1, 0)
    grp_of_lane = lax.broadcasted_iota(jnp.int32, (band, 2 * KV_WIDTH), 1) >> 6
    out_grp = lax.broadcasted_iota(jnp.int32, (WINDOW, 2 * KV_WIDTH), 1) >> 6
    for j in range(nblk):
        rows = slice(j * WINDOW, (j + 1) * WINDOW)
        kvb = kv_s[j * WINDOW:j * WINDOW + band, :]
        for h in range(N_KV_HEADS):
            ka = kvb[:, h * KV_WIDTH:(h + 1) * KV_WIDTH]
            va = kvb[:, (2 + h) * KV_WIDTH:(3 + h) * KV_WIDTH]
            ka2 = jnp.concatenate([ka, ka], axis=1)
            va2 = jnp.concatenate([va, va], axis=1)
            kbig = jnp.concatenate([jnp.where(grp_of_lane == g, ka2, 0.0) for g in range(GQA_GROUP)],
                                   axis=0).astype(BF16)
            vbig = jnp.concatenate([jnp.where(grp_of_lane == g, va2, 0.0) for g in range(GQA_GROUP)],
                                   axis=0).astype(BF16)
            q = q_s[rows, h * 2 * KV_WIDTH:(h + 1) * 2 * KV_WIDTH]
            bias = bias_s[table, h] if j == 0 else bias_s[0, h]
            s = _dot_nt(q, kbig) + bias
            probs, inv = [], []
            for g in range(GQA_GROUP):
                sg = s[:, g * band:(g + 1) * band]
                sink = sinks_ref[h * GQA_GROUP + g]
                m = jnp.maximum(jnp.max(sg, axis=1, keepdims=True), sink)
                p = jnp.exp(sg - m)
                inv.append(1.0 / (jnp.sum(p, axis=1, keepdims=True) + jnp.exp(sink - m)))
                probs.append(p.astype(BF16))
            o = _dot(jnp.concatenate(probs, axis=1), vbig)
            scale = jnp.where(out_grp == 0, inv[0], jnp.where(out_grp == 1, inv[1],
                                                              jnp.where(out_grp == 2, inv[2], inv[3])))
            attn_s[rows, h * 2 * KV_WIDTH:(h + 1) * 2 * KV_WIDTH] = o * scale
    kv_s[0:WINDOW, :] = kv_s[MIX_ROWS:MIX_ROWS + WINDOW, :]

    u = jax.nn.gelu(_dot(h_in, win_ref[:, 768:1280]))
    gact = jax.nn.gelu(_dot(h_in, win_ref[:, 1280:1792]))
    for gi in range(N_GATE_GROUPS):
        lanes = slice(gi * GROUP_CH, (gi + 1) * GROUP_CH)
        gn = _gate_norm(gact, vnorm_ref, gi).astype(BF16)
        chunks = jnp.concatenate([gn[c * CHUNK:(c + 1) * CHUNK, :] for c in range(nblk)], axis=1)
        mixed = _dot(wtril_s[gi], chunks) + bst_ref[:, gi:gi + 1]
        for c in range(nblk):
            rows = slice(c * CHUNK, (c + 1) * CHUNK)
            gate_s[rows, lanes] = u[rows, lanes] * mixed[:, c * GROUP_CH:(c + 1) * GROUP_CH]

    o_ref[...] = _merge(x, attn_s[...], gate_s[...], anorm_ref, gnorm_ref, wout_ref)


def _mixer_prompt(x, mnorm, win, sinks, vnorm, ws, bst, anorm, gnorm, wout, *, batch):
    rows = x.shape[0]
    tiles = rows // (batch * MIX_ROWS)
    assert rows == batch * tiles * MIX_ROWS
    band = 2 * WINDOW
    row_spec = pl.BlockSpec((MIX_ROWS, D_MODEL), lambda b, t: (b * tiles + t, 0))
    kv_spec = pl.BlockSpec((1, WINDOW, KV_WIDTH), lambda b, t: (b, 0, 0))
    kv_shape = jax.ShapeDtypeStruct((batch, WINDOW, KV_WIDTH), F32)
    return pl.pallas_call(
        _mixer_prompt_body,
        grid=(batch, tiles),
        in_specs=[row_spec, _resident((1, D_MODEL)), _resident(win.shape), _smem(), _resident(vnorm.shape),
                  _resident(ws.shape), _resident(bst.shape), _resident(anorm.shape), _resident(gnorm.shape),
                  _resident(wout.shape)],
        out_specs=[row_spec, kv_spec, kv_spec],
        out_shape=[jax.ShapeDtypeStruct(x.shape, F32), kv_shape, kv_shape],
        scratch_shapes=[
            pltpu.VMEM((WINDOW + MIX_ROWS, 4 * KV_WIDTH), F32),
            pltpu.VMEM((MIX_ROWS, ATTN_WIDTH), BF16),
            pltpu.VMEM((MIX_ROWS, ATTN_WIDTH), F32),
            pltpu.VMEM((MIX_ROWS, GMLP_WIDTH), F32),
            pltpu.VMEM((2, N_KV_HEADS, WINDOW, GQA_GROUP * band), F32),
            pltpu.VMEM((N_GATE_GROUPS, CHUNK, CHUNK), BF16),
        ],
        compiler_params=pltpu.CompilerParams(dimension_semantics=("arbitrary", "arbitrary"),
                                             vmem_limit_bytes=VMEM_LIMIT),
        name="mixer_prompt",
    )(x, mnorm, win, sinks, vnorm, ws, bst, anorm, gnorm, wout)


def _mixer_sample_body(x_ref, mnorm_ref, win_ref, sinks_ref, vnorm_ref, wsm_ref, bsm_ref, anorm_ref, gnorm_ref,
                       wout_ref, ck_ref, cv_ref, o_ref, gn_ref, sk_ref, sv_ref,
                       qall_s, oall_s, knew_s, vnew_s, gate_s, bias_s, *, n_seq, n_tok):
    step = pl.program_id(0)
    n_steps = pl.num_programs(0)
    n_rows = n_seq * n_tok
    q_rows = N_HEADS * n_tok
    slots = 8

    @pl.when(step == 0)
    def _project():
        x = x_ref[...]
        h_in = _rms(x, mnorm_ref[...]).astype(BF16)
        pq = _dot(h_in, win_ref[:, 0:ATTN_WIDTH]) * ATTN_SCALE
        pkv = _dot(h_in, win_ref[:, ATTN_WIDTH:ATTN_WIDTH + 2 * KV_WIDTH])
        zeros = jnp.zeros(((slots - n_tok) * n_seq, KV_WIDTH), F32)
        knew_s[0:(slots - n_tok) * n_seq, :] = zeros
        vnew_s[0:(slots - n_tok) * n_seq, :] = zeros
        knew_s[(slots - n_tok) * n_seq:, :] = pkv[:, 0:KV_WIDTH]
        vnew_s[(slots - n_tok) * n_seq:, :] = pkv[:, KV_WIDTH:2 * KV_WIDTH]
        half = lax.broadcasted_iota(jnp.int32, (n_rows, LANES), 1) >> 6
        for hg in range(N_HEADS):
            kvh = hg // GQA_GROUP
            slab = pq[:, (hg // 2) * LANES:(hg // 2 + 1) * LANES]
            if (hg % 2) != kvh:
                slab = pltpu.roll(slab, HEAD_DIM, 1)
            qall_s[hg * n_rows:(hg + 1) * n_rows, :] = jnp.where(half == kvh, slab, 0.0)

        u = jax.nn.gelu(_dot(h_in, win_ref[:, 768:1280]))
        gact = jax.nn.gelu(_dot(h_in, win_ref[:, 1280:1792]))
        for gi in range(N_GATE_GROUPS):
            lanes = slice(gi * GROUP_CH, (gi + 1) * GROUP_CH)
            gn = _gate_norm(gact, vnorm_ref, gi)
            gn_ref[:, lanes] = gn
            for i in range(n_tok):
                mixed = jnp.full((n_seq, GROUP_CH), bsm_ref[gi * n_tok + i], F32)
                for j in range(i + 1):
                    mixed = mixed + wsm_ref[(gi * n_tok + i) * n_tok + j] * gn[j * n_seq:(j + 1) * n_seq, :]
                rows = slice(i * n_seq, (i + 1) * n_seq)
                gate_s[rows, lanes] = u[rows, lanes] * mixed

        row = lax.broadcasted_iota(jnp.int32, (q_rows, SAMPLE_KEYS), 0)
        col = lax.broadcasted_iota(jnp.int32, (q_rows, SAMPLE_KEYS), 1)
        tok = row & (n_tok - 1)
        head = row >> 2
        slope = jnp.zeros((q_rows, SAMPLE_KEYS), F32)
        sink = jnp.zeros((q_rows, SAMPLE_KEYS), F32)
        for hg in range(N_HEADS):
            slope = jnp.where(head == hg, ALIBI_SLOPES[hg], slope)
            sink = jnp.where(head == hg, sinks_ref[hg], sink)
        new_tok = col - (WINDOW + slots - n_tok)
        is_cache = col < WINDOW
        is_new = (new_tok >= 0) & (new_tok < n_tok)
        dist = jnp.where(is_cache, WINDOW + tok - col, tok - new_tok)
        valid = (is_cache | is_new) & (dist >= 0) & (dist < WINDOW)
        bias = jnp.where(valid, -(slope * dist.astype(F32)), MASK_VALUE)
        bias_s[...] = jnp.where(col == WINDOW, sink, bias)

    pad = jnp.zeros((SAMPLE_KEYS - WINDOW - slots, KV_WIDTH), F32)
    sub = lax.broadcasted_iota(jnp.int32, (slots, KV_WIDTH), 0)

    def _shifted(cache, new8):
        rolled = pltpu.roll(cache, WINDOW - n_tok, 0)
        tail = jnp.where(sub >= slots - n_tok, new8, rolled[WINDOW - slots:, :])
        return jnp.concatenate([rolled[0:WINDOW - slots, :], tail], axis=0)

    def _one_sequence(bl, carry):
        bb = step * SEQ_PER_STEP + bl
        qb = qall_s[pl.ds(bb, q_rows, stride=n_seq), :].astype(BF16)
        kc = ck_ref[bl]
        vc = cv_ref[bl]
        kn = knew_s[pl.ds(bb, slots, stride=n_seq), :]
        vn = vnew_s[pl.ds(bb, slots, stride=n_seq), :]
        kext = jnp.concatenate([kc, kn, pad], axis=0).astype(BF16)
        vext = jnp.concatenate([vc, vn, pad], axis=0).astype(BF16)
        s = _dot_nt(qb, kext) + bias_s[...]
        m = jnp.max(s, axis=1, keepdims=True)
        p = jnp.exp(s - m)
        inv = 1.0 / jnp.sum(p, axis=1, keepdims=True)
        oall_s[pl.ds(bb, q_rows, stride=n_seq), :] = _dot(p.astype(BF16), vext) * inv
        sk_ref[bl] = _shifted(kc, kn)
        sv_ref[bl] = _shifted(vc, vn)
        return carry

    lax.fori_loop(0, SEQ_PER_STEP, _one_sequence, 0)

    @pl.when(step == n_steps - 1)
    def _merge_rows():
        left = lax.broadcasted_iota(jnp.int32, (n_rows, LANES), 1) < HEAD_DIM
        slabs = []
        for pair in range(N_HEADS // 2):
            halves = []
            for hg in (2 * pair, 2 * pair + 1):
                o = oall_s[hg * n_rows:(hg + 1) * n_rows, :]
                if (hg % 2) != (hg // GQA_GROUP):
                    o = pltpu.roll(o, HEAD_DIM, 1)
                halves.append(o)
            slabs.append(jnp.where(left, halves[0], halves[1]))
        attn = jnp.concatenate(slabs, axis=1)
        o_ref[...] = _merge(x_ref[...], attn, gate_s[...], anorm_ref, gnorm_ref, wout_ref)


def _mixer_sample(x, mnorm, win, sinks, vnorm, wsm, bsm, anorm, gnorm, wout, ck, cv, *, n_seq, n_tok):
    n_rows = n_seq * n_tok
    assert x.shape[0] == n_rows and n_seq % SEQ_PER_STEP == 0 and n_tok == 4
    cache_spec = pl.BlockSpec((SEQ_PER_STEP, WINDOW, KV_WIDTH), lambda s: (s, 0, 0))
    cache_shape = jax.ShapeDtypeStruct(ck.shape, F32)
    rows_spec = pl.BlockSpec((n_rows, D_MODEL), lambda s: (0, 0))
    gn_spec = pl.BlockSpec((n_rows, GMLP_WIDTH), lambda s: (0, 0))
    return pl.pallas_call(
        functools.partial(_mixer_sample_body, n_seq=n_seq, n_tok=n_tok),
        grid=(n_seq // SEQ_PER_STEP,),
        in_specs=[rows_spec, _resident((1, D_MODEL)), _resident(win.shape), _smem(), _resident(vnorm.shape),
                  _smem(), _smem(), _resident(anorm.shape), _resident(gnorm.shape), _resident(wout.shape),
                  cache_spec, cache_spec],
        out_specs=[rows_spec, gn_spec, cache_spec, cache_spec],
        out_shape=[jax.ShapeDtypeStruct(x.shape, F32), jax.ShapeDtypeStruct((n_rows, GMLP_WIDTH), F32),
                   cache_shape, cache_shape],
        scratch_shapes=[
            pltpu.VMEM((N_HEADS * n_rows, LANES), F32),
            pltpu.VMEM((N_HEADS * n_rows, LANES), F32),
            pltpu.VMEM((8 * n_seq, KV_WIDTH), F32),
            pltpu.VMEM((8 * n_seq, KV_WIDTH), F32),
            pltpu.VMEM((n_rows, GMLP_WIDTH), F32),
            pltpu.VMEM((N_HEADS * n_tok, SAMPLE_KEYS), F32),
        ],
        compiler_params=pltpu.CompilerParams(dimension_semantics=("arbitrary",), vmem_limit_bytes=VMEM_LIMIT),
        name="mixer_sample",
    )(x, mnorm, win, sinks, vnorm, wsm, bsm, anorm, gnorm, wout, ck, cv)


def kernel(x_prompt, x_sample, cache_k, cache_v, ffn1_norm, ffn1_w_gate, ffn1_w_up, ffn1_w_down, mix_norm, w_in,
           attn_sinks, gmlp_v_norm, gmlp_w_spatial, gmlp_b_spatial, attn_out_norm, gmlp_out_norm, w_out, ffn2_norm,
           ffn2_w_gate, ffn2_w_up, ffn2_w_down, final_norm):
    batch, seq, _ = x_prompt.shape
    n_seq, n_tok, _ = x_sample.shape
    depth = cache_k.shape[0]
    assert depth == 1

    xp = x_prompt.reshape(batch * seq, D_MODEL)
    xs = jnp.transpose(x_sample, (1, 0, 2)).reshape(n_tok * n_seq, D_MODEL)
    row = lambda v: v.reshape(1, -1)
    fnorm = row(final_norm)
    l = 0
    wg1, wu1, wd1 = (w[l].astype(BF16) for w in (ffn1_w_gate, ffn1_w_up, ffn1_w_down))
    wg2, wu2, wd2 = (w[l].astype(BF16) for w in (ffn2_w_gate, ffn2_w_up, ffn2_w_down))
    win = w_in[l].astype(BF16)
    wout = w_out[l].astype(BF16)
    sinks = attn_sinks[l]
    vnorm = gmlp_v_norm[l]
    ws = gmlp_w_spatial[l]
    bs = gmlp_b_spatial[l]
    mnorm, anorm, gnorm = row(mix_norm[l]), row(attn_out_norm[l]), row(gmlp_out_norm[l])

    xp, xs = _ffn_half(xp, xs, row(ffn1_norm[l]), wg1, wu1, wd1, fnorm, final_norm=False)
    xp, pk, pv = _mixer_prompt(xp, mnorm, win, sinks, vnorm, ws, bs.T, anorm, gnorm, wout, batch=batch)
    ck = cache_k[l].reshape(n_seq, WINDOW, KV_WIDTH)
    cv = cache_v[l].reshape(n_seq, WINDOW, KV_WIDTH)
    xs, gn, sk, sv = _mixer_sample(xs, mnorm, win, sinks, vnorm, ws[:, :n_tok, :n_tok].reshape(-1),
                                   bs[:, :n_tok].reshape(-1), anorm, gnorm, wout, ck, cv, n_seq=n_seq, n_tok=n_tok)
    yp, ys = _ffn_half(xp, xs, row(ffn2_norm[l]), wg2, wu2, wd2, fnorm, final_norm=True)

    kv5 = lambda a, n: a.reshape(1, n, WINDOW, N_KV_HEADS, HEAD_DIM)
    y_prompt = yp.reshape(batch, seq, D_MODEL)
    y_sample = jnp.transpose(ys.reshape(n_tok, n_seq, D_MODEL), (1, 0, 2))
    chunk_v = jnp.transpose(gn.reshape(n_tok, n_seq, GMLP_WIDTH), (1, 0, 2))[None]
    return (y_prompt, y_sample, kv5(pk, batch), kv5(pv, batch), kv5(sk, n_seq), kv5(sv, n_seq), chunk_v)
```

```python
import functools

import jax
import jax.numpy as jnp
from jax import lax
from jax.experimental import pallas as pl
from jax.experimental.pallas import tpu as pltpu

F32 = jnp.float32
BF16 = jnp.bfloat16

D_MODEL = 1024
D_FF = 2816
N_HEADS = 8
N_KV_HEADS = 2
GQA_GROUP = 4
HEAD_DIM = 64
ATTN_WIDTH = 512
KV_WIDTH = 128
GMLP_WIDTH = 512
N_GATE_GROUPS = 4
GROUP_CH = 128
WINDOW = 128
CHUNK = 128
RMS_EPS = 1e-6
FFN_RESIDUAL = 0.5
ATTN_SCALE = 0.125
LOG2E = 1.4426950408889634
ALIBI_SLOPES = tuple(2.0 ** (-(i + 1)) for i in range(N_HEADS))
MASK_VALUE = -1e30

LANES = 128
BF16_SUBLANES = 16
FFN_ROWS = 512
FFN_CHUNK = 256
MIX_ROWS = 512
SEQ_PER_STEP = 16
SAMPLE_KEYS = 256
VMEM_LIMIT = 56 * 1024 * 1024


def _rms(x, g):
    ms = jnp.mean(x * x, axis=-1, keepdims=True)
    return x * lax.rsqrt(ms + RMS_EPS) * g


def _dot(a, b):
    return jnp.dot(a, b, preferred_element_type=F32)


def _dot_nt(a, b):
    return lax.dot_general(a, b, (((1,), (1,)), ((), ())), preferred_element_type=F32)


def _resident(shape):
    nd = len(shape)
    return pl.BlockSpec(shape, lambda *_: (0,) * nd, pipeline_mode=pl.Buffered(1))


def _smem():
    return pl.BlockSpec(memory_space=pltpu.SMEM)


def _ffn_body(*refs, n_prompt_tiles, final_norm, n_cast):
    xp_ref, xs_ref, norm_ref, wg_ref, wu_ref, wd_ref, fnorm_ref = refs[:7]
    cast_in = refs[7:7 + n_cast]
    op_ref, os_ref = refs[7 + n_cast:9 + n_cast]
    cast_out = refs[9 + n_cast:9 + 2 * n_cast]
    act_ref = refs[9 + 2 * n_cast]
    i = pl.program_id(0)
    is_prompt = i < n_prompt_tiles
    x = jnp.where(is_prompt, xp_ref[...], xs_ref[...])
    h = _rms(x, norm_ref[...]).astype(BF16)
    for c in range(D_FF // FFN_CHUNK):
        cols = slice(c * FFN_CHUNK, (c + 1) * FFN_CHUNK)
        a = _dot(h, wg_ref[:, cols])
        b = _dot(h, wu_ref[:, cols])
        act_ref[:, cols] = (a * jax.nn.sigmoid(a) * b).astype(BF16)
    y = x + FFN_RESIDUAL * _dot(act_ref[...], wd_ref[...])
    if final_norm:
        y = _rms(y, fnorm_ref[...])

    @pl.when(is_prompt)
    def _():
        op_ref[...] = y

    @pl.when(jnp.logical_not(is_prompt))
    def _():
        os_ref[...] = y

    for src, dst in zip(cast_in, cast_out):
        dst[...] = src[...].astype(BF16)


def _cast_block_rows(rows, n_steps):
    br = BF16_SUBLANES
    while rows % br or rows // br > n_steps:
        br += BF16_SUBLANES
    return br


def _ffn_half(xp, xs, norm, wg, wu, wd, fnorm, *, final_norm, to_cast=()):
    n_prompt_tiles = xp.shape[0] // FFN_ROWS
    assert xp.shape[0] == n_prompt_tiles * FFN_ROWS and xs.shape[0] == FFN_ROWS
    row_block = (FFN_ROWS, D_MODEL)
    prompt_spec = pl.BlockSpec(row_block, lambda i: (jnp.minimum(i, n_prompt_tiles - 1), 0))
    sample_spec = pl.BlockSpec(row_block, lambda i: (0, 0))
    cast_specs, cast_shapes = [], []
    for w in to_cast:
        br = _cast_block_rows(w.shape[0], n_prompt_tiles)
        last = w.shape[0] // br - 1
        cast_specs.append(pl.BlockSpec((br, w.shape[1]), functools.partial(
            lambda i, last: (jnp.minimum(i, last), 0), last=last)))
        cast_shapes.append(jax.ShapeDtypeStruct(w.shape, BF16))
    return pl.pallas_call(
        functools.partial(_ffn_body, n_prompt_tiles=n_prompt_tiles, final_norm=final_norm, n_cast=len(to_cast)),
        grid=(n_prompt_tiles + 1,),
        in_specs=[prompt_spec, sample_spec, _resident((1, D_MODEL)), _resident((D_MODEL, D_FF)),
                  _resident((D_MODEL, D_FF)), _resident((D_FF, D_MODEL)), _resident((1, D_MODEL))] + cast_specs,
        out_specs=[prompt_spec, sample_spec] + cast_specs,
        out_shape=[jax.ShapeDtypeStruct(xp.shape, F32), jax.ShapeDtypeStruct(xs.shape, F32)] + cast_shapes,
        scratch_shapes=[pltpu.VMEM((FFN_ROWS, D_FF), BF16)],
        compiler_params=pltpu.CompilerParams(dimension_semantics=("arbitrary",), vmem_limit_bytes=VMEM_LIMIT),
        name="ffn_half_final" if final_norm else "ffn_half",
    )(xp, xs, norm, wg, wu, wd, fnorm, *to_cast)


def _pair_heads(t):
    swapped = pltpu.roll(t, HEAD_DIM, 1)
    left = lax.broadcasted_iota(jnp.int32, t.shape, 1) < HEAD_DIM
    return jnp.where(left, t, swapped), jnp.where(left, swapped, t)


def _gate_norm(g, vnorm_ref, gi):
    gg = g[:, gi * GROUP_CH:(gi + 1) * GROUP_CH]
    return _rms(gg, vnorm_ref[gi:gi + 1, :])


def _merge(x, attn, gate, anorm_ref, gnorm_ref, wout_ref):
    cat = jnp.concatenate([_rms(attn, anorm_ref[...]), _rms(gate, gnorm_ref[...])], axis=1)
    return x + _dot(cat.astype(BF16), wout_ref[...])


def _mixer_prompt_body(x_ref, mnorm_ref, win_ref, sinks_ref, vnorm_ref, ws_ref, bst_ref, anorm_ref, gnorm_ref,
                       wout_ref, o_ref, pk_ref, pv_ref, kv_s, q_s, attn_s, gate_s, bias_s, wtril_s):
    b = pl.program_id(0)
    t = pl.program_id(1)
    nblk = MIX_ROWS // WINDOW
    band = 2 * WINDOW
    qrows = GQA_GROUP * WINDOW

    @pl.when((b == 0) & (t == 0))
    def _init_tables():
        row = lax.broadcasted_iota(jnp.int32, (qrows, band), 0)
        key = lax.broadcasted_iota(jnp.int32, (qrows, band), 1)
        grp = row >> 7
        dist = WINDOW + (row & (WINDOW - 1)) - key
        valid = (dist >= 0) & (dist < WINDOW)
        distf = dist.astype(F32)
        for h in range(N_KV_HEADS):
            s = [ALIBI_SLOPES[h * GQA_GROUP + g] for g in range(GQA_GROUP)]
            slope = jnp.where(grp == 0, s[0], jnp.where(grp == 1, s[1], jnp.where(grp == 2, s[2], s[3])))
            bias = jnp.where(valid, -(slope * distf) * LOG2E, MASK_VALUE)
            bias_s[0, h] = bias
            bias_s[1, h] = jnp.where(key < WINDOW, MASK_VALUE, bias)
        r = lax.broadcasted_iota(jnp.int32, (CHUNK, CHUNK), 0)
        c = lax.broadcasted_iota(jnp.int32, (CHUNK, CHUNK), 1)
        for gi in range(N_GATE_GROUPS):
            wtril_s[gi] = jnp.where(r >= c, ws_ref[gi], 0.0).astype(BF16)

    @pl.when(t == 0)
    def _no_previous_block():
        kv_s[0:WINDOW, :] = jnp.zeros((WINDOW, 4 * KV_WIDTH), BF16)

    x = x_ref[...]
    h_in = _rms(x, mnorm_ref[...]).astype(BF16)
    q_s[...] = _dot(h_in, win_ref[:, 0:ATTN_WIDTH]) * (ATTN_SCALE * LOG2E)
    pkv = _dot(h_in, win_ref[:, ATTN_WIDTH:ATTN_WIDTH + 2 * KV_WIDTH])
    k_new = pkv[:, 0:KV_WIDTH]
    v_new = pkv[:, KV_WIDTH:2 * KV_WIDTH]
    pk_ref[0] = k_new[MIX_ROWS - WINDOW:, :]
    pv_ref[0] = v_new[MIX_ROWS - WINDOW:, :]
    k0, k1 = _pair_heads(k_new)
    v0, v1 = _pair_heads(v_new)
    kv_s[WINDOW:, :] = jnp.concatenate([k0, k1, v0, v1], axis=1).astype(BF16)

    table = jnp.where(t == 0, 1, 0)
    lane_grp = lax.broadcasted_iota(jnp.int32, (WINDOW, 2 * KV_WIDTH), 1) >> 6
    for j in range(nblk):
        rows = slice(j * WINDOW, (j + 1) * WINDOW)
        kvb = kv_s[j * WINDOW:j * WINDOW + band, :]
        for h in range(N_KV_HEADS):
            ka = kvb[:, h * KV_WIDTH:(h + 1) * KV_WIDTH]
            va = kvb[:, (2 + h) * KV_WIDTH:(3 + h) * KV_WIDTH]
            k4 = jnp.concatenate([ka, ka], axis=1)
            v4 = jnp.concatenate([va, va], axis=1)
            q = q_s[rows, h * 2 * KV_WIDTH:(h + 1) * 2 * KV_WIDTH]
            qstack = jnp.concatenate([jnp.where(lane_grp == g, q, 0.0) for g in range(GQA_GROUP)],
                                     axis=0).astype(BF16)
            bias = bias_s[table, h] if j == 0 else bias_s[0, h]
            s = _dot_nt(qstack, k4) + bias
            probs, inv = [], []
            for g in range(GQA_GROUP):
                sg = s[g * WINDOW:(g + 1) * WINDOW, :]
                sink = sinks_ref[h * GQA_GROUP + g] * LOG2E
                m = jnp.maximum(jnp.max(sg, axis=1, keepdims=True), sink)
                p = jnp.exp2(sg - m)
                inv.append(1.0 / (jnp.sum(p, axis=1, keepdims=True) + jnp.exp2(sink - m)))
                probs.append(p.astype(BF16))
            o = _dot(jnp.concatenate(probs, axis=0), v4)
            og = [o[g * WINDOW:(g + 1) * WINDOW, :] * inv[g] for g in range(GQA_GROUP)]
            attn_s[rows, h * 2 * KV_WIDTH:(h + 1) * 2 * KV_WIDTH] = jnp.where(
                lane_grp == 0, og[0], jnp.where(lane_grp == 1, og[1], jnp.where(lane_grp == 2, og[2], og[3])))
    kv_s[0:WINDOW, :] = kv_s[MIX_ROWS:MIX_ROWS + WINDOW, :]

    u = jax.nn.gelu(_dot(h_in, win_ref[:, 768:1280]))
    gact = jax.nn.gelu(_dot(h_in, win_ref[:, 1280:1792]))
    for gi in range(N_GATE_GROUPS):
        lanes = slice(gi * GROUP_CH, (gi + 1) * GROUP_CH)
        gn = _gate_norm(gact, vnorm_ref, gi).astype(BF16)
        chunks = jnp.concatenate([gn[c * CHUNK:(c + 1) * CHUNK, :] for c in range(nblk)], axis=1)
        mixed = _dot(wtril_s[gi], chunks) + bst_ref[:, gi:gi + 1]
        for c in range(nblk):
            rows = slice(c * CHUNK, (c + 1) * CHUNK)
            gate_s[rows, lanes] = u[rows, lanes] * mixed[:, c * GROUP_CH:(c + 1) * GROUP_CH]

    o_ref[...] = _merge(x, attn_s[...], gate_s[...], anorm_ref, gnorm_ref, wout_ref)


def _mixer_prompt(x, mnorm, win, sinks, vnorm, ws, bst, anorm, gnorm, wout, *, batch):
    rows = x.shape[0]
    tiles = rows // (batch * MIX_ROWS)
    assert rows == batch * tiles * MIX_ROWS
    band = 2 * WINDOW
    row_spec = pl.BlockSpec((MIX_ROWS, D_MODEL), lambda b, t: (b * tiles + t, 0))
    kv_spec = pl.BlockSpec((1, WINDOW, KV_WIDTH), lambda b, t: (b, 0, 0))
    kv_shape = jax.ShapeDtypeStruct((batch, WINDOW, KV_WIDTH), F32)
    return pl.pallas_call(
        _mixer_prompt_body,
        grid=(batch, tiles),
        in_specs=[row_spec, _resident((1, D_MODEL)), _resident(win.shape), _smem(), _resident(vnorm.shape),
                  _resident(ws.shape), _resident(bst.shape), _resident(anorm.shape), _resident(gnorm.shape),
                  _resident(wout.shape)],
        out_specs=[row_spec, kv_spec, kv_spec],
        out_shape=[jax.ShapeDtypeStruct(x.shape, F32), kv_shape, kv_shape],
        scratch_shapes=[
            pltpu.VMEM((WINDOW + MIX_ROWS, 4 * KV_WIDTH), BF16),
            pltpu.VMEM((MIX_ROWS, ATTN_WIDTH), F32),
            pltpu.VMEM((MIX_ROWS, ATTN_WIDTH), F32),
            pltpu.VMEM((MIX_ROWS, GMLP_WIDTH), F32),
            pltpu.VMEM((2, N_KV_HEADS, GQA_GROUP * WINDOW, band), F32),
            pltpu.VMEM((N_GATE_GROUPS, CHUNK, CHUNK), BF16),
        ],
        compiler_params=pltpu.CompilerParams(dimension_semantics=("arbitrary", "arbitrary"),
                                             vmem_limit_bytes=VMEM_LIMIT),
        name="mixer_prompt",
    )(x, mnorm, win, sinks, vnorm, ws, bst, anorm, gnorm, wout)


def _mixer_sample_body(x_ref, mnorm_ref, win_ref, sinks_ref, vnorm_ref, wsm_ref, bsm_ref, anorm_ref, gnorm_ref,
                       wout_ref, ck_ref, cv_ref, o_ref, gn_ref, sk_ref, sv_ref,
                       qall_s, oall_s, knew_s, vnew_s, gate_s, bias_s, *, n_seq, n_tok):
    step = pl.program_id(0)
    n_steps = pl.num_programs(0)
    n_rows = n_seq * n_tok
    q_rows = N_HEADS * n_tok
    slots = 8

    @pl.when(step == 0)
    def _project():
        x = x_ref[...]
        h_in = _rms(x, mnorm_ref[...]).astype(BF16)
        pq = _dot(h_in, win_ref[:, 0:ATTN_WIDTH]) * ATTN_SCALE
        pkv = _dot(h_in, win_ref[:, ATTN_WIDTH:ATTN_WIDTH + 2 * KV_WIDTH])
        zeros = jnp.zeros(((slots - n_tok) * n_seq, KV_WIDTH), F32)
        knew_s[0:(slots - n_tok) * n_seq, :] = zeros
        vnew_s[0:(slots - n_tok) * n_seq, :] = zeros
        knew_s[(slots - n_tok) * n_seq:, :] = pkv[:, 0:KV_WIDTH]
        vnew_s[(slots - n_tok) * n_seq:, :] = pkv[:, KV_WIDTH:2 * KV_WIDTH]
        half = lax.broadcasted_iota(jnp.int32, (n_rows, LANES), 1) >> 6
        for hg in range(N_HEADS):
            kvh = hg // GQA_GROUP
            slab = pq[:, (hg // 2) * LANES:(hg // 2 + 1) * LANES]
            if (hg % 2) != kvh:
                slab = pltpu.roll(slab, HEAD_DIM, 1)
            qall_s[hg * n_rows:(hg + 1) * n_rows, :] = jnp.where(half == kvh, slab, 0.0)

        u = jax.nn.gelu(_dot(h_in, win_ref[:, 768:1280]))
        gact = jax.nn.gelu(_dot(h_in, win_ref[:, 1280:1792]))
        for gi in range(N_GATE_GROUPS):
            lanes = slice(gi * GROUP_CH, (gi + 1) * GROUP_CH)
            gn = _gate_norm(gact, vnorm_ref, gi)
            gn_ref[:, lanes] = gn
            for i in range(n_tok):
                mixed = jnp.full((n_seq, GROUP_CH), bsm_ref[gi * n_tok + i], F32)
                for j in range(i + 1):
                    mixed = mixed + wsm_ref[(gi * n_tok + i) * n_tok + j] * gn[j * n_seq:(j + 1) * n_seq, :]
                rows = slice(i * n_seq, (i + 1) * n_seq)
                gate_s[rows, lanes] = u[rows, lanes] * mixed

        row = lax.broadcasted_iota(jnp.int32, (q_rows, SAMPLE_KEYS), 0)
        col = lax.broadcasted_iota(jnp.int32, (q_rows, SAMPLE_KEYS), 1)
        tok = row & (n_tok - 1)
        head = row >> 2
        slope = jnp.zeros((q_rows, SAMPLE_KEYS), F32)
        sink = jnp.zeros((q_rows, SAMPLE_KEYS), F32)
        for hg in range(N_HEADS):
            slope = jnp.where(head == hg, ALIBI_SLOPES[hg], slope)
            sink = jnp.where(head == hg, sinks_ref[hg], sink)
        new_tok = col - (WINDOW + slots - n_tok)
        is_cache = col < WINDOW
        is_new = (new_tok >= 0) & (new_tok < n_tok)
        dist = jnp.where(is_cache, WINDOW + tok - col, tok - new_tok)
        valid = (is_cache | is_new) & (dist >= 0) & (dist < WINDOW)
        bias = jnp.where(valid, -(slope * dist.astype(F32)), MASK_VALUE)
        bias_s[...] = jnp.where(col == WINDOW, sink, bias)

    pad = jnp.zeros((SAMPLE_KEYS - WINDOW - slots, KV_WIDTH), F32)
    sub = lax.broadcasted_iota(jnp.int32, (slots, KV_WIDTH), 0)

    def _shifted(cache, new8):
        rolled = pltpu.roll(cache, WINDOW - n_tok, 0)
        tail = jnp.where(sub >= slots - n_tok, new8, rolled[WINDOW - slots:, :])
        return jnp.concatenate([rolled[0:WINDOW - slots, :], tail], axis=0)

    def _one_sequence(bl, carry):
        bb = step * SEQ_PER_STEP + bl
        qb = qall_s[pl.ds(bb, q_rows, stride=n_seq), :].astype(BF16)
        kc = ck_ref[bl]
        vc = cv_ref[bl]
        kn = knew_s[pl.ds(bb, slots, stride=n_seq), :]
        vn = vnew_s[pl.ds(bb, slots, stride=n_seq), :]
        kext = jnp.concatenate([kc, kn, pad], axis=0).astype(BF16)
        vext = jnp.concatenate([vc, vn, pad], axis=0).astype(BF16)
        s = _dot_nt(qb, kext) + bias_s[...]
        m = jnp.max(s, axis=1, keepdims=True)
        p = jnp.exp(s - m)
        inv = 1.0 / jnp.sum(p, axis=1, keepdims=True)
        oall_s[pl.ds(bb, q_rows, stride=n_seq), :] = _dot(p.astype(BF16), vext) * inv
        sk_ref[bl] = _shifted(kc, kn)
        sv_ref[bl] = _shifted(vc, vn)
        return carry

    lax.fori_loop(0, SEQ_PER_STEP, _one_sequence, 0)

    @pl.when(step == n_steps - 1)
    def _merge_rows():
        left = lax.broadcasted_iota(jnp.int32, (n_rows, LANES), 1) < HEAD_DIM
        slabs = []
        for pair in range(N_HEADS // 2):
            halves = []
            for hg in (2 * pair, 2 * pair + 1):
                o = oall_s[hg * n_rows:(hg + 1) * n_rows, :]
                if (hg % 2) != (hg // GQA_GROUP):
                    o = pltpu.roll(o, HEAD_DIM, 1)
                halves.append(o)
            slabs.append(jnp.where(left, halves[0], halves[1]))
        attn = jnp.concatenate(slabs, axis=1)
        o_ref[...] = _merge(x_ref[...], attn, gate_s[...], anorm_ref, gnorm_ref, wout_ref)


def _mixer_sample(x, mnorm, win, sinks, vnorm, wsm, bsm, anorm, gnorm, wout, ck, cv, *, n_seq, n_tok):
    n_rows = n_seq * n_tok
    assert x.shape[0] == n_rows and n_seq % SEQ_PER_STEP == 0 and n_tok == 4
    cache_spec = pl.BlockSpec((SEQ_PER_STEP, WINDOW, KV_WIDTH), lambda s: (s, 0, 0))
    cache_shape = jax.ShapeDtypeStruct(ck.shape, F32)
    rows_spec = pl.BlockSpec((n_rows, D_MODEL), lambda s: (0, 0))
    gn_spec = pl.BlockSpec((n_rows, GMLP_WIDTH), lambda s: (0, 0))
    return pl.pallas_call(
        functools.partial(_mixer_sample_body, n_seq=n_seq, n_tok=n_tok),
        grid=(n_seq // SEQ_PER_STEP,),
        in_specs=[rows_spec, _resident((1, D_MODEL)), _resident(win.shape), _smem(), _resident(vnorm.shape),
                  _smem(), _smem(), _resident(anorm.shape), _resident(gnorm.shape), _resident(wout.shape),
                  cache_spec, cache_spec],
        out_specs=[rows_spec, gn_spec, cache_spec, cache_spec],
        out_shape=[jax.ShapeDtypeStruct(x.shape, F32), jax.ShapeDtypeStruct((n_rows, GMLP_WIDTH), F32),
                   cache_shape, cache_shape],
        scratch_shapes=[
            pltpu.VMEM((N_HEADS * n_rows, LANES), F32),
            pltpu.VMEM((N_HEADS * n_rows, LANES), F32),
            pltpu.VMEM((8 * n_seq, KV_WIDTH), F32),
            pltpu.VMEM((8 * n_seq, KV_WIDTH), F32),
            pltpu.VMEM((n_rows, GMLP_WIDTH), F32),
            pltpu.VMEM((N_HEADS * n_tok, SAMPLE_KEYS), F32),
        ],
        compiler_params=pltpu.CompilerParams(dimension_semantics=("arbitrary",), vmem_limit_bytes=VMEM_LIMIT),
        name="mixer_sample",
    )(x, mnorm, win, sinks, vnorm, wsm, bsm, anorm, gnorm, wout, ck, cv)


def kernel(x_prompt, x_sample, cache_k, cache_v, ffn1_norm, ffn1_w_gate, ffn1_w_up, ffn1_w_down, mix_norm, w_in,
           attn_sinks, gmlp_v_norm, gmlp_w_spatial, gmlp_b_spatial, attn_out_norm, gmlp_out_norm, w_out, ffn2_norm,
           ffn2_w_gate, ffn2_w_up, ffn2_w_down, final_norm):
    batch, seq, _ = x_prompt.shape
    n_seq, n_tok, _ = x_sample.shape
    depth = cache_k.shape[0]
    assert depth == 1

    xp = x_prompt.reshape(batch * seq, D_MODEL)
    xs = jnp.transpose(x_sample, (1, 0, 2)).reshape(n_tok * n_seq, D_MODEL)
    row = lambda v: v.reshape(1, -1)
    fnorm = row(final_norm)
    l = 0
    wg1, wu1, wd1 = (w[l].astype(BF16) for w in (ffn1_w_gate, ffn1_w_up, ffn1_w_down))
    sinks = attn_sinks[l]
    vnorm = gmlp_v_norm[l]
    ws = gmlp_w_spatial[l]
    bs = gmlp_b_spatial[l]
    mnorm, anorm, gnorm = row(mix_norm[l]), row(attn_out_norm[l]), row(gmlp_out_norm[l])

    later = (ffn2_w_gate[l], ffn2_w_up[l], ffn2_w_down[l], w_in[l], w_out[l])
    xp, xs, wg2, wu2, wd2, win, wout = _ffn_half(xp, xs, row(ffn1_norm[l]), wg1, wu1, wd1, fnorm, final_norm=False,
                                                 to_cast=later)
    xp, pk, pv = _mixer_prompt(xp, mnorm, win, sinks, vnorm, ws, bs.T, anorm, gnorm, wout, batch=batch)
    ck = cache_k[l].reshape(n_seq, WINDOW, KV_WIDTH)
    cv = cache_v[l].reshape(n_seq, WINDOW, KV_WIDTH)
    xs, gn, sk, sv = _mixer_sample(xs, mnorm, win, sinks, vnorm, ws[:, :n_tok, :n_tok].reshape(-1),
                                   bs[:, :n_tok].reshape(-1), anorm, gnorm, wout, ck, cv, n_seq=n_seq, n_tok=n_tok)
    yp, ys = _ffn_half(xp, xs, row(ffn2_norm[l]), wg2, wu2, wd2, fnorm, final_norm=True)

    kv5 = lambda a, n: a.reshape(1, n, WINDOW, N_KV_HEADS, HEAD_DIM)
    y_prompt = yp.reshape(batch, seq, D_MODEL)
    y_sample = jnp.transpose(ys.reshape(n_tok, n_seq, D_MODEL), (1, 0, 2))
    chunk_v = jnp.transpose(gn.reshape(n_tok, n_seq, GMLP_WIDTH), (1, 0, 2))[None]
    return (y_prompt, y_sample, kv5(pk, batch), kv5(pv, batch), kv5(sk, n_seq), kv5(sv, n_seq), chunk_v)
```

```python
import functools

import jax
import jax.numpy as jnp
from jax import lax
from jax.experimental import pallas as pl
from jax.experimental.pallas import tpu as pltpu

F32 = jnp.float32
BF16 = jnp.bfloat16

D_MODEL = 1024
D_FF = 2816
N_HEADS = 8
N_KV_HEADS = 2
GQA_GROUP = 4
HEAD_DIM = 64
ATTN_WIDTH = 512
KV_WIDTH = 128
GMLP_WIDTH = 512
N_GATE_GROUPS = 4
GROUP_CH = 128
WINDOW = 128
CHUNK = 128
RMS_EPS = 1e-6
FFN_RESIDUAL = 0.5
ATTN_SCALE = 0.125
LOG2E = 1.4426950408889634
ALIBI_SLOPES = tuple(2.0 ** (-(i + 1)) for i in range(N_HEADS))
MASK_VALUE = -1e30

LANES = 128
BF16_SUBLANES = 16
FFN_ROWS = 512
FFN_CHUNK = 256
MIX_ROWS = 512
SEQ_PER_STEP = 32
VMEM_LIMIT = 56 * 1024 * 1024


def _rms(x, g):
    ms = jnp.mean(x * x, axis=-1, keepdims=True)
    return x * lax.rsqrt(ms + RMS_EPS) * g


def _dot(a, b):
    return jnp.dot(a, b, preferred_element_type=F32)


def _dot_nt(a, b):
    return lax.dot_general(a, b, (((1,), (1,)), ((), ())), preferred_element_type=F32)


def _resident(shape):
    nd = len(shape)
    return pl.BlockSpec(shape, lambda *_: (0,) * nd, pipeline_mode=pl.Buffered(1))


def _smem():
    return pl.BlockSpec(memory_space=pltpu.SMEM)


def _ffn_body(*refs, n_prompt_tiles, final_norm, n_cast):
    xp_ref, xs_ref, norm_ref, wg_ref, wu_ref, wd_ref, fnorm_ref = refs[:7]
    cast_in = refs[7:7 + n_cast]
    op_ref, os_ref = refs[7 + n_cast:9 + n_cast]
    cast_out = refs[9 + n_cast:9 + 2 * n_cast]
    act_ref = refs[9 + 2 * n_cast]
    i = pl.program_id(0)
    for src, dst in zip(cast_in, cast_out):
        dst[...] = src[...].astype(BF16)

    x = jnp.where(i == 0, xs_ref[...], xp_ref[...])
    h = _rms(x, norm_ref[...]).astype(BF16)
    for c in range(D_FF // FFN_CHUNK):
        cols = slice(c * FFN_CHUNK, (c + 1) * FFN_CHUNK)
        a = _dot(h, wg_ref[:, cols])
        b = _dot(h, wu_ref[:, cols])
        act_ref[:, cols] = (a * jax.nn.sigmoid(a) * b).astype(BF16)
    y = x + FFN_RESIDUAL * _dot(act_ref[...], wd_ref[...])
    if final_norm:
        y = _rms(y, fnorm_ref[...])
    op_ref[...] = y

    @pl.when(i == 0)
    def _():
        os_ref[...] = op_ref[...]


def _cast_block_rows(rows, n_steps):
    br = BF16_SUBLANES
    while rows % br or rows // br > n_steps:
        br += BF16_SUBLANES
    return br


def _ffn_half(xp, xs, norm, wg, wu, wd, fnorm, *, final_norm, to_cast=()):
    n_prompt_tiles = xp.shape[0] // FFN_ROWS
    assert xp.shape[0] == n_prompt_tiles * FFN_ROWS and xs.shape[0] == FFN_ROWS
    row_block = (FFN_ROWS, D_MODEL)
    prompt_spec = pl.BlockSpec(row_block, lambda i: (jnp.maximum(i - 1, 0), 0))
    sample_spec = pl.BlockSpec(row_block, lambda i: (0, 0))
    cast_specs, cast_shapes = [], []
    for w in to_cast:
        br = _cast_block_rows(w.shape[0], n_prompt_tiles)
        last = w.shape[0] // br - 1
        cast_specs.append(pl.BlockSpec((br, w.shape[1]), functools.partial(
            lambda i, last: (jnp.minimum(i, last), 0), last=last)))
        cast_shapes.append(jax.ShapeDtypeStruct(w.shape, BF16))
    return pl.pallas_call(
        functools.partial(_ffn_body, n_prompt_tiles=n_prompt_tiles, final_norm=final_norm, n_cast=len(to_cast)),
        grid=(n_prompt_tiles + 1,),
        in_specs=[prompt_spec, sample_spec, _resident((1, D_MODEL)), _resident((D_MODEL, D_FF)),
                  _resident((D_MODEL, D_FF)), _resident((D_FF, D_MODEL)), _resident((1, D_MODEL))] + cast_specs,
        out_specs=[prompt_spec, sample_spec] + cast_specs,
        out_shape=[jax.ShapeDtypeStruct(xp.shape, F32), jax.ShapeDtypeStruct(xs.shape, F32)] + cast_shapes,
        scratch_shapes=[pltpu.VMEM((FFN_ROWS, D_FF), BF16)],
        compiler_params=pltpu.CompilerParams(dimension_semantics=("arbitrary",), vmem_limit_bytes=VMEM_LIMIT),
        name="ffn_half_final" if final_norm else "ffn_half",
    )(xp, xs, norm, wg, wu, wd, fnorm, *to_cast)


def _pair_heads(t):
    swapped = pltpu.roll(t, HEAD_DIM, 1)
    left = lax.broadcasted_iota(jnp.int32, t.shape, 1) < HEAD_DIM
    return jnp.where(left, t, swapped), jnp.where(left, swapped, t)


def _gate_norm(g, vnorm_ref, gi):
    gg = g[:, gi * GROUP_CH:(gi + 1) * GROUP_CH]
    return _rms(gg, vnorm_ref[gi:gi + 1, :])


def _merge(x, attn, gate, anorm_ref, gnorm_ref, wout_ref):
    cat = jnp.concatenate([_rms(attn, anorm_ref[...]), _rms(gate, gnorm_ref[...])], axis=1)
    return x + _dot(cat.astype(BF16), wout_ref[...])


def _mixer_prompt_body(x_ref, mnorm_ref, win_ref, sinks_ref, vnorm_ref, ws_ref, bst_ref, anorm_ref, gnorm_ref,
                       wout_ref, o_ref, pk_ref, pv_ref, kv_s, q_s, attn_s, gate_s, bias_s, wtril_s):
    b = pl.program_id(0)
    t = pl.program_id(1)
    nblk = MIX_ROWS // WINDOW
    band = 2 * WINDOW
    qrows = GQA_GROUP * WINDOW

    @pl.when((b == 0) & (t == 0))
    def _init_tables():
        row = lax.broadcasted_iota(jnp.int32, (qrows, band), 0)
        key = lax.broadcasted_iota(jnp.int32, (qrows, band), 1)
        grp = row >> 7
        dist = WINDOW + (row & (WINDOW - 1)) - key
        valid = (dist >= 0) & (dist < WINDOW)
        distf = dist.astype(F32)
        for h in range(N_KV_HEADS):
            s = [ALIBI_SLOPES[h * GQA_GROUP + g] for g in range(GQA_GROUP)]
            slope = jnp.where(grp == 0, s[0], jnp.where(grp == 1, s[1], jnp.where(grp == 2, s[2], s[3])))
            bias = jnp.where(valid, -(slope * distf) * LOG2E, MASK_VALUE)
            bias_s[0, h] = bias
            bias_s[1, h] = jnp.where(key < WINDOW, MASK_VALUE, bias)
        r = lax.broadcasted_iota(jnp.int32, (CHUNK, CHUNK), 0)
        c = lax.broadcasted_iota(jnp.int32, (CHUNK, CHUNK), 1)
        for gi in range(N_GATE_GROUPS):
            wtril_s[gi] = jnp.where(r >= c, ws_ref[gi], 0.0).astype(BF16)

    @pl.when(t == 0)
    def _no_previous_block():
        kv_s[0:WINDOW, :] = jnp.zeros((WINDOW, 4 * KV_WIDTH), BF16)

    x = x_ref[...]
    h_in = _rms(x, mnorm_ref[...]).astype(BF16)
    q_s[...] = _dot(h_in, win_ref[:, 0:ATTN_WIDTH]) * (ATTN_SCALE * LOG2E)
    pkv = _dot(h_in, win_ref[:, ATTN_WIDTH:ATTN_WIDTH + 2 * KV_WIDTH])
    k_new = pkv[:, 0:KV_WIDTH]
    v_new = pkv[:, KV_WIDTH:2 * KV_WIDTH]
    pk_ref[0] = k_new[MIX_ROWS - WINDOW:, :]
    pv_ref[0] = v_new[MIX_ROWS - WINDOW:, :]
    k0, k1 = _pair_heads(k_new)
    v0, v1 = _pair_heads(v_new)
    kv_s[WINDOW:, :] = jnp.concatenate([k0, k1, v0, v1], axis=1).astype(BF16)

    table = jnp.where(t == 0, 1, 0)
    lane_grp = lax.broadcasted_iota(jnp.int32, (WINDOW, 2 * KV_WIDTH), 1) >> 6
    for j in range(nblk):
        rows = slice(j * WINDOW, (j + 1) * WINDOW)
        kvb = kv_s[j * WINDOW:j * WINDOW + band, :]
        for h in range(N_KV_HEADS):
            ka = kvb[:, h * KV_WIDTH:(h + 1) * KV_WIDTH]
            va = kvb[:, (2 + h) * KV_WIDTH:(3 + h) * KV_WIDTH]
            k4 = jnp.concatenate([ka, ka], axis=1)
            v4 = jnp.concatenate([va, va], axis=1)
            q = q_s[rows, h * 2 * KV_WIDTH:(h + 1) * 2 * KV_WIDTH]
            qstack = jnp.concatenate([jnp.where(lane_grp == g, q, 0.0) for g in range(GQA_GROUP)],
                                     axis=0).astype(BF16)
            bias = bias_s[table, h] if j == 0 else bias_s[0, h]
            s = _dot_nt(qstack, k4) + bias
            probs, inv = [], []
            for g in range(GQA_GROUP):
                sg = s[g * WINDOW:(g + 1) * WINDOW, :]
                sink = sinks_ref[h * GQA_GROUP + g] * LOG2E
                m = jnp.maximum(jnp.max(sg, axis=1, keepdims=True), sink)
                p = jnp.exp2(sg - m)
                inv.append(1.0 / (jnp.sum(p, axis=1, keepdims=True) + jnp.exp2(sink - m)))
                probs.append(p.astype(BF16))
            o = _dot(jnp.concatenate(probs, axis=0), v4)
            og = [o[g * WINDOW:(g + 1) * WINDOW, :] * inv[g] for g in range(GQA_GROUP)]
            attn_s[rows, h * 2 * KV_WIDTH:(h + 1) * 2 * KV_WIDTH] = jnp.where(
                lane_grp == 0, og[0], jnp.where(lane_grp == 1, og[1], jnp.where(lane_grp == 2, og[2], og[3])))
    kv_s[0:WINDOW, :] = kv_s[MIX_ROWS:MIX_ROWS + WINDOW, :]

    u = jax.nn.gelu(_dot(h_in, win_ref[:, 768:1280]))
    gact = jax.nn.gelu(_dot(h_in, win_ref[:, 1280:1792]))
    for gi in range(N_GATE_GROUPS):
        lanes = slice(gi * GROUP_CH, (gi + 1) * GROUP_CH)
        gn = _gate_norm(gact, vnorm_ref, gi).astype(BF16)
        chunks = jnp.concatenate([gn[c * CHUNK:(c + 1) * CHUNK, :] for c in range(nblk)], axis=1)
        mixed = _dot(wtril_s[gi], chunks) + bst_ref[:, gi:gi + 1]
        for c in range(nblk):
            rows = slice(c * CHUNK, (c + 1) * CHUNK)
            gate_s[rows, lanes] = u[rows, lanes] * mixed[:, c * GROUP_CH:(c + 1) * GROUP_CH]

    o_ref[...] = _merge(x, attn_s[...], gate_s[...], anorm_ref, gnorm_ref, wout_ref)


def _mixer_prompt(x, mnorm, win, sinks, vnorm, ws, bst, anorm, gnorm, wout, *, batch):
    rows = x.shape[0]
    tiles = rows // (batch * MIX_ROWS)
    assert rows == batch * tiles * MIX_ROWS
    band = 2 * WINDOW
    row_spec = pl.BlockSpec((MIX_ROWS, D_MODEL), lambda b, t: (b * tiles + t, 0))
    kv_spec = pl.BlockSpec((1, WINDOW, KV_WIDTH), lambda b, t: (b, 0, 0))
    kv_shape = jax.ShapeDtypeStruct((batch, WINDOW, KV_WIDTH), F32)
    return pl.pallas_call(
        _mixer_prompt_body,
        grid=(batch, tiles),
        in_specs=[row_spec, _resident((1, D_MODEL)), _resident(win.shape), _smem(), _resident(vnorm.shape),
                  _resident(ws.shape), _resident(bst.shape), _resident(anorm.shape), _resident(gnorm.shape),
                  _resident(wout.shape)],
        out_specs=[row_spec, kv_spec, kv_spec],
        out_shape=[jax.ShapeDtypeStruct(x.shape, F32), kv_shape, kv_shape],
        scratch_shapes=[
            pltpu.VMEM((WINDOW + MIX_ROWS, 4 * KV_WIDTH), BF16),
            pltpu.VMEM((MIX_ROWS, ATTN_WIDTH), F32),
            pltpu.VMEM((MIX_ROWS, ATTN_WIDTH), F32),
            pltpu.VMEM((MIX_ROWS, GMLP_WIDTH), F32),
            pltpu.VMEM((2, N_KV_HEADS, GQA_GROUP * WINDOW, band), F32),
            pltpu.VMEM((N_GATE_GROUPS, CHUNK, CHUNK), BF16),
        ],
        compiler_params=pltpu.CompilerParams(dimension_semantics=("arbitrary", "arbitrary"),
                                             vmem_limit_bytes=VMEM_LIMIT),
        name="mixer_prompt",
    )(x, mnorm, win, sinks, vnorm, ws, bst, anorm, gnorm, wout)


def _mixer_sample_body(x_ref, mnorm_ref, win_ref, sinks_ref, vnorm_ref, wsm_ref, bsm_ref, anorm_ref, gnorm_ref,
                       wout_ref, ck_ref, cv_ref, o_ref, gn_ref, sk_ref, sv_ref,
                       perm_s, qall_s, oall_s, kvt_s, gate_s, bias_s, sink_s, s_s, p_s, *, n_seq, n_tok):
    step = pl.program_id(0)
    n_steps = pl.num_programs(0)
    n_rows = n_seq * n_tok
    q_rows = N_HEADS * n_tok
    keys = 2 * WINDOW
    step_rows = SEQ_PER_STEP * q_rows

    def by_token(val):
        for c in range(val.shape[1] // LANES):
            perm_s[c] = val[:, c * LANES:(c + 1) * LANES]
        return jnp.concatenate(
            [jnp.concatenate([perm_s[c, pl.ds(i, n_seq, stride=n_tok), :] for i in range(n_tok)], axis=0)
             for c in range(val.shape[1] // LANES)], axis=1)

    def by_sequence(val):
        for c in range(val.shape[1] // LANES):
            for i in range(n_tok):
                perm_s[c, pl.ds(i, n_seq, stride=n_tok), :] = val[i * n_seq:(i + 1) * n_seq, c * LANES:(c + 1) * LANES]
        return jnp.concatenate([perm_s[c] for c in range(val.shape[1] // LANES)], axis=1)

    @pl.when(step == 0)
    def _project():
        h = _rms(x_ref[...], mnorm_ref[...])
        pkv = _dot(h.astype(BF16), win_ref[:, ATTN_WIDTH:ATTN_WIDTH + 2 * KV_WIDTH])
        kvt_s[...] = pkv.T
        h_in = by_token(h).astype(BF16)
        pq = _dot(h_in, win_ref[:, 0:ATTN_WIDTH]) * (ATTN_SCALE * LOG2E)
        half = lax.broadcasted_iota(jnp.int32, (n_rows, LANES), 1) >> 6
        for hg in range(N_HEADS):
            kvh = hg // GQA_GROUP
            slab = pq[:, (hg // 2) * LANES:(hg // 2 + 1) * LANES]
            if (hg % 2) != kvh:
                slab = pltpu.roll(slab, HEAD_DIM, 1)
            qall_s[hg * n_rows:(hg + 1) * n_rows, :] = jnp.where(half == kvh, slab, 0.0)

        u = jax.nn.gelu(_dot(h_in, win_ref[:, 768:1280]))
        gact = jax.nn.gelu(_dot(h_in, win_ref[:, 1280:1792]))
        gns = []
        for gi in range(N_GATE_GROUPS):
            lanes = slice(gi * GROUP_CH, (gi + 1) * GROUP_CH)
            gn = _gate_norm(gact, vnorm_ref, gi)
            gns.append(gn)
            for i in range(n_tok):
                rows = slice(i * n_seq, (i + 1) * n_seq)
                mixed = jnp.full((n_seq, GROUP_CH), bsm_ref[gi * n_tok + i], F32)
                for j in range(i + 1):
                    mixed = mixed + wsm_ref[(gi * n_tok + i) * n_tok + j] * gn[j * n_seq:(j + 1) * n_seq, :]
                gate_s[rows, lanes] = u[rows, lanes] * mixed
        gn_ref[...] = by_sequence(jnp.concatenate(gns, axis=1))

        row = lax.broadcasted_iota(jnp.int32, (step_rows, keys), 0)
        col = lax.broadcasted_iota(jnp.int32, (step_rows, keys), 1)
        tok = row & (n_tok - 1)
        head = (row >> 2) & (N_HEADS - 1)
        slope = jnp.zeros((step_rows, keys), F32)
        for hg in range(N_HEADS):
            slope = jnp.where(head == hg, ALIBI_SLOPES[hg], slope)
        shifted = col < WINDOW
        pos = jnp.where(shifted, col + n_tok, col - WINDOW)
        dist = WINDOW + tok - pos
        valid = (dist >= 0) & (dist < WINDOW) & (shifted | (pos < n_tok))
        bias_s[...] = jnp.where(valid, -(slope * dist.astype(F32)) * LOG2E, MASK_VALUE)
        head1 = (lax.broadcasted_iota(jnp.int32, (step_rows, 1), 0) >> 2) & (N_HEADS - 1)
        sink = jnp.zeros((step_rows, 1), F32)
        for hg in range(N_HEADS):
            sink = jnp.where(head1 == hg, sinks_ref[hg] * LOG2E, sink)
        sink_s[...] = sink

    col0 = pl.multiple_of(step * LANES, LANES)
    k_cols = kvt_s[0:KV_WIDTH, pl.ds(col0, LANES)]
    v_cols = kvt_s[KV_WIDTH:2 * KV_WIDTH, pl.ds(col0, LANES)]
    is_new = lax.broadcasted_iota(jnp.int32, (KV_WIDTH, WINDOW), 1) >= WINDOW - n_tok
    for bl in range(SEQ_PER_STEP):
        to_tail = (WINDOW - n_tok - n_tok * bl) % LANES
        sk_ref[bl] = jnp.where(is_new, pltpu.roll(k_cols, to_tail, 1), pltpu.roll(ck_ref[bl], WINDOW - n_tok, 1))
        sv_ref[bl] = jnp.where(is_new, pltpu.roll(v_cols, to_tail, 1), pltpu.roll(cv_ref[bl], WINDOW - n_tok, 1))

    seq0 = step * SEQ_PER_STEP
    for bl in range(SEQ_PER_STEP):
        qb = qall_s[pl.ds(seq0 + bl, q_rows, stride=n_seq), :].astype(BF16)
        k2 = jnp.concatenate([sk_ref[bl], ck_ref[bl]], axis=1).astype(BF16)
        s_s[bl * q_rows:(bl + 1) * q_rows, :] = _dot(qb, k2)
    s = s_s[...] + bias_s[...]
    sink = sink_s[...]
    m = jnp.maximum(jnp.max(s, axis=1, keepdims=True), sink)
    p = jnp.exp2(s - m)
    inv = 1.0 / (jnp.sum(p, axis=1, keepdims=True) + jnp.exp2(sink - m))
    p_s[...] = p.astype(BF16)
    for bl in range(SEQ_PER_STEP):
        rows = slice(bl * q_rows, (bl + 1) * q_rows)
        v2 = jnp.concatenate([sv_ref[bl], cv_ref[bl]], axis=1).astype(BF16)
        oall_s[pl.ds(seq0 + bl, q_rows, stride=n_seq), :] = _dot_nt(p_s[rows, :], v2) * inv[rows, :]

    @pl.when(step == n_steps - 1)
    def _merge_rows():
        left = lax.broadcasted_iota(jnp.int32, (n_rows, LANES), 1) < HEAD_DIM
        slabs = []
        for pair in range(N_HEADS // 2):
            halves = []
            for hg in (2 * pair, 2 * pair + 1):
                o = oall_s[hg * n_rows:(hg + 1) * n_rows, :]
                if (hg % 2) != (hg // GQA_GROUP):
                    o = pltpu.roll(o, HEAD_DIM, 1)
                halves.append(o)
            slabs.append(jnp.where(left, halves[0], halves[1]))
        attn = jnp.concatenate(slabs, axis=1)
        o_ref[...] = by_sequence(_merge(by_token(x_ref[...]), attn, gate_s[...], anorm_ref, gnorm_ref, wout_ref))


def _mixer_sample(x, mnorm, win, sinks, vnorm, wsm, bsm, anorm, gnorm, wout, ck, cv, *, n_seq, n_tok):
    n_rows = n_seq * n_tok
    q_rows = N_HEADS * n_tok
    assert x.shape[0] == n_rows and n_seq % SEQ_PER_STEP == 0 and n_tok == 4 and n_seq == LANES
    assert SEQ_PER_STEP * n_tok == LANES
    cache_spec = pl.BlockSpec((SEQ_PER_STEP, KV_WIDTH, WINDOW), lambda s: (s, 0, 0))
    cache_shape = jax.ShapeDtypeStruct(ck.shape, F32)
    rows_spec = pl.BlockSpec((n_rows, D_MODEL), lambda s: (0, 0))
    gn_spec = pl.BlockSpec((n_rows, GMLP_WIDTH), lambda s: (0, 0))
    return pl.pallas_call(
        functools.partial(_mixer_sample_body, n_seq=n_seq, n_tok=n_tok),
        grid=(n_seq // SEQ_PER_STEP,),
        in_specs=[rows_spec, _resident((1, D_MODEL)), _resident(win.shape), _smem(), _resident(vnorm.shape),
                  _smem(), _smem(), _resident(anorm.shape), _resident(gnorm.shape), _resident(wout.shape),
                  cache_spec, cache_spec],
        out_specs=[rows_spec, gn_spec, cache_spec, cache_spec],
        out_shape=[jax.ShapeDtypeStruct(x.shape, F32), jax.ShapeDtypeStruct((n_rows, GMLP_WIDTH), F32),
                   cache_shape, cache_shape],
        scratch_shapes=[
            pltpu.VMEM((D_MODEL // LANES, n_rows, LANES), F32),
            pltpu.VMEM((N_HEADS * n_rows, LANES), F32),
            pltpu.VMEM((N_HEADS * n_rows, LANES), F32),
            pltpu.VMEM((2 * KV_WIDTH, n_rows), F32),
            pltpu.VMEM((n_rows, GMLP_WIDTH), F32),
            pltpu.VMEM((SEQ_PER_STEP * q_rows, 2 * WINDOW), F32),
            pltpu.VMEM((SEQ_PER_STEP * q_rows, 1), F32),
            pltpu.VMEM((SEQ_PER_STEP * q_rows, 2 * WINDOW), F32),
            pltpu.VMEM((SEQ_PER_STEP * q_rows, 2 * WINDOW), BF16),
        ],
        compiler_params=pltpu.CompilerParams(dimension_semantics=("arbitrary",), vmem_limit_bytes=VMEM_LIMIT),
        name="mixer_sample",
    )(x, mnorm, win, sinks, vnorm, wsm, bsm, anorm, gnorm, wout, ck, cv)


def kernel(x_prompt, x_sample, cache_k, cache_v, ffn1_norm, ffn1_w_gate, ffn1_w_up, ffn1_w_down, mix_norm, w_in,
           attn_sinks, gmlp_v_norm, gmlp_w_spatial, gmlp_b_spatial, attn_out_norm, gmlp_out_norm, w_out, ffn2_norm,
           ffn2_w_gate, ffn2_w_up, ffn2_w_down, final_norm):
    batch, seq, _ = x_prompt.shape
    n_seq, n_tok, _ = x_sample.shape
    depth = cache_k.shape[0]
    assert depth == 1

    xp = x_prompt.reshape(batch * seq, D_MODEL)
    xs = x_sample.reshape(n_seq * n_tok, D_MODEL)
    row = lambda v: v.reshape(1, -1)
    fnorm = row(final_norm)
    l = 0
    wg1, wu1, wd1 = (w[l].astype(BF16) for w in (ffn1_w_gate, ffn1_w_up, ffn1_w_down))
    sinks = attn_sinks[l]
    vnorm = gmlp_v_norm[l]
    ws = gmlp_w_spatial[l]
    bs = gmlp_b_spatial[l]
    mnorm, anorm, gnorm = row(mix_norm[l]), row(attn_out_norm[l]), row(gmlp_out_norm[l])

    later = (ffn2_w_gate[l], ffn2_w_up[l], ffn2_w_down[l], w_in[l], w_out[l])
    xp, xs, wg2, wu2, wd2, win, wout = _ffn_half(xp, xs, row(ffn1_norm[l]), wg1, wu1, wd1, fnorm, final_norm=False,
                                                 to_cast=later)
    xp, pk, pv = _mixer_prompt(xp, mnorm, win, sinks, vnorm, ws, bs.T, anorm, gnorm, wout, batch=batch)
    to_cols = lambda c: jnp.transpose(c, (0, 2, 3, 1)).reshape(n_seq, KV_WIDTH, WINDOW)
    to_rows = lambda c: jnp.transpose(c.reshape(n_seq, N_KV_HEADS, HEAD_DIM, WINDOW), (0, 3, 1, 2))[None]
    xs, gn, sk, sv = _mixer_sample(xs, mnorm, win, sinks, vnorm, ws[:, :n_tok, :n_tok].reshape(-1),
                                   bs[:, :n_tok].reshape(-1), anorm, gnorm, wout, to_cols(cache_k[l]),
                                   to_cols(cache_v[l]), n_seq=n_seq, n_tok=n_tok)
    yp, ys = _ffn_half(xp, xs, row(ffn2_norm[l]), wg2, wu2, wd2, fnorm, final_norm=True)

    kv5 = lambda a, n: a.reshape(1, n, WINDOW, N_KV_HEADS, HEAD_DIM)
    y_prompt = yp.reshape(batch, seq, D_MODEL)
    y_sample = ys.reshape(n_seq, n_tok, D_MODEL)
    chunk_v = gn.reshape(1, n_seq, n_tok, GMLP_WIDTH)
    return (y_prompt, y_sample, kv5(pk, batch), kv5(pv, batch), to_rows(sk), to_rows(sv), chunk_v)
```

```python
import functools

import jax
import jax.numpy as jnp
from jax import lax
from jax.experimental import pallas as pl
from jax.experimental.pallas import tpu as pltpu

F32 = jnp.float32
BF16 = jnp.bfloat16

D_MODEL = 1024
D_FF = 2816
N_HEADS = 8
N_KV_HEADS = 2
GQA_GROUP = 4
HEAD_DIM = 64
ATTN_WIDTH = 512
KV_WIDTH = 128
GMLP_WIDTH = 512
N_GATE_GROUPS = 4
GROUP_CH = 128
WINDOW = 128
CHUNK = 128
RMS_EPS = 1e-6
FFN_RESIDUAL = 0.5
ATTN_SCALE = 0.125
LOG2E = 1.4426950408889634
ALIBI_SLOPES = tuple(2.0 ** (-(i + 1)) for i in range(N_HEADS))
MASK_VALUE = -1e30

LANES = 128
BF16_SUBLANES = 16
FFN_ROWS = 512
FFN_CHUNK = 256
MIX_ROWS = 512
SEQ_PER_STEP = 32
VMEM_LIMIT = 56 * 1024 * 1024


def _rms(x, g):
    ms = jnp.mean(x * x, axis=-1, keepdims=True)
    return x * lax.rsqrt(ms + RMS_EPS) * g


def _dot(a, b):
    return jnp.dot(a, b, preferred_element_type=F32)


def _dot_nt(a, b):
    return lax.dot_general(a, b, (((1,), (1,)), ((), ())), preferred_element_type=F32)


def _resident(shape):
    nd = len(shape)
    return pl.BlockSpec(shape, lambda *_: (0,) * nd, pipeline_mode=pl.Buffered(1))


def _smem():
    return pl.BlockSpec(memory_space=pltpu.SMEM)


def _ffn_body(*refs, n_prompt_tiles, final_norm, n_cast):
    xp_ref, xs_ref, xnext_ref, norm_ref, wg_ref, wu_ref, wd_ref, fnorm_ref = refs[:8]
    cast_in = refs[8:8 + n_cast]
    op_ref, os_ref = refs[8 + n_cast:10 + n_cast]
    cast_out = refs[10 + n_cast:10 + 2 * n_cast]
    act_ref, h0_s, h1_s = refs[10 + 2 * n_cast:]
    i = pl.program_id(0)

    @pl.when(i == 0)
    def _first_input():
        h0_s[...] = _rms(xs_ref[...], norm_ref[...]).astype(BF16)

    def _tile(h_cur, h_next):
        for src, dst in zip(cast_in, cast_out):
            dst[...] = src[...].astype(BF16)
        h_next[...] = _rms(xnext_ref[...], norm_ref[...]).astype(BF16)
        h = h_cur[...]
        for c in range(D_FF // FFN_CHUNK):
            cols = slice(c * FFN_CHUNK, (c + 1) * FFN_CHUNK)
            a = _dot(h, wg_ref[:, cols])
            b = _dot(h, wu_ref[:, cols])
            act_ref[:, cols] = (a * jax.nn.sigmoid(a) * b).astype(BF16)
        x = jnp.where(i == 0, xs_ref[...], xp_ref[...])
        y = x + FFN_RESIDUAL * _dot(act_ref[...], wd_ref[...])
        if final_norm:
            y = _rms(y, fnorm_ref[...])
        op_ref[...] = y

    @pl.when(i % 2 == 0)
    def _even():
        _tile(h0_s, h1_s)

    @pl.when(i % 2 == 1)
    def _odd():
        _tile(h1_s, h0_s)

    @pl.when(i == 0)
    def _sample_out():
        os_ref[...] = op_ref[...]


def _cast_block_rows(rows, n_steps):
    br = BF16_SUBLANES
    while rows % br or rows // br > n_steps:
        br += BF16_SUBLANES
    return br


def _ffn_half(xp, xs, norm, wg, wu, wd, fnorm, *, final_norm, to_cast=()):
    n_prompt_tiles = xp.shape[0] // FFN_ROWS
    assert xp.shape[0] == n_prompt_tiles * FFN_ROWS and xs.shape[0] == FFN_ROWS
    row_block = (FFN_ROWS, D_MODEL)
    prompt_spec = pl.BlockSpec(row_block, lambda i: (jnp.maximum(i - 1, 0), 0))
    sample_spec = pl.BlockSpec(row_block, lambda i: (0, 0))
    next_spec = pl.BlockSpec(row_block, lambda i: (jnp.minimum(i, n_prompt_tiles - 1), 0))
    cast_specs, cast_shapes = [], []
    for w in to_cast:
        br = _cast_block_rows(w.shape[0], n_prompt_tiles)
        last = w.shape[0] // br - 1
        cast_specs.append(pl.BlockSpec((br, w.shape[1]), functools.partial(
            lambda i, last: (jnp.minimum(i, last), 0), last=last)))
        cast_shapes.append(jax.ShapeDtypeStruct(w.shape, BF16))
    return pl.pallas_call(
        functools.partial(_ffn_body, n_prompt_tiles=n_prompt_tiles, final_norm=final_norm, n_cast=len(to_cast)),
        grid=(n_prompt_tiles + 1,),
        in_specs=[prompt_spec, sample_spec, next_spec, _resident((1, D_MODEL)), _resident((D_MODEL, D_FF)),
                  _resident((D_MODEL, D_FF)), _resident((D_FF, D_MODEL)), _resident((1, D_MODEL))] + cast_specs,
        out_specs=[prompt_spec, sample_spec] + cast_specs,
        out_shape=[jax.ShapeDtypeStruct(xp.shape, F32), jax.ShapeDtypeStruct(xs.shape, F32)] + cast_shapes,
        scratch_shapes=[pltpu.VMEM((FFN_ROWS, D_FF), BF16),
                        pltpu.VMEM((FFN_ROWS, D_MODEL), BF16),
                        pltpu.VMEM((FFN_ROWS, D_MODEL), BF16)],
        compiler_params=pltpu.CompilerParams(dimension_semantics=("arbitrary",), vmem_limit_bytes=VMEM_LIMIT),
        name="ffn_half_final" if final_norm else "ffn_half",
    )(xp, xs, xp, norm, wg, wu, wd, fnorm, *to_cast)


def _pair_heads(t):
    swapped = pltpu.roll(t, HEAD_DIM, 1)
    left = lax.broadcasted_iota(jnp.int32, t.shape, 1) < HEAD_DIM
    return jnp.where(left, t, swapped), jnp.where(left, swapped, t)


def _gate_norm(g, vnorm_ref, gi):
    gg = g[:, gi * GROUP_CH:(gi + 1) * GROUP_CH]
    return _rms(gg, vnorm_ref[gi:gi + 1, :])


def _merge(x, attn, gate, anorm_ref, gnorm_ref, wout_ref):
    cat = jnp.concatenate([_rms(attn, anorm_ref[...]), _rms(gate, gnorm_ref[...])], axis=1)
    return x + _dot(cat.astype(BF16), wout_ref[...])


def _mixer_prompt_body(x_ref, xnext_ref, mnorm_ref, win_ref, sinks_ref, vnorm_ref, ws_ref, bst_ref, anorm_ref,
                       gnorm_ref, wout_ref, o_ref, pk_ref, pv_ref, h0_s, h1_s, kv_s, q_s, attn_s, gate_s, bias_s,
                       wtril_s, *, tiles):
    step = pl.program_id(0)
    t = step % tiles
    nblk = MIX_ROWS // WINDOW
    band = 2 * WINDOW
    qrows = GQA_GROUP * WINDOW

    @pl.when(step == 0)
    def _init_tables():
        h0_s[...] = _rms(x_ref[...], mnorm_ref[...]).astype(BF16)
        row = lax.broadcasted_iota(jnp.int32, (qrows, band), 0)
        key = lax.broadcasted_iota(jnp.int32, (qrows, band), 1)
        grp = row >> 7
        dist = WINDOW + (row & (WINDOW - 1)) - key
        valid = (dist >= 0) & (dist < WINDOW)
        distf = dist.astype(F32)
        for h in range(N_KV_HEADS):
            s = [ALIBI_SLOPES[h * GQA_GROUP + g] for g in range(GQA_GROUP)]
            slope = jnp.where(grp == 0, s[0], jnp.where(grp == 1, s[1], jnp.where(grp == 2, s[2], s[3])))
            bias = jnp.where(valid, -(slope * distf) * LOG2E, MASK_VALUE)
            bias_s[0, h] = bias
            bias_s[1, h] = jnp.where(key < WINDOW, MASK_VALUE, bias)
        r = lax.broadcasted_iota(jnp.int32, (CHUNK, CHUNK), 0)
        c = lax.broadcasted_iota(jnp.int32, (CHUNK, CHUNK), 1)
        for gi in range(N_GATE_GROUPS):
            wtril_s[gi] = jnp.where(r >= c, ws_ref[gi], 0.0).astype(BF16)

    @pl.when(t == 0)
    def _no_previous_block():
        kv_s[0:WINDOW, :] = jnp.zeros((WINDOW, 4 * KV_WIDTH), BF16)

    def _tile(h_cur, h_next):
        h_in = h_cur[...]
        q_s[...] = _dot(h_in, win_ref[:, 0:ATTN_WIDTH]) * (ATTN_SCALE * LOG2E)
        h_next[...] = _rms(xnext_ref[...], mnorm_ref[...]).astype(BF16)
        wkv = win_ref[:, ATTN_WIDTH:ATTN_WIDTH + 2 * KV_WIDTH]
        pkv = jnp.concatenate([_dot(h_in[0:MIX_ROWS // 2, :], wkv), _dot(h_in[MIX_ROWS // 2:, :], wkv)], axis=0)
        k_new = pkv[:, 0:KV_WIDTH]
        v_new = pkv[:, KV_WIDTH:2 * KV_WIDTH]
        pk_ref[0] = k_new[MIX_ROWS - WINDOW:, :]
        pv_ref[0] = v_new[MIX_ROWS - WINDOW:, :]
        k0, k1 = _pair_heads(k_new)
        v0, v1 = _pair_heads(v_new)
        kv_s[WINDOW:, :] = jnp.concatenate([k0, k1, v0, v1], axis=1).astype(BF16)

        table = jnp.where(t == 0, 1, 0)
        lane_grp = lax.broadcasted_iota(jnp.int32, (WINDOW, 2 * KV_WIDTH), 1) >> 6
        for j in range(nblk):
            rows = slice(j * WINDOW, (j + 1) * WINDOW)
            kvb = kv_s[j * WINDOW:j * WINDOW + band, :]
            for h in range(N_KV_HEADS):
                ka = kvb[:, h * KV_WIDTH:(h + 1) * KV_WIDTH]
                va = kvb[:, (2 + h) * KV_WIDTH:(3 + h) * KV_WIDTH]
                k4 = jnp.concatenate([ka, ka], axis=1)
                v4 = jnp.concatenate([va, va], axis=1)
                q = q_s[rows, h * 2 * KV_WIDTH:(h + 1) * 2 * KV_WIDTH]
                qstack = jnp.concatenate([jnp.where(lane_grp == g, q, 0.0) for g in range(GQA_GROUP)],
                                         axis=0).astype(BF16)
                bias = bias_s[table, h] if j == 0 else bias_s[0, h]
                s = _dot_nt(qstack, k4) + bias
                probs, inv = [], []
                for g in range(GQA_GROUP):
                    sg = s[g * WINDOW:(g + 1) * WINDOW, :]
                    sink = sinks_ref[h * GQA_GROUP + g] * LOG2E
                    m = jnp.maximum(jnp.max(sg, axis=1, keepdims=True), sink)
                    p = jnp.exp2(sg - m)
                    inv.append(1.0 / (jnp.sum(p, axis=1, keepdims=True) + jnp.exp2(sink - m)))
                    probs.append(p.astype(BF16))
                o = _dot(jnp.concatenate(probs, axis=0), v4)
                for g in range(GQA_GROUP):
                    lanes = slice(g * HEAD_DIM, (g + 1) * HEAD_DIM)
                    out_lanes = slice((h * GQA_GROUP + g) * HEAD_DIM, (h * GQA_GROUP + g + 1) * HEAD_DIM)
                    attn_s[rows, out_lanes] = o[g * WINDOW:(g + 1) * WINDOW, lanes] * inv[g]
        kv_s[0:WINDOW, :] = kv_s[MIX_ROWS:MIX_ROWS + WINDOW, :]

        u = jax.nn.gelu(_dot(h_in, win_ref[:, 768:1280]))
        gact = jax.nn.gelu(_dot(h_in, win_ref[:, 1280:1792]))
        for gi in range(N_GATE_GROUPS):
            lanes = slice(gi * GROUP_CH, (gi + 1) * GROUP_CH)
            gn = _gate_norm(gact, vnorm_ref, gi).astype(BF16)
            chunks = jnp.concatenate([gn[c * CHUNK:(c + 1) * CHUNK, :] for c in range(nblk)], axis=1)
            mixed = _dot(wtril_s[gi], chunks) + bst_ref[:, gi:gi + 1]
            for c in range(nblk):
                rows = slice(c * CHUNK, (c + 1) * CHUNK)
                gate_s[rows, lanes] = u[rows, lanes] * mixed[:, c * GROUP_CH:(c + 1) * GROUP_CH]

        o_ref[...] = _merge(x_ref[...], attn_s[...], gate_s[...], anorm_ref, gnorm_ref, wout_ref)

    @pl.when(step % 2 == 0)
    def _even():
        _tile(h0_s, h1_s)

    @pl.when(step % 2 == 1)
    def _odd():
        _tile(h1_s, h0_s)


def _mixer_prompt(x, mnorm, win, sinks, vnorm, ws, bst, anorm, gnorm, wout, *, batch):
    rows = x.shape[0]
    tiles = rows // (batch * MIX_ROWS)
    assert rows == batch * tiles * MIX_ROWS
    band = 2 * WINDOW
    n_steps = batch * tiles
    row_spec = pl.BlockSpec((MIX_ROWS, D_MODEL), lambda s: (s, 0))
    next_spec = pl.BlockSpec((MIX_ROWS, D_MODEL), lambda s: (jnp.minimum(s + 1, n_steps - 1), 0))
    kv_spec = pl.BlockSpec((1, WINDOW, KV_WIDTH), lambda s: (s // tiles, 0, 0))
    kv_shape = jax.ShapeDtypeStruct((batch, WINDOW, KV_WIDTH), F32)
    return pl.pallas_call(
        functools.partial(_mixer_prompt_body, tiles=tiles),
        grid=(n_steps,),
        in_specs=[row_spec, next_spec, _resident((1, D_MODEL)), _resident(win.shape), _smem(),
                  _resident(vnorm.shape), _resident(ws.shape), _resident(bst.shape), _resident(anorm.shape),
                  _resident(gnorm.shape), _resident(wout.shape)],
        out_specs=[row_spec, kv_spec, kv_spec],
        out_shape=[jax.ShapeDtypeStruct(x.shape, F32), kv_shape, kv_shape],
        scratch_shapes=[
            pltpu.VMEM((MIX_ROWS, D_MODEL), BF16),
            pltpu.VMEM((MIX_ROWS, D_MODEL), BF16),
            pltpu.VMEM((WINDOW + MIX_ROWS, 4 * KV_WIDTH), BF16),
            pltpu.VMEM((MIX_ROWS, ATTN_WIDTH), F32),
            pltpu.VMEM((MIX_ROWS, ATTN_WIDTH), F32),
            pltpu.VMEM((MIX_ROWS, GMLP_WIDTH), F32),
            pltpu.VMEM((2, N_KV_HEADS, GQA_GROUP * WINDOW, band), F32),
            pltpu.VMEM((N_GATE_GROUPS, CHUNK, CHUNK), BF16),
        ],
        compiler_params=pltpu.CompilerParams(dimension_semantics=("arbitrary",), vmem_limit_bytes=VMEM_LIMIT),
        name="mixer_prompt",
    )(x, x, mnorm, win, sinks, vnorm, ws, bst, anorm, gnorm, wout)


def _mixer_sample_body(x_ref, mnorm_ref, win_ref, sinks_ref, vnorm_ref, wsm_ref, bsm_ref, anorm_ref, gnorm_ref,
                       wout_ref, ck_ref, cv_ref, o_ref, gn_ref, sk_ref, sv_ref,
                       perm_s, qall_s, oall_s, kvt_s, gate_s, bias_s, sink_s, s_s, p_s, *, n_seq, n_tok):
    step = pl.program_id(0)
    n_steps = pl.num_programs(0)
    n_rows = n_seq * n_tok
    q_rows = N_HEADS * n_tok
    keys = 2 * WINDOW
    step_rows = SEQ_PER_STEP * q_rows

    def by_token(val):
        for c in range(val.shape[1] // LANES):
            perm_s[c] = val[:, c * LANES:(c + 1) * LANES]
        return jnp.concatenate(
            [jnp.concatenate([perm_s[c, pl.ds(i, n_seq, stride=n_tok), :] for i in range(n_tok)], axis=0)
             for c in range(val.shape[1] // LANES)], axis=1)

    def by_sequence(val):
        for c in range(val.shape[1] // LANES):
            for i in range(n_tok):
                perm_s[c, pl.ds(i, n_seq, stride=n_tok), :] = val[i * n_seq:(i + 1) * n_seq, c * LANES:(c + 1) * LANES]
        return jnp.concatenate([perm_s[c] for c in range(val.shape[1] // LANES)], axis=1)

    @pl.when(step == 0)
    def _project():
        h = _rms(x_ref[...], mnorm_ref[...])
        pkv = _dot(h.astype(BF16), win_ref[:, ATTN_WIDTH:ATTN_WIDTH + 2 * KV_WIDTH])
        kvt_s[...] = pkv.T
        h_in = by_token(h).astype(BF16)
        pq = _dot(h_in, win_ref[:, 0:ATTN_WIDTH]) * (ATTN_SCALE * LOG2E)
        half = lax.broadcasted_iota(jnp.int32, (n_rows, LANES), 1) >> 6
        for hg in range(N_HEADS):
            kvh = hg // GQA_GROUP
            slab = pq[:, (hg // 2) * LANES:(hg // 2 + 1) * LANES]
            if (hg % 2) != kvh:
                slab = pltpu.roll(slab, HEAD_DIM, 1)
            qall_s[hg * n_rows:(hg + 1) * n_rows, :] = jnp.where(half == kvh, slab, 0.0)

        u = jax.nn.gelu(_dot(h_in, win_ref[:, 768:1280]))
        gact = jax.nn.gelu(_dot(h_in, win_ref[:, 1280:1792]))
        gns = []
        for gi in range(N_GATE_GROUPS):
            lanes = slice(gi * GROUP_CH, (gi + 1) * GROUP_CH)
            gn = _gate_norm(gact, vnorm_ref, gi)
            gns.append(gn)
            for i in range(n_tok):
                rows = slice(i * n_seq, (i + 1) * n_seq)
                mixed = jnp.full((n_seq, GROUP_CH), bsm_ref[gi * n_tok + i], F32)
                for j in range(i + 1):
                    mixed = mixed + wsm_ref[(gi * n_tok + i) * n_tok + j] * gn[j * n_seq:(j + 1) * n_seq, :]
                gate_s[rows, lanes] = u[rows, lanes] * mixed
        gn_ref[...] = by_sequence(jnp.concatenate(gns, axis=1))

        row = lax.broadcasted_iota(jnp.int32, (step_rows, keys), 0)
        col = lax.broadcasted_iota(jnp.int32, (step_rows, keys), 1)
        tok = row & (n_tok - 1)
        head = (row >> 2) & (N_HEADS - 1)
        slope = jnp.zeros((step_rows, keys), F32)
        for hg in range(N_HEADS):
            slope = jnp.where(head == hg, ALIBI_SLOPES[hg], slope)
        shifted = col < WINDOW
        pos = jnp.where(shifted, col + n_tok, col - WINDOW)
        dist = WINDOW + tok - pos
        valid = (dist >= 0) & (dist < WINDOW) & (shifted | (pos < n_tok))
        bias_s[...] = jnp.where(valid, -(slope * dist.astype(F32)) * LOG2E, MASK_VALUE)
        head1 = (lax.broadcasted_iota(jnp.int32, (step_rows, 1), 0) >> 2) & (N_HEADS - 1)
        sink = jnp.zeros((step_rows, 1), F32)
        for hg in range(N_HEADS):
            sink = jnp.where(head1 == hg, sinks_ref[hg] * LOG2E, sink)
        sink_s[...] = sink

    col0 = pl.multiple_of(step * LANES, LANES)
    k_cols = kvt_s[0:KV_WIDTH, pl.ds(col0, LANES)]
    v_cols = kvt_s[KV_WIDTH:2 * KV_WIDTH, pl.ds(col0, LANES)]
    is_new = lax.broadcasted_iota(jnp.int32, (KV_WIDTH, WINDOW), 1) >= WINDOW - n_tok
    for bl in range(SEQ_PER_STEP):
        to_tail = (WINDOW - n_tok - n_tok * bl) % LANES
        sk_ref[bl] = jnp.where(is_new, pltpu.roll(k_cols, to_tail, 1), pltpu.roll(ck_ref[bl], WINDOW - n_tok, 1))
        sv_ref[bl] = jnp.where(is_new, pltpu.roll(v_cols, to_tail, 1), pltpu.roll(cv_ref[bl], WINDOW - n_tok, 1))

    seq0 = step * SEQ_PER_STEP
    for bl in range(SEQ_PER_STEP):
        qb = qall_s[pl.ds(seq0 + bl, q_rows, stride=n_seq), :].astype(BF16)
        k2 = jnp.concatenate([sk_ref[bl], ck_ref[bl]], axis=1).astype(BF16)
        s_s[bl * q_rows:(bl + 1) * q_rows, :] = _dot(qb, k2)
    s = s_s[...] + bias_s[...]
    sink = sink_s[...]
    m = jnp.maximum(jnp.max(s, axis=1, keepdims=True), sink)
    p = jnp.exp2(s - m)
    inv = 1.0 / (jnp.sum(p, axis=1, keepdims=True) + jnp.exp2(sink - m))
    p_s[...] = p.astype(BF16)
    for bl in range(SEQ_PER_STEP):
        rows = slice(bl * q_rows, (bl + 1) * q_rows)
        v2 = jnp.concatenate([sv_ref[bl], cv_ref[bl]], axis=1).astype(BF16)
        oall_s[pl.ds(seq0 + bl, q_rows, stride=n_seq), :] = _dot_nt(p_s[rows, :], v2) * inv[rows, :]

    @pl.when(step == n_steps - 1)
    def _merge_rows():
        left = lax.broadcasted_iota(jnp.int32, (n_rows, LANES), 1) < HEAD_DIM
        slabs = []
        for pair in range(N_HEADS // 2):
            halves = []
            for hg in (2 * pair, 2 * pair + 1):
                o = oall_s[hg * n_rows:(hg + 1) * n_rows, :]
                if (hg % 2) != (hg // GQA_GROUP):
                    o = pltpu.roll(o, HEAD_DIM, 1)
                halves.append(o)
            slabs.append(jnp.where(left, halves[0], halves[1]))
        attn = jnp.concatenate(slabs, axis=1)
        o_ref[...] = by_sequence(_merge(by_token(x_ref[...]), attn, gate_s[...], anorm_ref, gnorm_ref, wout_ref))


def _mixer_sample(x, mnorm, win, sinks, vnorm, wsm, bsm, anorm, gnorm, wout, ck, cv, *, n_seq, n_tok):
    n_rows = n_seq * n_tok
    q_rows = N_HEADS * n_tok
    assert x.shape[0] == n_rows and n_seq % SEQ_PER_STEP == 0 and n_tok == 4 and n_seq == LANES
    assert SEQ_PER_STEP * n_tok == LANES
    cache_spec = pl.BlockSpec((SEQ_PER_STEP, KV_WIDTH, WINDOW), lambda s: (s, 0, 0))
    cache_shape = jax.ShapeDtypeStruct(ck.shape, F32)
    rows_spec = pl.BlockSpec((n_rows, D_MODEL), lambda s: (0, 0))
    gn_spec = pl.BlockSpec((n_rows, GMLP_WIDTH), lambda s: (0, 0))
    return pl.pallas_call(
        functools.partial(_mixer_sample_body, n_seq=n_seq, n_tok=n_tok),
        grid=(n_seq // SEQ_PER_STEP,),
        in_specs=[rows_spec, _resident((1, D_MODEL)), _resident(win.shape), _smem(), _resident(vnorm.shape),
                  _smem(), _smem(), _resident(anorm.shape), _resident(gnorm.shape), _resident(wout.shape),
                  cache_spec, cache_spec],
        out_specs=[rows_spec, gn_spec, cache_spec, cache_spec],
        out_shape=[jax.ShapeDtypeStruct(x.shape, F32), jax.ShapeDtypeStruct((n_rows, GMLP_WIDTH), F32),
                   cache_shape, cache_shape],
        scratch_shapes=[
            pltpu.VMEM((D_MODEL // LANES, n_rows, LANES), F32),
            pltpu.VMEM((N_HEADS * n_rows, LANES), F32),
            pltpu.VMEM((N_HEADS * n_rows, LANES), F32),
            pltpu.VMEM((2 * KV_WIDTH, n_rows), F32),
            pltpu.VMEM((n_rows, GMLP_WIDTH), F32),
            pltpu.VMEM((SEQ_PER_STEP * q_rows, 2 * WINDOW), F32),
            pltpu.VMEM((SEQ_PER_STEP * q_rows, 1), F32),
            pltpu.VMEM((SEQ_PER_STEP * q_rows, 2 * WINDOW), F32),
            pltpu.VMEM((SEQ_PER_STEP * q_rows, 2 * WINDOW), BF16),
        ],
        compiler_params=pltpu.CompilerParams(dimension_semantics=("arbitrary",), vmem_limit_bytes=VMEM_LIMIT),
        name="mixer_sample",
    )(x, mnorm, win, sinks, vnorm, wsm, bsm, anorm, gnorm, wout, ck, cv)


def kernel(x_prompt, x_sample, cache_k, cache_v, ffn1_norm, ffn1_w_gate, ffn1_w_up, ffn1_w_down, mix_norm, w_in,
           attn_sinks, gmlp_v_norm, gmlp_w_spatial, gmlp_b_spatial, attn_out_norm, gmlp_out_norm, w_out, ffn2_norm,
           ffn2_w_gate, ffn2_w_up, ffn2_w_down, final_norm):
    batch, seq, _ = x_prompt.shape
    n_seq, n_tok, _ = x_sample.shape
    depth = cache_k.shape[0]
    assert depth == 1

    xp = x_prompt.reshape(batch * seq, D_MODEL)
    xs = x_sample.reshape(n_seq * n_tok, D_MODEL)
    row = lambda v: v.reshape(1, -1)
    fnorm = row(final_norm)
    l = 0
    wg1, wu1, wd1 = (w[l].astype(BF16) for w in (ffn1_w_gate, ffn1_w_up, ffn1_w_down))
    sinks = attn_sinks[l]
    vnorm = gmlp_v_norm[l]
    ws = gmlp_w_spatial[l]
    bs = gmlp_b_spatial[l]
    mnorm, anorm, gnorm = row(mix_norm[l]), row(attn_out_norm[l]), row(gmlp_out_norm[l])

    later = (ffn2_w_gate[l], ffn2_w_up[l], ffn2_w_down[l], w_in[l], w_out[l])
    xp, xs, wg2, wu2, wd2, win, wout = _ffn_half(xp, xs, row(ffn1_norm[l]), wg1, wu1, wd1, fnorm, final_norm=False,
                                                 to_cast=later)
    xp, pk, pv = _mixer_prompt(xp, mnorm, win, sinks, vnorm, ws, bs.T, anorm, gnorm, wout, batch=batch)
    to_cols = lambda c: jnp.transpose(c, (0, 2, 3, 1)).reshape(n_seq, KV_WIDTH, WINDOW)
    to_rows = lambda c: jnp.transpose(c.reshape(n_seq, N_KV_HEADS, HEAD_DIM, WINDOW), (0, 3, 1, 2))[None]
    xs, gn, sk, sv = _mixer_sample(xs, mnorm, win, sinks, vnorm, ws[:, :n_tok, :n_tok].reshape(-1),
                                   bs[:, :n_tok].reshape(-1), anorm, gnorm, wout, to_cols(cache_k[l]),
                                   to_cols(cache_v[l]), n_seq=n_seq, n_tok=n_tok)
    yp, ys = _ffn_half(xp, xs, row(ffn2_norm[l]), wg2, wu2, wd2, fnorm, final_norm=True)

    kv5 = lambda a, n: a.reshape(1, n, WINDOW, N_KV_HEADS, HEAD_DIM)
    y_prompt = yp.reshape(batch, seq, D_MODEL)
    y_sample = ys.reshape(n_seq, n_tok, D_MODEL)
    chunk_v = gn.reshape(1, n_seq, n_tok, GMLP_WIDTH)
    return (y_prompt, y_sample, kv5(pk, batch), kv5(pv, batch), to_rows(sk), to_rows(sv), chunk_v)
```

```python
import functools

import jax
import jax.numpy as jnp
from jax import lax
from jax.experimental import pallas as pl
from jax.experimental.pallas import tpu as pltpu

F32 = jnp.float32
BF16 = jnp.bfloat16

D_MODEL = 1024
D_FF = 2816
N_HEADS = 8
N_KV_HEADS = 2
GQA_GROUP = 4
HEAD_DIM = 64
ATTN_WIDTH = 512
KV_WIDTH = 128
GMLP_WIDTH = 512
N_GATE_GROUPS = 4
GROUP_CH = 128
WINDOW = 128
CHUNK = 128
RMS_EPS = 1e-6
FFN_RESIDUAL = 0.5
ATTN_SCALE = 0.125
LOG2E = 1.4426950408889634
ALIBI_SLOPES = tuple(2.0 ** (-(i + 1)) for i in range(N_HEADS))
MASK_VALUE = -1e30

LANES = 128
BF16_SUBLANES = 16
FFN_ROWS = 512
FFN_CHUNK = 256
MIX_ROWS = 512
SEQ_PER_STEP = 32
VMEM_LIMIT = 56 * 1024 * 1024


def _rms(x, g):
    ms = jnp.mean(x * x, axis=-1, keepdims=True)
    return x * lax.rsqrt(ms + RMS_EPS) * g


def _dot(a, b):
    return jnp.dot(a, b, preferred_element_type=F32)


def _dot_nt(a, b):
    return lax.dot_general(a, b, (((1,), (1,)), ((), ())), preferred_element_type=F32)


def _resident(shape):
    nd = len(shape)
    return pl.BlockSpec(shape, lambda *_: (0,) * nd, pipeline_mode=pl.Buffered(1))


def _smem():
    return pl.BlockSpec(memory_space=pltpu.SMEM)


def _ffn_body(*refs, n_prompt_tiles, final_norm, n_cast):
    xs_ref, xnext_ref, norm_ref, wg_ref, wu_ref, wd_ref, fnorm_ref = refs[:7]
    cast_in = refs[7:7 + n_cast]
    op_ref, os_ref = refs[7 + n_cast:9 + n_cast]
    cast_out = refs[9 + n_cast:9 + 2 * n_cast]
    act_ref, h0_s, h1_s, x0_s, x1_s = refs[9 + 2 * n_cast:]
    i = pl.program_id(0)

    @pl.when(i == 0)
    def _first_input():
        x0_s[...] = xs_ref[...]
        h0_s[...] = _rms(xs_ref[...], norm_ref[...]).astype(BF16)

    def _tile(h_cur, h_next, x_cur, x_next):
        for src, dst in zip(cast_in, cast_out):
            dst[...] = src[...].astype(BF16)
        x_next[...] = xnext_ref[...]
        h_next[...] = _rms(xnext_ref[...], norm_ref[...]).astype(BF16)
        h = h_cur[...]
        for c in range(D_FF // FFN_CHUNK):
            cols = slice(c * FFN_CHUNK, (c + 1) * FFN_CHUNK)
            a = _dot(h, wg_ref[:, cols])
            b = _dot(h, wu_ref[:, cols])
            act_ref[:, cols] = (a * jax.nn.sigmoid(a) * b).astype(BF16)
        y = x_cur[...] + FFN_RESIDUAL * _dot(act_ref[...], wd_ref[...])
        if final_norm:
            y = _rms(y, fnorm_ref[...])
        op_ref[...] = y

    @pl.when(i % 2 == 0)
    def _even():
        _tile(h0_s, h1_s, x0_s, x1_s)

    @pl.when(i % 2 == 1)
    def _odd():
        _tile(h1_s, h0_s, x1_s, x0_s)

    @pl.when(i == 0)
    def _sample_out():
        os_ref[...] = op_ref[...]


def _cast_block_rows(rows, n_steps):
    br = BF16_SUBLANES
    while rows % br or rows // br > n_steps:
        br += BF16_SUBLANES
    return br


def _ffn_half(xp, xs, norm, wg, wu, wd, fnorm, *, final_norm, to_cast=()):
    n_prompt_tiles = xp.shape[0] // FFN_ROWS
    assert xp.shape[0] == n_prompt_tiles * FFN_ROWS and xs.shape[0] == FFN_ROWS
    row_block = (FFN_ROWS, D_MODEL)
    prompt_spec = pl.BlockSpec(row_block, lambda i: (jnp.maximum(i - 1, 0), 0))
    sample_spec = pl.BlockSpec(row_block, lambda i: (0, 0))
    next_spec = pl.BlockSpec(row_block, lambda i: (jnp.minimum(i, n_prompt_tiles - 1), 0))
    cast_specs, cast_shapes = [], []
    for w in to_cast:
        br = _cast_block_rows(w.shape[0], n_prompt_tiles)
        last = w.shape[0] // br - 1
        cast_specs.append(pl.BlockSpec((br, w.shape[1]), functools.partial(
            lambda i, last: (jnp.minimum(i, last), 0), last=last)))
        cast_shapes.append(jax.ShapeDtypeStruct(w.shape, BF16))
    return pl.pallas_call(
        functools.partial(_ffn_body, n_prompt_tiles=n_prompt_tiles, final_norm=final_norm, n_cast=len(to_cast)),
        grid=(n_prompt_tiles + 1,),
        in_specs=[sample_spec, next_spec, _resident((1, D_MODEL)), _resident((D_MODEL, D_FF)),
                  _resident((D_MODEL, D_FF)), _resident((D_FF, D_MODEL)), _resident((1, D_MODEL))] + cast_specs,
        out_specs=[prompt_spec, sample_spec] + cast_specs,
        out_shape=[jax.ShapeDtypeStruct(xp.shape, F32), jax.ShapeDtypeStruct(xs.shape, F32)] + cast_shapes,
        scratch_shapes=[pltpu.VMEM((FFN_ROWS, D_FF), BF16),
                        pltpu.VMEM((FFN_ROWS, D_MODEL), BF16),
                        pltpu.VMEM((FFN_ROWS, D_MODEL), BF16),
                        pltpu.VMEM((FFN_ROWS, D_MODEL), F32),
                        pltpu.VMEM((FFN_ROWS, D_MODEL), F32)],
        compiler_params=pltpu.CompilerParams(dimension_semantics=("arbitrary",), vmem_limit_bytes=VMEM_LIMIT),
        name="ffn_half_final" if final_norm else "ffn_half",
    )(xs, xp, norm, wg, wu, wd, fnorm, *to_cast)


def _pair_heads(t):
    swapped = pltpu.roll(t, HEAD_DIM, 1)
    left = lax.broadcasted_iota(jnp.int32, t.shape, 1) < HEAD_DIM
    return jnp.where(left, t, swapped), jnp.where(left, swapped, t)


def _gate_norm(g, vnorm_ref, gi):
    gg = g[:, gi * GROUP_CH:(gi + 1) * GROUP_CH]
    return _rms(gg, vnorm_ref[gi:gi + 1, :])


def _merge(x, attn, gate, anorm_ref, gnorm_ref, wout_ref):
    cat = jnp.concatenate([_rms(attn, anorm_ref[...]), _rms(gate, gnorm_ref[...])], axis=1)
    return x + _dot(cat.astype(BF16), wout_ref[...])


def _mixer_prompt_body(xfirst_ref, xnext_ref, mnorm_ref, win_ref, sinks_ref, vnorm_ref, ws_ref, bst_ref, anorm_ref,
                       gnorm_ref, wout_ref, o_ref, pk_ref, pv_ref, h0_s, h1_s, x0_s, x1_s, kv_s, q_s, attn_s, gate_s,
                       bias_s, wtril_s, *, tiles):
    step = pl.program_id(0)
    t = step % tiles
    nblk = MIX_ROWS // WINDOW
    band = 2 * WINDOW
    qrows = GQA_GROUP * WINDOW

    @pl.when(step == 0)
    def _init_tables():
        x0_s[...] = xfirst_ref[...]
        h0_s[...] = _rms(xfirst_ref[...], mnorm_ref[...]).astype(BF16)
        row = lax.broadcasted_iota(jnp.int32, (qrows, band), 0)
        key = lax.broadcasted_iota(jnp.int32, (qrows, band), 1)
        grp = row >> 7
        dist = WINDOW + (row & (WINDOW - 1)) - key
        valid = (dist >= 0) & (dist < WINDOW)
        distf = dist.astype(F32)
        for h in range(N_KV_HEADS):
            s = [ALIBI_SLOPES[h * GQA_GROUP + g] for g in range(GQA_GROUP)]
            slope = jnp.where(grp == 0, s[0], jnp.where(grp == 1, s[1], jnp.where(grp == 2, s[2], s[3])))
            bias = jnp.where(valid, -(slope * distf) * LOG2E, MASK_VALUE)
            bias_s[0, h] = bias
            bias_s[1, h] = jnp.where(key < WINDOW, MASK_VALUE, bias)
        r = lax.broadcasted_iota(jnp.int32, (CHUNK, CHUNK), 0)
        c = lax.broadcasted_iota(jnp.int32, (CHUNK, CHUNK), 1)
        for gi in range(N_GATE_GROUPS):
            wtril_s[gi] = jnp.where(r >= c, ws_ref[gi], 0.0).astype(BF16)

    @pl.when(t == 0)
    def _no_previous_block():
        kv_s[0:WINDOW, :] = jnp.zeros((WINDOW, 4 * KV_WIDTH), BF16)

    def _tile(h_cur, h_next, x_cur, x_next):
        h_in = h_cur[...]
        q_s[...] = _dot(h_in, win_ref[:, 0:ATTN_WIDTH]) * (ATTN_SCALE * LOG2E)
        x_next[...] = xnext_ref[...]
        h_next[...] = _rms(xnext_ref[...], mnorm_ref[...]).astype(BF16)
        wkv = win_ref[:, ATTN_WIDTH:ATTN_WIDTH + 2 * KV_WIDTH]
        pkv = jnp.concatenate([_dot(h_in[0:MIX_ROWS // 2, :], wkv), _dot(h_in[MIX_ROWS // 2:, :], wkv)], axis=0)
        k_new = pkv[:, 0:KV_WIDTH]
        v_new = pkv[:, KV_WIDTH:2 * KV_WIDTH]
        pk_ref[0] = k_new[MIX_ROWS - WINDOW:, :]
        pv_ref[0] = v_new[MIX_ROWS - WINDOW:, :]
        k0, k1 = _pair_heads(k_new)
        v0, v1 = _pair_heads(v_new)
        kv_s[WINDOW:, :] = jnp.concatenate([k0, k1, v0, v1], axis=1).astype(BF16)

        table = jnp.where(t == 0, 1, 0)
        lane_grp = lax.broadcasted_iota(jnp.int32, (WINDOW, 2 * KV_WIDTH), 1) >> 6
        for j in range(nblk):
            rows = slice(j * WINDOW, (j + 1) * WINDOW)
            kvb = kv_s[j * WINDOW:j * WINDOW + band, :]
            for h in range(N_KV_HEADS):
                ka = kvb[:, h * KV_WIDTH:(h + 1) * KV_WIDTH]
                va = kvb[:, (2 + h) * KV_WIDTH:(3 + h) * KV_WIDTH]
                k4 = jnp.concatenate([ka, ka], axis=1)
                v4 = jnp.concatenate([va, va], axis=1)
                q = q_s[rows, h * 2 * KV_WIDTH:(h + 1) * 2 * KV_WIDTH]
                qstack = jnp.concatenate([jnp.where(lane_grp == g, q, 0.0) for g in range(GQA_GROUP)],
                                         axis=0).astype(BF16)
                bias = bias_s[table, h] if j == 0 else bias_s[0, h]
                s = _dot_nt(qstack, k4) + bias
                probs, inv = [], []
                for g in range(GQA_GROUP):
                    sg = s[g * WINDOW:(g + 1) * WINDOW, :]
                    sink = sinks_ref[h * GQA_GROUP + g] * LOG2E
                    m = jnp.maximum(jnp.max(sg, axis=1, keepdims=True), sink)
                    p = jnp.exp2(sg - m)
                    inv.append(1.0 / (jnp.sum(p, axis=1, keepdims=True) + jnp.exp2(sink - m)))
                    probs.append(p.astype(BF16))
                o = _dot(jnp.concatenate(probs, axis=0), v4)
                for g in range(GQA_GROUP):
                    lanes = slice(g * HEAD_DIM, (g + 1) * HEAD_DIM)
                    out_lanes = slice((h * GQA_GROUP + g) * HEAD_DIM, (h * GQA_GROUP + g + 1) * HEAD_DIM)
                    attn_s[rows, out_lanes] = o[g * WINDOW:(g + 1) * WINDOW, lanes] * inv[g]
        kv_s[0:WINDOW, :] = kv_s[MIX_ROWS:MIX_ROWS + WINDOW, :]

        u = jax.nn.gelu(_dot(h_in, win_ref[:, 768:1280]))
        gact = jax.nn.gelu(_dot(h_in, win_ref[:, 1280:1792]))
        for gi in range(N_GATE_GROUPS):
            lanes = slice(gi * GROUP_CH, (gi + 1) * GROUP_CH)
            gn = _gate_norm(gact, vnorm_ref, gi).astype(BF16)
            chunks = jnp.concatenate([gn[c * CHUNK:(c + 1) * CHUNK, :] for c in range(nblk)], axis=1)
            mixed = _dot(wtril_s[gi], chunks) + bst_ref[:, gi:gi + 1]
            for c in range(nblk):
                rows = slice(c * CHUNK, (c + 1) * CHUNK)
                gate_s[rows, lanes] = u[rows, lanes] * mixed[:, c * GROUP_CH:(c + 1) * GROUP_CH]

        o_ref[...] = _merge(x_cur[...], attn_s[...], gate_s[...], anorm_ref, gnorm_ref, wout_ref)

    @pl.when(step % 2 == 0)
    def _even():
        _tile(h0_s, h1_s, x0_s, x1_s)

    @pl.when(step % 2 == 1)
    def _odd():
        _tile(h1_s, h0_s, x1_s, x0_s)


def _mixer_prompt(x, mnorm, win, sinks, vnorm, ws, bst, anorm, gnorm, wout, *, batch):
    rows = x.shape[0]
    tiles = rows // (batch * MIX_ROWS)
    assert rows == batch * tiles * MIX_ROWS
    band = 2 * WINDOW
    n_steps = batch * tiles
    row_spec = pl.BlockSpec((MIX_ROWS, D_MODEL), lambda s: (s, 0))
    first_spec = pl.BlockSpec((MIX_ROWS, D_MODEL), lambda s: (0, 0))
    next_spec = pl.BlockSpec((MIX_ROWS, D_MODEL), lambda s: (jnp.minimum(s + 1, n_steps - 1), 0))
    kv_spec = pl.BlockSpec((1, WINDOW, KV_WIDTH), lambda s: (s // tiles, 0, 0))
    kv_shape = jax.ShapeDtypeStruct((batch, WINDOW, KV_WIDTH), F32)
    return pl.pallas_call(
        functools.partial(_mixer_prompt_body, tiles=tiles),
        grid=(n_steps,),
        in_specs=[first_spec, next_spec, _resident((1, D_MODEL)), _resident(win.shape), _smem(),
                  _resident(vnorm.shape), _resident(ws.shape), _resident(bst.shape), _resident(anorm.shape),
                  _resident(gnorm.shape), _resident(wout.shape)],
        out_specs=[row_spec, kv_spec, kv_spec],
        out_shape=[jax.ShapeDtypeStruct(x.shape, F32), kv_shape, kv_shape],
        scratch_shapes=[
            pltpu.VMEM((MIX_ROWS, D_MODEL), BF16),
            pltpu.VMEM((MIX_ROWS, D_MODEL), BF16),
            pltpu.VMEM((MIX_ROWS, D_MODEL), F32),
            pltpu.VMEM((MIX_ROWS, D_MODEL), F32),
            pltpu.VMEM((WINDOW + MIX_ROWS, 4 * KV_WIDTH), BF16),
            pltpu.VMEM((MIX_ROWS, ATTN_WIDTH), F32),
            pltpu.VMEM((MIX_ROWS, ATTN_WIDTH), F32),
            pltpu.VMEM((MIX_ROWS, GMLP_WIDTH), F32),
            pltpu.VMEM((2, N_KV_HEADS, GQA_GROUP * WINDOW, band), F32),
            pltpu.VMEM((N_GATE_GROUPS, CHUNK, CHUNK), BF16),
        ],
        compiler_params=pltpu.CompilerParams(dimension_semantics=("arbitrary",), vmem_limit_bytes=VMEM_LIMIT),
        name="mixer_prompt",
    )(x, x, mnorm, win, sinks, vnorm, ws, bst, anorm, gnorm, wout)


def _mixer_sample_body(x_ref, mnorm_ref, win_ref, sinks_ref, vnorm_ref, wsm_ref, bsm_ref, anorm_ref, gnorm_ref,
                       wout_ref, ck_ref, cv_ref, o_ref, gn_ref, sk_ref, sv_ref,
                       perm_s, qall_s, oall_s, kvt_s, gate_s, bias_s, sink_s, s_s, p_s, *, n_seq, n_tok):
    step = pl.program_id(0)
    n_steps = pl.num_programs(0)
    n_rows = n_seq * n_tok
    q_rows = N_HEADS * n_tok
    keys = 2 * WINDOW
    step_rows = SEQ_PER_STEP * q_rows

    def by_token(val):
        for c in range(val.shape[1] // LANES):
            perm_s[c] = val[:, c * LANES:(c + 1) * LANES]
        return jnp.concatenate(
            [jnp.concatenate([perm_s[c, pl.ds(i, n_seq, stride=n_tok), :] for i in range(n_tok)], axis=0)
             for c in range(val.shape[1] // LANES)], axis=1)

    def by_sequence(val):
        for c in range(val.shape[1] // LANES):
            for i in range(n_tok):
                perm_s[c, pl.ds(i, n_seq, stride=n_tok), :] = val[i * n_seq:(i + 1) * n_seq, c * LANES:(c + 1) * LANES]
        return jnp.concatenate([perm_s[c] for c in range(val.shape[1] // LANES)], axis=1)

    @pl.when(step == 0)
    def _project():
        h = _rms(x_ref[...], mnorm_ref[...])
        pkv = _dot(h.astype(BF16), win_ref[:, ATTN_WIDTH:ATTN_WIDTH + 2 * KV_WIDTH])
        kvt_s[...] = pkv.T
        h_in = by_token(h).astype(BF16)
        pq = _dot(h_in, win_ref[:, 0:ATTN_WIDTH]) * (ATTN_SCALE * LOG2E)
        half = lax.broadcasted_iota(jnp.int32, (n_rows, LANES), 1) >> 6
        for hg in range(N_HEADS):
            kvh = hg // GQA_GROUP
            slab = pq[:, (hg // 2) * LANES:(hg // 2 + 1) * LANES]
            if (hg % 2) != kvh:
                slab = pltpu.roll(slab, HEAD_DIM, 1)
            qall_s[hg * n_rows:(hg + 1) * n_rows, :] = jnp.where(half == kvh, slab, 0.0)

        u = jax.nn.gelu(_dot(h_in, win_ref[:, 768:1280]))
        gact = jax.nn.gelu(_dot(h_in, win_ref[:, 1280:1792]))
        gns = []
        for gi in range(N_GATE_GROUPS):
            lanes = slice(gi * GROUP_CH, (gi + 1) * GROUP_CH)
            gn = _gate_norm(gact, vnorm_ref, gi)
            gns.append(gn)
            for i in range(n_tok):
                rows = slice(i * n_seq, (i + 1) * n_seq)
                mixed = jnp.full((n_seq, GROUP_CH), bsm_ref[gi * n_tok + i], F32)
                for j in range(i + 1):
                    mixed = mixed + wsm_ref[(gi * n_tok + i) * n_tok + j] * gn[j * n_seq:(j + 1) * n_seq, :]
                gate_s[rows, lanes] = u[rows, lanes] * mixed
        gn_ref[...] = by_sequence(jnp.concatenate(gns, axis=1))

        row = lax.broadcasted_iota(jnp.int32, (step_rows, keys), 0)
        col = lax.broadcasted_iota(jnp.int32, (step_rows, keys), 1)
        tok = row & (n_tok - 1)
        head = (row >> 2) & (N_HEADS - 1)
        slope = jnp.zeros((step_rows, keys), F32)
        for hg in range(N_HEADS):
            slope = jnp.where(head == hg, ALIBI_SLOPES[hg], slope)
        shifted = col < WINDOW
        pos = jnp.where(shifted, col + n_tok, col - WINDOW)
        dist = WINDOW + tok - pos
        valid = (dist >= 0) & (dist < WINDOW) & (shifted | (pos < n_tok))
        bias_s[...] = jnp.where(valid, -(slope * dist.astype(F32)) * LOG2E, MASK_VALUE)
        head1 = (lax.broadcasted_iota(jnp.int32, (step_rows, 1), 0) >> 2) & (N_HEADS - 1)
        sink = jnp.zeros((step_rows, 1), F32)
        for hg in range(N_HEADS):
            sink = jnp.where(head1 == hg, sinks_ref[hg] * LOG2E, sink)
        sink_s[...] = sink

    col0 = pl.multiple_of(step * LANES, LANES)
    k_cols = kvt_s[0:KV_WIDTH, pl.ds(col0, LANES)]
    v_cols = kvt_s[KV_WIDTH:2 * KV_WIDTH, pl.ds(col0, LANES)]
    is_new = lax.broadcasted_iota(jnp.int32, (KV_WIDTH, WINDOW), 1) >= WINDOW - n_tok
    for bl in range(SEQ_PER_STEP):
        to_tail = (WINDOW - n_tok - n_tok * bl) % LANES
        sk_ref[bl] = jnp.where(is_new, pltpu.roll(k_cols, to_tail, 1), pltpu.roll(ck_ref[bl], WINDOW - n_tok, 1))
        sv_ref[bl] = jnp.where(is_new, pltpu.roll(v_cols, to_tail, 1), pltpu.roll(cv_ref[bl], WINDOW - n_tok, 1))

    seq0 = step * SEQ_PER_STEP
    for bl in range(SEQ_PER_STEP):
        qb = qall_s[pl.ds(seq0 + bl, q_rows, stride=n_seq), :].astype(BF16)
        k2 = jnp.concatenate([sk_ref[bl], ck_ref[bl]], axis=1).astype(BF16)
        s_s[bl * q_rows:(bl + 1) * q_rows, :] = _dot(qb, k2)
    s = s_s[...] + bias_s[...]
    sink = sink_s[...]
    m = jnp.maximum(jnp.max(s, axis=1, keepdims=True), sink)
    p = jnp.exp2(s - m)
    inv = 1.0 / (jnp.sum(p, axis=1, keepdims=True) + jnp.exp2(sink - m))
    p_s[...] = p.astype(BF16)
    for bl in range(SEQ_PER_STEP):
        rows = slice(bl * q_rows, (bl + 1) * q_rows)
        v2 = jnp.concatenate([sv_ref[bl], cv_ref[bl]], axis=1).astype(BF16)
        oall_s[pl.ds(seq0 + bl, q_rows, stride=n_seq), :] = _dot_nt(p_s[rows, :], v2) * inv[rows, :]

    @pl.when(step == n_steps - 1)
    def _merge_rows():
        left = lax.broadcasted_iota(jnp.int32, (n_rows, LANES), 1) < HEAD_DIM
        slabs = []
        for pair in range(N_HEADS // 2):
            halves = []
            for hg in (2 * pair, 2 * pair + 1):
                o = oall_s[hg * n_rows:(hg + 1) * n_rows, :]
                if (hg % 2) != (hg // GQA_GROUP):
                    o = pltpu.roll(o, HEAD_DIM, 1)
                halves.append(o)
            slabs.append(jnp.where(left, halves[0], halves[1]))
        attn = jnp.concatenate(slabs, axis=1)
        o_ref[...] = by_sequence(_merge(by_token(x_ref[...]), attn, gate_s[...], anorm_ref, gnorm_ref, wout_ref))


def _mixer_sample(x, mnorm, win, sinks, vnorm, wsm, bsm, anorm, gnorm, wout, ck, cv, *, n_seq, n_tok):
    n_rows = n_seq * n_tok
    q_rows = N_HEADS * n_tok
    assert x.shape[0] == n_rows and n_seq % SEQ_PER_STEP == 0 and n_tok == 4 and n_seq == LANES
    assert SEQ_PER_STEP * n_tok == LANES
    cache_spec = pl.BlockSpec((SEQ_PER_STEP, KV_WIDTH, WINDOW), lambda s: (s, 0, 0))
    cache_shape = jax.ShapeDtypeStruct(ck.shape, F32)
    rows_spec = pl.BlockSpec((n_rows, D_MODEL), lambda s: (0, 0))
    gn_spec = pl.BlockSpec((n_rows, GMLP_WIDTH), lambda s: (0, 0))
    return pl.pallas_call(
        functools.partial(_mixer_sample_body, n_seq=n_seq, n_tok=n_tok),
        grid=(n_seq // SEQ_PER_STEP,),
        in_specs=[rows_spec, _resident((1, D_MODEL)), _resident(win.shape), _smem(), _resident(vnorm.shape),
                  _smem(), _smem(), _resident(anorm.shape), _resident(gnorm.shape), _resident(wout.shape),
                  cache_spec, cache_spec],
        out_specs=[rows_spec, gn_spec, cache_spec, cache_spec],
        out_shape=[jax.ShapeDtypeStruct(x.shape, F32), jax.ShapeDtypeStruct((n_rows, GMLP_WIDTH), F32),
                   cache_shape, cache_shape],
        scratch_shapes=[
            pltpu.VMEM((D_MODEL // LANES, n_rows, LANES), F32),
            pltpu.VMEM((N_HEADS * n_rows, LANES), F32),
            pltpu.VMEM((N_HEADS * n_rows, LANES), F32),
            pltpu.VMEM((2 * KV_WIDTH, n_rows), F32),
            pltpu.VMEM((n_rows, GMLP_WIDTH), F32),
            pltpu.VMEM((SEQ_PER_STEP * q_rows, 2 * WINDOW), F32),
            pltpu.VMEM((SEQ_PER_STEP * q_rows, 1), F32),
            pltpu.VMEM((SEQ_PER_STEP * q_rows, 2 * WINDOW), F32),
            pltpu.VMEM((SEQ_PER_STEP * q_rows, 2 * WINDOW), BF16),
        ],
        compiler_params=pltpu.CompilerParams(dimension_semantics=("arbitrary",), vmem_limit_bytes=VMEM_LIMIT),
        name="mixer_sample",
    )(x, mnorm, win, sinks, vnorm, wsm, bsm, anorm, gnorm, wout, ck, cv)


def kernel(x_prompt, x_sample, cache_k, cache_v, ffn1_norm, ffn1_w_gate, ffn1_w_up, ffn1_w_down, mix_norm, w_in,
           attn_sinks, gmlp_v_norm, gmlp_w_spatial, gmlp_b_spatial, attn_out_norm, gmlp_out_norm, w_out, ffn2_norm,
           ffn2_w_gate, ffn2_w_up, ffn2_w_down, final_norm):
    batch, seq, _ = x_prompt.shape
    n_seq, n_tok, _ = x_sample.shape
    depth = cache_k.shape[0]
    assert depth == 1

    xp = x_prompt.reshape(batch * seq, D_MODEL)
    xs = x_sample.reshape(n_seq * n_tok, D_MODEL)
    row = lambda v: v.reshape(1, -1)
    fnorm = row(final_norm)
    l = 0
    wg1, wu1, wd1 = (w[l].astype(BF16) for w in (ffn1_w_gate, ffn1_w_up, ffn1_w_down))
    sinks = attn_sinks[l]
    vnorm = gmlp_v_norm[l]
    ws = gmlp_w_spatial[l]
    bs = gmlp_b_spatial[l]
    mnorm, anorm, gnorm = row(mix_norm[l]), row(attn_out_norm[l]), row(gmlp_out_norm[l])

    later = (ffn2_w_gate[l], ffn2_w_up[l], ffn2_w_down[l], w_in[l], w_out[l])
    xp, xs, wg2, wu2, wd2, win, wout = _ffn_half(xp, xs, row(ffn1_norm[l]), wg1, wu1, wd1, fnorm, final_norm=False,
                                                 to_cast=later)
    xp, pk, pv = _mixer_prompt(xp, mnorm, win, sinks, vnorm, ws, bs.T, anorm, gnorm, wout, batch=batch)
    to_cols = lambda c: jnp.transpose(c, (0, 2, 3, 1)).reshape(n_seq, KV_WIDTH, WINDOW)
    to_rows = lambda c: jnp.transpose(c.reshape(n_seq, N_KV_HEADS, HEAD_DIM, WINDOW), (0, 3, 1, 2))[None]
    xs, gn, sk, sv = _mixer_sample(xs, mnorm, win, sinks, vnorm, ws[:, :n_tok, :n_tok].reshape(-1),
                                   bs[:, :n_tok].reshape(-1), anorm, gnorm, wout, to_cols(cache_k[l]),
                                   to_cols(cache_v[l]), n_seq=n_seq, n_tok=n_tok)
    yp, ys = _ffn_half(xp, xs, row(ffn2_norm[l]), wg2, wu2, wd2, fnorm, final_norm=True)

    kv5 = lambda a, n: a.reshape(1, n, WINDOW, N_KV_HEADS, HEAD_DIM)
    y_prompt = yp.reshape(batch, seq, D_MODEL)
    y_sample = ys.reshape(n_seq, n_tok, D_MODEL)
    chunk_v = gn.reshape(1, n_seq, n_tok, GMLP_WIDTH)
    return (y_prompt, y_sample, kv5(pk, batch), kv5(pv, batch), to_rows(sk), to_rows(sv), chunk_v)
```

```python
import functools

import jax
import jax.numpy as jnp
from jax import lax
from jax.experimental import pallas as pl
from jax.experimental.pallas import tpu as pltpu

F32 = jnp.float32
BF16 = jnp.bfloat16

D_MODEL = 1024
D_FF = 2816
N_HEADS = 8
N_KV_HEADS = 2
GQA_GROUP = 4
HEAD_DIM = 64
ATTN_WIDTH = 512
KV_WIDTH = 128
GMLP_WIDTH = 512
N_GATE_GROUPS = 4
GROUP_CH = 128
WINDOW = 128
CHUNK = 128
RMS_EPS = 1e-6
FFN_RESIDUAL = 0.5
ATTN_SCALE = 0.125
LOG2E = 1.4426950408889634
ALIBI_SLOPES = tuple(2.0 ** (-(i + 1)) for i in range(N_HEADS))
MASK_VALUE = -1e30

LANES = 128
BF16_SUBLANES = 16
FFN_ROWS = 512
FFN_SUBTILES = 2
FFN_CHUNK = 256
MIX_ROWS = 512
SEQ_PER_STEP = 32
VMEM_LIMIT = 56 * 1024 * 1024


def _rms(x, g):
    ms = jnp.mean(x * x, axis=-1, keepdims=True)
    return x * lax.rsqrt(ms + RMS_EPS) * g


def _dot(a, b):
    return jnp.dot(a, b, preferred_element_type=F32)


def _dot_nt(a, b):
    return lax.dot_general(a, b, (((1,), (1,)), ((), ())), preferred_element_type=F32)


def _resident(shape):
    nd = len(shape)
    return pl.BlockSpec(shape, lambda *_: (0,) * nd, pipeline_mode=pl.Buffered(1))


def _smem():
    return pl.BlockSpec(memory_space=pltpu.SMEM)


def _ffn_body(*refs, final_norm, n_cast):
    xp_ref, xs_ref, norm_ref, wg_ref, wu_ref, wd_ref, fnorm_ref = refs[:7]
    cast_in = refs[7:7 + n_cast]
    op_ref, os_ref = refs[7 + n_cast:9 + n_cast]
    cast_out = refs[9 + n_cast:9 + 2 * n_cast]
    act_ref = refs[9 + 2 * n_cast]
    i = pl.program_id(0)
    for src, dst in zip(cast_in, cast_out):
        dst[...] = src[...].astype(BF16)

    def _sub_tile(k, carry):
        rows = pl.ds(pl.multiple_of(k * FFN_ROWS, FFN_ROWS), FFN_ROWS)
        x = jnp.where(i == 0, xs_ref[...], xp_ref[rows, :])
        h = _rms(x, norm_ref[...]).astype(BF16)
        for c in range(D_FF // FFN_CHUNK):
            cols = slice(c * FFN_CHUNK, (c + 1) * FFN_CHUNK)
            a = _dot(h, wg_ref[:, cols])
            b = _dot(h, wu_ref[:, cols])
            act_ref[:, cols] = (a * jax.nn.sigmoid(a) * b).astype(BF16)
        y = x + FFN_RESIDUAL * _dot(act_ref[...], wd_ref[...])
        if final_norm:
            y = _rms(y, fnorm_ref[...])
        op_ref[rows, :] = y
        return carry

    lax.fori_loop(0, jnp.where(i == 0, 1, FFN_SUBTILES), _sub_tile, 0)

    @pl.when(i == 0)
    def _():
        os_ref[...] = op_ref[0:FFN_ROWS, :]


def _cast_block_rows(rows, n_steps):
    br = BF16_SUBLANES
    while rows % br or rows // br > n_steps:
        br += BF16_SUBLANES
    return br


def _ffn_half(xp, xs, norm, wg, wu, wd, fnorm, *, final_norm, to_cast=()):
    block_rows = FFN_SUBTILES * FFN_ROWS
    n_prompt_tiles = xp.shape[0] // block_rows
    assert xp.shape[0] == n_prompt_tiles * block_rows and xs.shape[0] == FFN_ROWS
    prompt_spec = pl.BlockSpec((block_rows, D_MODEL), lambda i: (jnp.maximum(i - 1, 0), 0))
    sample_spec = pl.BlockSpec((FFN_ROWS, D_MODEL), lambda i: (0, 0))
    cast_specs, cast_shapes = [], []
    for w in to_cast:
        br = _cast_block_rows(w.shape[0], n_prompt_tiles)
        last = w.shape[0] // br - 1
        cast_specs.append(pl.BlockSpec((br, w.shape[1]), functools.partial(
            lambda i, last: (jnp.minimum(i, last), 0), last=last)))
        cast_shapes.append(jax.ShapeDtypeStruct(w.shape, BF16))
    return pl.pallas_call(
        functools.partial(_ffn_body, final_norm=final_norm, n_cast=len(to_cast)),
        grid=(n_prompt_tiles + 1,),
        in_specs=[prompt_spec, sample_spec, _resident((1, D_MODEL)), _resident((D_MODEL, D_FF)),
                  _resident((D_MODEL, D_FF)), _resident((D_FF, D_MODEL)), _resident((1, D_MODEL))] + cast_specs,
        out_specs=[prompt_spec, sample_spec] + cast_specs,
        out_shape=[jax.ShapeDtypeStruct(xp.shape, F32), jax.ShapeDtypeStruct(xs.shape, F32)] + cast_shapes,
        scratch_shapes=[pltpu.VMEM((FFN_ROWS, D_FF), BF16)],
        compiler_params=pltpu.CompilerParams(dimension_semantics=("arbitrary",), vmem_limit_bytes=VMEM_LIMIT),
        name="ffn_half_final" if final_norm else "ffn_half",
    )(xp, xs, norm, wg, wu, wd, fnorm, *to_cast)


def _pair_heads(t):
    swapped = pltpu.roll(t, HEAD_DIM, 1)
    left = lax.broadcasted_iota(jnp.int32, t.shape, 1) < HEAD_DIM
    return jnp.where(left, t, swapped), jnp.where(left, swapped, t)


def _gate_norm(g, vnorm_ref, gi):
    gg = g[:, gi * GROUP_CH:(gi + 1) * GROUP_CH]
    return _rms(gg, vnorm_ref[gi:gi + 1, :])


def _merge(x, attn, gate, anorm_ref, gnorm_ref, wout_ref):
    cat = jnp.concatenate([_rms(attn, anorm_ref[...]), _rms(gate, gnorm_ref[...])], axis=1)
    return x + _dot(cat.astype(BF16), wout_ref[...])


def _mixer_prompt_body(x_ref, mnorm_ref, win_ref, sinks_ref, vnorm_ref, ws_ref, bst_ref, anorm_ref, gnorm_ref,
                       wout_ref, o_ref, pk_ref, pv_ref, kv_s, q_s, attn_s, gate_s, bias_s, wtril_s):
    b = pl.program_id(0)
    t = pl.program_id(1)
    nblk = MIX_ROWS // WINDOW
    band = 2 * WINDOW
    qrows = GQA_GROUP * WINDOW

    @pl.when((b == 0) & (t == 0))
    def _init_tables():
        row = lax.broadcasted_iota(jnp.int32, (qrows, band), 0)
        key = lax.broadcasted_iota(jnp.int32, (qrows, band), 1)
        grp = row >> 7
        dist = WINDOW + (row & (WINDOW - 1)) - key
        valid = (dist >= 0) & (dist < WINDOW)
        distf = dist.astype(F32)
        for h in range(N_KV_HEADS):
            s = [ALIBI_SLOPES[h * GQA_GROUP + g] for g in range(GQA_GROUP)]
            slope = jnp.where(grp == 0, s[0], jnp.where(grp == 1, s[1], jnp.where(grp == 2, s[2], s[3])))
            bias = jnp.where(valid, -(slope * distf) * LOG2E, MASK_VALUE)
            bias_s[0, h] = bias
            bias_s[1, h] = jnp.where(key < WINDOW, MASK_VALUE, bias)
        r = lax.broadcasted_iota(jnp.int32, (CHUNK, CHUNK), 0)
        c = lax.broadcasted_iota(jnp.int32, (CHUNK, CHUNK), 1)
        for gi in range(N_GATE_GROUPS):
            wtril_s[gi] = jnp.where(r >= c, ws_ref[gi], 0.0).astype(BF16)

    @pl.when(t == 0)
    def _no_previous_block():
        kv_s[0:WINDOW, :] = jnp.zeros((WINDOW, 4 * KV_WIDTH), BF16)

    x = x_ref[...]
    h_in = _rms(x, mnorm_ref[...]).astype(BF16)
    q_s[...] = _dot(h_in, win_ref[:, 0:ATTN_WIDTH]) * (ATTN_SCALE * LOG2E)
    wkv = win_ref[:, ATTN_WIDTH:ATTN_WIDTH + 2 * KV_WIDTH]
    pkv = jnp.concatenate([_dot(h_in[0:MIX_ROWS // 2, :], wkv), _dot(h_in[MIX_ROWS // 2:, :], wkv)], axis=0)
    k_new = pkv[:, 0:KV_WIDTH]
    v_new = pkv[:, KV_WIDTH:2 * KV_WIDTH]
    pk_ref[0] = k_new[MIX_ROWS - WINDOW:, :]
    pv_ref[0] = v_new[MIX_ROWS - WINDOW:, :]
    k0, k1 = _pair_heads(k_new)
    v0, v1 = _pair_heads(v_new)
    kv_s[WINDOW:, :] = jnp.concatenate([k0, k1, v0, v1], axis=1).astype(BF16)

    table = jnp.where(t == 0, 1, 0)
    lane_grp = lax.broadcasted_iota(jnp.int32, (WINDOW, 2 * KV_WIDTH), 1) >> 6
    for j in range(nblk):
        rows = slice(j * WINDOW, (j + 1) * WINDOW)
        kvb = kv_s[j * WINDOW:j * WINDOW + band, :]
        for h in range(N_KV_HEADS):
            ka = kvb[:, h * KV_WIDTH:(h + 1) * KV_WIDTH]
            va = kvb[:, (2 + h) * KV_WIDTH:(3 + h) * KV_WIDTH]
            k4 = jnp.concatenate([ka, ka], axis=1)
            v4 = jnp.concatenate([va, va], axis=1)
            q = q_s[rows, h * 2 * KV_WIDTH:(h + 1) * 2 * KV_WIDTH]
            qstack = jnp.concatenate([jnp.where(lane_grp == g, q, 0.0) for g in range(GQA_GROUP)],
                                     axis=0).astype(BF16)
            bias = bias_s[table, h] if j == 0 else bias_s[0, h]
            s = _dot_nt(qstack, k4) + bias
            probs, inv = [], []
            for g in range(GQA_GROUP):
                sg = s[g * WINDOW:(g + 1) * WINDOW, :]
                sink = sinks_ref[h * GQA_GROUP + g] * LOG2E
                m = jnp.maximum(jnp.max(sg, axis=1, keepdims=True), sink)
                p = jnp.exp2(sg - m)
                inv.append(1.0 / (jnp.sum(p, axis=1, keepdims=True) + jnp.exp2(sink - m)))
                probs.append(p.astype(BF16))
            o = _dot(jnp.concatenate(probs, axis=0), v4)
            for g in range(GQA_GROUP):
                lanes = slice(g * HEAD_DIM, (g + 1) * HEAD_DIM)
                out_lanes = slice((h * GQA_GROUP + g) * HEAD_DIM, (h * GQA_GROUP + g + 1) * HEAD_DIM)
                attn_s[rows, out_lanes] = o[g * WINDOW:(g + 1) * WINDOW, lanes] * inv[g]
    kv_s[0:WINDOW, :] = kv_s[MIX_ROWS:MIX_ROWS + WINDOW, :]

    u = jax.nn.gelu(_dot(h_in, win_ref[:, 768:1280]))
    gact = jax.nn.gelu(_dot(h_in, win_ref[:, 1280:1792]))
    for gi in range(N_GATE_GROUPS):
        lanes = slice(gi * GROUP_CH, (gi + 1) * GROUP_CH)
        gn = _gate_norm(gact, vnorm_ref, gi).astype(BF16)
        chunks = jnp.concatenate([gn[c * CHUNK:(c + 1) * CHUNK, :] for c in range(nblk)], axis=1)
        mixed = _dot(wtril_s[gi], chunks) + bst_ref[:, gi:gi + 1]
        for c in range(nblk):
            rows = slice(c * CHUNK, (c + 1) * CHUNK)
            gate_s[rows, lanes] = u[rows, lanes] * mixed[:, c * GROUP_CH:(c + 1) * GROUP_CH]

    o_ref[...] = _merge(x, attn_s[...], gate_s[...], anorm_ref, gnorm_ref, wout_ref)


def _mixer_prompt(x, mnorm, win, sinks, vnorm, ws, bst, anorm, gnorm, wout, *, batch):
    rows = x.shape[0]
    tiles = rows // (batch * MIX_ROWS)
    assert rows == batch * tiles * MIX_ROWS
    band = 2 * WINDOW
    row_spec = pl.BlockSpec((MIX_ROWS, D_MODEL), lambda b, t: (b * tiles + t, 0))
    kv_spec = pl.BlockSpec((1, WINDOW, KV_WIDTH), lambda b, t: (b, 0, 0))
    kv_shape = jax.ShapeDtypeStruct((batch, WINDOW, KV_WIDTH), F32)
    return pl.pallas_call(
        _mixer_prompt_body,
        grid=(batch, tiles),
        in_specs=[row_spec, _resident((1, D_MODEL)), _resident(win.shape), _smem(), _resident(vnorm.shape),
                  _resident(ws.shape), _resident(bst.shape), _resident(anorm.shape), _resident(gnorm.shape),
                  _resident(wout.shape)],
        out_specs=[row_spec, kv_spec, kv_spec],
        out_shape=[jax.ShapeDtypeStruct(x.shape, F32), kv_shape, kv_shape],
        scratch_shapes=[
            pltpu.VMEM((WINDOW + MIX_ROWS, 4 * KV_WIDTH), BF16),
            pltpu.VMEM((MIX_ROWS, ATTN_WIDTH), F32),
            pltpu.VMEM((MIX_ROWS, ATTN_WIDTH), F32),
            pltpu.VMEM((MIX_ROWS, GMLP_WIDTH), F32),
            pltpu.VMEM((2, N_KV_HEADS, GQA_GROUP * WINDOW, band), F32),
            pltpu.VMEM((N_GATE_GROUPS, CHUNK, CHUNK), BF16),
        ],
        compiler_params=pltpu.CompilerParams(dimension_semantics=("arbitrary", "arbitrary"),
                                             vmem_limit_bytes=VMEM_LIMIT),
        name="mixer_prompt",
    )(x, mnorm, win, sinks, vnorm, ws, bst, anorm, gnorm, wout)


def _mixer_sample_body(x_ref, mnorm_ref, win_ref, sinks_ref, vnorm_ref, wsm_ref, bsm_ref, anorm_ref, gnorm_ref,
                       wout_ref, ck_ref, cv_ref, o_ref, gn_ref, sk_ref, sv_ref,
                       perm_s, qall_s, oall_s, kvt_s, gate_s, bias_s, sink_s, s_s, p_s, *, n_seq, n_tok):
    step = pl.program_id(0)
    n_steps = pl.num_programs(0)
    n_rows = n_seq * n_tok
    q_rows = N_HEADS * n_tok
    keys = 2 * WINDOW
    step_rows = SEQ_PER_STEP * q_rows

    def by_token(val):
        for c in range(val.shape[1] // LANES):
            perm_s[c] = val[:, c * LANES:(c + 1) * LANES]
        return jnp.concatenate(
            [jnp.concatenate([perm_s[c, pl.ds(i, n_seq, stride=n_tok), :] for i in range(n_tok)], axis=0)
             for c in range(val.shape[1] // LANES)], axis=1)

    def by_sequence(val):
        for c in range(val.shape[1] // LANES):
            for i in range(n_tok):
                perm_s[c, pl.ds(i, n_seq, stride=n_tok), :] = val[i * n_seq:(i + 1) * n_seq, c * LANES:(c + 1) * LANES]
        return jnp.concatenate([perm_s[c] for c in range(val.shape[1] // LANES)], axis=1)

    @pl.when(step == 0)
    def _project():
        h = _rms(x_ref[...], mnorm_ref[...])
        pkv = _dot(h.astype(BF16), win_ref[:, ATTN_WIDTH:ATTN_WIDTH + 2 * KV_WIDTH])
        kvt_s[...] = pkv.T
        h_in = by_token(h).astype(BF16)
        pq = _dot(h_in, win_ref[:, 0:ATTN_WIDTH]) * (ATTN_SCALE * LOG2E)
        half = lax.broadcasted_iota(jnp.int32, (n_rows, LANES), 1) >> 6
        for hg in range(N_HEADS):
            kvh = hg // GQA_GROUP
            slab = pq[:, (hg // 2) * LANES:(hg // 2 + 1) * LANES]
            if (hg % 2) != kvh:
                slab = pltpu.roll(slab, HEAD_DIM, 1)
            qall_s[hg * n_rows:(hg + 1) * n_rows, :] = jnp.where(half == kvh, slab, 0.0)

        u = jax.nn.gelu(_dot(h_in, win_ref[:, 768:1280]))
        gact = jax.nn.gelu(_dot(h_in, win_ref[:, 1280:1792]))
        gns = []
        for gi in range(N_GATE_GROUPS):
            lanes = slice(gi * GROUP_CH, (gi + 1) * GROUP_CH)
            gn = _gate_norm(gact, vnorm_ref, gi)
            gns.append(gn)
            for i in range(n_tok):
                rows = slice(i * n_seq, (i + 1) * n_seq)
                mixed = jnp.full((n_seq, GROUP_CH), bsm_ref[gi * n_tok + i], F32)
                for j in range(i + 1):
                    mixed = mixed + wsm_ref[(gi * n_tok + i) * n_tok + j] * gn[j * n_seq:(j + 1) * n_seq, :]
                gate_s[rows, lanes] = u[rows, lanes] * mixed
        gn_ref[...] = by_sequence(jnp.concatenate(gns, axis=1))

        row = lax.broadcasted_iota(jnp.int32, (step_rows, keys), 0)
        col = lax.broadcasted_iota(jnp.int32, (step_rows, keys), 1)
        tok = row & (n_tok - 1)
        head = (row >> 2) & (N_HEADS - 1)
        slope = jnp.zeros((step_rows, keys), F32)
        for hg in range(N_HEADS):
            slope = jnp.where(head == hg, ALIBI_SLOPES[hg], slope)
        shifted = col < WINDOW
        pos = jnp.where(shifted, col + n_tok, col - WINDOW)
        dist = WINDOW + tok - pos
        valid = (dist >= 0) & (dist < WINDOW) & (shifted | (pos < n_tok))
        bias_s[...] = jnp.where(valid, -(slope * dist.astype(F32)) * LOG2E, MASK_VALUE)
        head1 = (lax.broadcasted_iota(jnp.int32, (step_rows, 1), 0) >> 2) & (N_HEADS - 1)
        sink = jnp.zeros((step_rows, 1), F32)
        for hg in range(N_HEADS):
            sink = jnp.where(head1 == hg, sinks_ref[hg] * LOG2E, sink)
        sink_s[...] = sink

    col0 = pl.multiple_of(step * LANES, LANES)
    k_cols = kvt_s[0:KV_WIDTH, pl.ds(col0, LANES)]
    v_cols = kvt_s[KV_WIDTH:2 * KV_WIDTH, pl.ds(col0, LANES)]
    is_new = lax.broadcasted_iota(jnp.int32, (KV_WIDTH, WINDOW), 1) >= WINDOW - n_tok
    for bl in range(SEQ_PER_STEP):
        to_tail = (WINDOW - n_tok - n_tok * bl) % LANES
        sk_ref[bl] = jnp.where(is_new, pltpu.roll(k_cols, to_tail, 1), pltpu.roll(ck_ref[bl], WINDOW - n_tok, 1))
        sv_ref[bl] = jnp.where(is_new, pltpu.roll(v_cols, to_tail, 1), pltpu.roll(cv_ref[bl], WINDOW - n_tok, 1))

    seq0 = step * SEQ_PER_STEP
    for bl in range(SEQ_PER_STEP):
        qb = qall_s[pl.ds(seq0 + bl, q_rows, stride=n_seq), :].astype(BF16)
        k2 = jnp.concatenate([sk_ref[bl], ck_ref[bl]], axis=1).astype(BF16)
        s_s[bl * q_rows:(bl + 1) * q_rows, :] = _dot(qb, k2)
    s = s_s[...] + bias_s[...]
    sink = sink_s[...]
    m = jnp.maximum(jnp.max(s, axis=1, keepdims=True), sink)
    p = jnp.exp2(s - m)
    inv = 1.0 / (jnp.sum(p, axis=1, keepdims=True) + jnp.exp2(sink - m))
    p_s[...] = p.astype(BF16)
    for bl in range(SEQ_PER_STEP):
        rows = slice(bl * q_rows, (bl + 1) * q_rows)
        v2 = jnp.concatenate([sv_ref[bl], cv_ref[bl]], axis=1).astype(BF16)
        oall_s[pl.ds(seq0 + bl, q_rows, stride=n_seq), :] = _dot_nt(p_s[rows, :], v2) * inv[rows, :]

    @pl.when(step == n_steps - 1)
    def _merge_rows():
        left = lax.broadcasted_iota(jnp.int32, (n_rows, LANES), 1) < HEAD_DIM
        slabs = []
        for pair in range(N_HEADS // 2):
            halves = []
            for hg in (2 * pair, 2 * pair + 1):
                o = oall_s[hg * n_rows:(hg + 1) * n_rows, :]
                if (hg % 2) != (hg // GQA_GROUP):
                    o = pltpu.roll(o, HEAD_DIM, 1)
                halves.append(o)
            slabs.append(jnp.where(left, halves[0], halves[1]))
        attn = jnp.concatenate(slabs, axis=1)
        o_ref[...] = by_sequence(_merge(by_token(x_ref[...]), attn, gate_s[...], anorm_ref, gnorm_ref, wout_ref))


def _mixer_sample(x, mnorm, win, sinks, vnorm, wsm, bsm, anorm, gnorm, wout, ck, cv, *, n_seq, n_tok):
    n_rows = n_seq * n_tok
    q_rows = N_HEADS * n_tok
    assert x.shape[0] == n_rows and n_seq % SEQ_PER_STEP == 0 and n_tok == 4 and n_seq == LANES
    assert SEQ_PER_STEP * n_tok == LANES
    cache_spec = pl.BlockSpec((SEQ_PER_STEP, KV_WIDTH, WINDOW), lambda s: (s, 0, 0))
    cache_shape = jax.ShapeDtypeStruct(ck.shape, F32)
    rows_spec = pl.BlockSpec((n_rows, D_MODEL), lambda s: (0, 0))
    gn_spec = pl.BlockSpec((n_rows, GMLP_WIDTH), lambda s: (0, 0))
    return pl.pallas_call(
        functools.partial(_mixer_sample_body, n_seq=n_seq, n_tok=n_tok),
        grid=(n_seq // SEQ_PER_STEP,),
        in_specs=[rows_spec, _resident((1, D_MODEL)), _resident(win.shape), _smem(), _resident(vnorm.shape),
                  _smem(), _smem(), _resident(anorm.shape), _resident(gnorm.shape), _resident(wout.shape),
                  cache_spec, cache_spec],
        out_specs=[rows_spec, gn_spec, cache_spec, cache_spec],
        out_shape=[jax.ShapeDtypeStruct(x.shape, F32), jax.ShapeDtypeStruct((n_rows, GMLP_WIDTH), F32),
                   cache_shape, cache_shape],
        scratch_shapes=[
            pltpu.VMEM((D_MODEL // LANES, n_rows, LANES), F32),
            pltpu.VMEM((N_HEADS * n_rows, LANES), F32),
            pltpu.VMEM((N_HEADS * n_rows, LANES), F32),
            pltpu.VMEM((2 * KV_WIDTH, n_rows), F32),
            pltpu.VMEM((n_rows, GMLP_WIDTH), F32),
            pltpu.VMEM((SEQ_PER_STEP * q_rows, 2 * WINDOW), F32),
            pltpu.VMEM((SEQ_PER_STEP * q_rows, 1), F32),
            pltpu.VMEM((SEQ_PER_STEP * q_rows, 2 * WINDOW), F32),
            pltpu.VMEM((SEQ_PER_STEP * q_rows, 2 * WINDOW), BF16),
        ],
        compiler_params=pltpu.CompilerParams(dimension_semantics=("arbitrary",), vmem_limit_bytes=VMEM_LIMIT),
        name="mixer_sample",
    )(x, mnorm, win, sinks, vnorm, wsm, bsm, anorm, gnorm, wout, ck, cv)


def kernel(x_prompt, x_sample, cache_k, cache_v, ffn1_norm, ffn1_w_gate, ffn1_w_up, ffn1_w_down, mix_norm, w_in,
           attn_sinks, gmlp_v_norm, gmlp_w_spatial, gmlp_b_spatial, attn_out_norm, gmlp_out_norm, w_out, ffn2_norm,
           ffn2_w_gate, ffn2_w_up, ffn2_w_down, final_norm):
    batch, seq, _ = x_prompt.shape
    n_seq, n_tok, _ = x_sample.shape
    depth = cache_k.shape[0]
    assert depth == 1

    xp = x_prompt.reshape(batch * seq, D_MODEL)
    xs = x_sample.reshape(n_seq * n_tok, D_MODEL)
    row = lambda v: v.reshape(1, -1)
    fnorm = row(final_norm)
    l = 0
    wg1, wu1, wd1 = (w[l].astype(BF16) for w in (ffn1_w_gate, ffn1_w_up, ffn1_w_down))
    sinks = attn_sinks[l]
    vnorm = gmlp_v_norm[l]
    ws = gmlp_w_spatial[l]
    bs = gmlp_b_spatial[l]
    mnorm, anorm, gnorm = row(mix_norm[l]), row(attn_out_norm[l]), row(gmlp_out_norm[l])

    later = (ffn2_w_gate[l], ffn2_w_up[l], ffn2_w_down[l], w_in[l], w_out[l])
    xp, xs, wg2, wu2, wd2, win, wout = _ffn_half(xp, xs, row(ffn1_norm[l]), wg1, wu1, wd1, fnorm, final_norm=False,
                                                 to_cast=later)
    xp, pk, pv = _mixer_prompt(xp, mnorm, win, sinks, vnorm, ws, bs.T, anorm, gnorm, wout, batch=batch)
    to_cols = lambda c: jnp.transpose(c, (0, 2, 3, 1)).reshape(n_seq, KV_WIDTH, WINDOW)
    to_rows = lambda c: jnp.transpose(c.reshape(n_seq, N_KV_HEADS, HEAD_DIM, WINDOW), (0, 3, 1, 2))[None]
    xs, gn, sk, sv = _mixer_sample(xs, mnorm, win, sinks, vnorm, ws[:, :n_tok, :n_tok].reshape(-1),
                                   bs[:, :n_tok].reshape(-1), anorm, gnorm, wout, to_cols(cache_k[l]),
                                   to_cols(cache_v[l]), n_seq=n_seq, n_tok=n_tok)
    yp, ys = _ffn_half(xp, xs, row(ffn2_norm[l]), wg2, wu2, wd2, fnorm, final_norm=True)

    kv5 = lambda a, n: a.reshape(1, n, WINDOW, N_KV_HEADS, HEAD_DIM)
    y_prompt = yp.reshape(batch, seq, D_MODEL)
    y_sample = ys.reshape(n_seq, n_tok, D_MODEL)
    chunk_v = gn.reshape(1, n_seq, n_tok, GMLP_WIDTH)
    return (y_prompt, y_sample, kv5(pk, batch), kv5(pv, batch), to_rows(sk), to_rows(sv), chunk_v)
```

```python
import functools

import jax
import jax.numpy as jnp
from jax import lax
from jax.experimental import pallas as pl
from jax.experimental.pallas import tpu as pltpu

F32 = jnp.float32
BF16 = jnp.bfloat16

D_MODEL = 1024
D_FF = 2816
N_HEADS = 8
N_KV_HEADS = 2
GQA_GROUP = 4
HEAD_DIM = 64
ATTN_WIDTH = 512
KV_WIDTH = 128
GMLP_WIDTH = 512
N_GATE_GROUPS = 4
GROUP_CH = 128
WINDOW = 128
CHUNK = 128
RMS_EPS = 1e-6
FFN_RESIDUAL = 0.5
ATTN_SCALE = 0.125
LOG2E = 1.4426950408889634
ALIBI_SLOPES = tuple(2.0 ** (-(i + 1)) for i in range(N_HEADS))
MASK_VALUE = -1e30

LANES = 128
BF16_SUBLANES = 16
FFN_ROWS = 512
FFN_LOAD_STEPS = 16
FFN_CHUNK = 256
MIX_ROWS = 512
SEQ_PER_STEP = 32
VMEM_LIMIT = 56 * 1024 * 1024


def _rms(x, g):
    ms = jnp.mean(x * x, axis=-1, keepdims=True)
    return x * lax.rsqrt(ms + RMS_EPS) * g


def _dot(a, b):
    return jnp.dot(a, b, preferred_element_type=F32)


def _dot_nt(a, b):
    return lax.dot_general(a, b, (((1,), (1,)), ((), ())), preferred_element_type=F32)


def _resident(shape):
    nd = len(shape)
    return pl.BlockSpec(shape, lambda *_: (0,) * nd, pipeline_mode=pl.Buffered(1))


def _smem():
    return pl.BlockSpec(memory_space=pltpu.SMEM)


def _ffn_body(*refs, final_norm, n_cast, subtiles, n_load_steps):
    xp_ref, xs_ref, norm_ref, wg_in, wu_in, wd_in, fnorm_ref = refs[:7]
    cast_in = refs[7:7 + n_cast]
    op_ref, os_ref = refs[7 + n_cast:9 + n_cast]
    cast_out = refs[9 + n_cast:9 + 2 * n_cast]
    act_ref = refs[9 + 2 * n_cast]
    i = pl.program_id(0)
    first = n_load_steps

    if n_load_steps:
        wg_ref, wu_ref, wd_ref = refs[10 + 2 * n_cast:]

        @pl.when(i < n_load_steps)
        def _load_weights():
            for src, dst in ((wg_in, wg_ref), (wu_in, wu_ref), (wd_in, wd_ref)):
                br = src.shape[0]
                dst[pl.ds(pl.multiple_of(i * br, br), br), :] = src[...].astype(BF16)
    else:
        wg_ref, wu_ref, wd_ref = wg_in, wu_in, wd_in

    for src, dst in zip(cast_in, cast_out):
        dst[...] = src[...].astype(BF16)

    def _sub_tile(k, carry):
        rows = pl.ds(pl.multiple_of(k * FFN_ROWS, FFN_ROWS), FFN_ROWS)
        x = jnp.where(i == first, xs_ref[...], xp_ref[rows, :])
        h = _rms(x, norm_ref[...]).astype(BF16)
        for c in range(D_FF // FFN_CHUNK):
            cols = slice(c * FFN_CHUNK, (c + 1) * FFN_CHUNK)
            a = _dot(h, wg_ref[:, cols])
            b = _dot(h, wu_ref[:, cols])
            act_ref[:, cols] = (a * jax.nn.sigmoid(a) * b).astype(BF16)
        y = x + FFN_RESIDUAL * _dot(act_ref[...], wd_ref[...])
        if final_norm:
            y = _rms(y, fnorm_ref[...])
        op_ref[rows, :] = y
        return carry

    trips = jnp.where(i < first, 0, jnp.where(i == first, 1, subtiles))
    lax.fori_loop(0, trips, _sub_tile, 0)

    @pl.when(i == first)
    def _():
        os_ref[...] = op_ref[0:FFN_ROWS, :]


def _block_rows(rows, n_steps):
    br = BF16_SUBLANES
    while rows % br or rows // br > n_steps:
        br += BF16_SUBLANES
    return br


def _ffn_half(xp, xs, norm, wg, wu, wd, fnorm, *, final_norm, subtiles, to_cast=()):
    block_rows = subtiles * FFN_ROWS
    n_prompt_tiles = xp.shape[0] // block_rows
    assert xp.shape[0] == n_prompt_tiles * block_rows and xs.shape[0] == FFN_ROWS
    n_load_steps = FFN_LOAD_STEPS if wg.dtype == F32 else 0
    first = n_load_steps
    prompt_spec = pl.BlockSpec((block_rows, D_MODEL), lambda i: (jnp.maximum(i - first - 1, 0), 0))
    sample_spec = pl.BlockSpec((FFN_ROWS, D_MODEL), lambda i: (0, 0))

    def streamed(w, n_steps, start):
        br = _block_rows(w.shape[0], n_steps)
        return pl.BlockSpec((br, w.shape[1]), functools.partial(
            lambda i, last: (jnp.clip(i - start, 0, last), 0), last=w.shape[0] // br - 1))

    cast_specs = [streamed(w, n_prompt_tiles, first) for w in to_cast]
    cast_shapes = [jax.ShapeDtypeStruct(w.shape, BF16) for w in to_cast]
    scratch = [pltpu.VMEM((FFN_ROWS, D_FF), BF16)]
    if n_load_steps:
        weight_specs = [streamed(w, n_load_steps, 0) for w in (wg, wu, wd)]
        scratch += [pltpu.VMEM(w.shape, BF16) for w in (wg, wu, wd)]
    else:
        weight_specs = [_resident(w.shape) for w in (wg, wu, wd)]
    return pl.pallas_call(
        functools.partial(_ffn_body, final_norm=final_norm, n_cast=len(to_cast), subtiles=subtiles,
                          n_load_steps=n_load_steps),
        grid=(n_load_steps + 1 + n_prompt_tiles,),
        in_specs=[prompt_spec, sample_spec, _resident((1, D_MODEL))] + weight_specs + [_resident((1, D_MODEL))]
        + cast_specs,
        out_specs=[prompt_spec, sample_spec] + cast_specs,
        out_shape=[jax.ShapeDtypeStruct(xp.shape, F32), jax.ShapeDtypeStruct(xs.shape, F32)] + cast_shapes,
        scratch_shapes=scratch,
        compiler_params=pltpu.CompilerParams(dimension_semantics=("arbitrary",), vmem_limit_bytes=VMEM_LIMIT),
        name="ffn_half_final" if final_norm else "ffn_half",
    )(xp, xs, norm, wg, wu, wd, fnorm, *to_cast)


def _pair_heads(t):
    swapped = pltpu.roll(t, HEAD_DIM, 1)
    left = lax.broadcasted_iota(jnp.int32, t.shape, 1) < HEAD_DIM
    return jnp.where(left, t, swapped), jnp.where(left, swapped, t)


def _gate_norm(g, vnorm_ref, gi):
    gg = g[:, gi * GROUP_CH:(gi + 1) * GROUP_CH]
    return _rms(gg, vnorm_ref[gi:gi + 1, :])


def _merge(x, attn, gate, anorm_ref, gnorm_ref, wout_ref):
    cat = jnp.concatenate([_rms(attn, anorm_ref[...]), _rms(gate, gnorm_ref[...])], axis=1)
    return x + _dot(cat.astype(BF16), wout_ref[...])


def _mixer_prompt_body(x_ref, mnorm_ref, win_ref, sinks_ref, vnorm_ref, ws_ref, bst_ref, anorm_ref, gnorm_ref,
                       wout_ref, o_ref, pk_ref, pv_ref, kv_s, q_s, attn_s, gate_s, bias_s, wtril_s):
    b = pl.program_id(0)
    t = pl.program_id(1)
    nblk = MIX_ROWS // WINDOW
    band = 2 * WINDOW
    qrows = GQA_GROUP * WINDOW

    @pl.when((b == 0) & (t == 0))
    def _init_tables():
        row = lax.broadcasted_iota(jnp.int32, (qrows, band), 0)
        key = lax.broadcasted_iota(jnp.int32, (qrows, band), 1)
        grp = row >> 7
        dist = WINDOW + (row & (WINDOW - 1)) - key
        valid = (dist >= 0) & (dist < WINDOW)
        distf = dist.astype(F32)
        for h in range(N_KV_HEADS):
            s = [ALIBI_SLOPES[h * GQA_GROUP + g] for g in range(GQA_GROUP)]
            slope = jnp.where(grp == 0, s[0], jnp.where(grp == 1, s[1], jnp.where(grp == 2, s[2], s[3])))
            bias = jnp.where(valid, -(slope * distf) * LOG2E, MASK_VALUE)
            bias_s[0, h] = bias
            bias_s[1, h] = jnp.where(key < WINDOW, MASK_VALUE, bias)
        r = lax.broadcasted_iota(jnp.int32, (CHUNK, CHUNK), 0)
        c = lax.broadcasted_iota(jnp.int32, (CHUNK, CHUNK), 1)
        for gi in range(N_GATE_GROUPS):
            wtril_s[gi] = jnp.where(r >= c, ws_ref[gi], 0.0).astype(BF16)

    @pl.when(t == 0)
    def _no_previous_block():
        kv_s[0:WINDOW, :] = jnp.zeros((WINDOW, 4 * KV_WIDTH), BF16)

    x = x_ref[...]
    h_in = _rms(x, mnorm_ref[...]).astype(BF16)
    q_s[...] = _dot(h_in, win_ref[:, 0:ATTN_WIDTH]) * (ATTN_SCALE * LOG2E)
    wkv = win_ref[:, ATTN_WIDTH:ATTN_WIDTH + 2 * KV_WIDTH]
    pkv = jnp.concatenate([_dot(h_in[0:MIX_ROWS // 2, :], wkv), _dot(h_in[MIX_ROWS // 2:, :], wkv)], axis=0)
    k_new = pkv[:, 0:KV_WIDTH]
    v_new = pkv[:, KV_WIDTH:2 * KV_WIDTH]
    pk_ref[0] = k_new[MIX_ROWS - WINDOW:, :]
    pv_ref[0] = v_new[MIX_ROWS - WINDOW:, :]
    k0, k1 = _pair_heads(k_new)
    v0, v1 = _pair_heads(v_new)
    kv_s[WINDOW:, :] = jnp.concatenate([k0, k1, v0, v1], axis=1).astype(BF16)

    table = jnp.where(t == 0, 1, 0)
    lane_grp = lax.broadcasted_iota(jnp.int32, (WINDOW, 2 * KV_WIDTH), 1) >> 6
    for j in range(nblk):
        rows = slice(j * WINDOW, (j + 1) * WINDOW)
        kvb = kv_s[j * WINDOW:j * WINDOW + band, :]
        for h in range(N_KV_HEADS):
            ka = kvb[:, h * KV_WIDTH:(h + 1) * KV_WIDTH]
            va = kvb[:, (2 + h) * KV_WIDTH:(3 + h) * KV_WIDTH]
            k4 = jnp.concatenate([ka, ka], axis=1)
            v4 = jnp.concatenate([va, va], axis=1)
            q = q_s[rows, h * 2 * KV_WIDTH:(h + 1) * 2 * KV_WIDTH]
            qstack = jnp.concatenate([jnp.where(lane_grp == g, q, 0.0) for g in range(GQA_GROUP)],
                                     axis=0).astype(BF16)
            bias = bias_s[table, h] if j == 0 else bias_s[0, h]
            s = _dot_nt(qstack, k4) + bias
            probs, inv = [], []
            for g in range(GQA_GROUP):
                sg = s[g * WINDOW:(g + 1) * WINDOW, :]
                sink = sinks_ref[h * GQA_GROUP + g] * LOG2E
                m = jnp.maximum(jnp.max(sg, axis=1, keepdims=True), sink)
                p = jnp.exp2(sg - m)
                inv.append(1.0 / (jnp.sum(p, axis=1, keepdims=True) + jnp.exp2(sink - m)))
                probs.append(p.astype(BF16))
            o = _dot(jnp.concatenate(probs, axis=0), v4)
            for g in range(GQA_GROUP):
                lanes = slice(g * HEAD_DIM, (g + 1) * HEAD_DIM)
                out_lanes = slice((h * GQA_GROUP + g) * HEAD_DIM, (h * GQA_GROUP + g + 1) * HEAD_DIM)
                attn_s[rows, out_lanes] = o[g * WINDOW:(g + 1) * WINDOW, lanes] * inv[g]
    kv_s[0:WINDOW, :] = kv_s[MIX_ROWS:MIX_ROWS + WINDOW, :]

    u = jax.nn.gelu(_dot(h_in, win_ref[:, 768:1280]))
    gact = jax.nn.gelu(_dot(h_in, win_ref[:, 1280:1792]))
    for gi in range(N_GATE_GROUPS):
        lanes = slice(gi * GROUP_CH, (gi + 1) * GROUP_CH)
        gn = _gate_norm(gact, vnorm_ref, gi).astype(BF16)
        chunks = jnp.concatenate([gn[c * CHUNK:(c + 1) * CHUNK, :] for c in range(nblk)], axis=1)
        mixed = _dot(wtril_s[gi], chunks) + bst_ref[:, gi:gi + 1]
        for c in range(nblk):
            rows = slice(c * CHUNK, (c + 1) * CHUNK)
            gate_s[rows, lanes] = u[rows, lanes] * mixed[:, c * GROUP_CH:(c + 1) * GROUP_CH]

    o_ref[...] = _merge(x, attn_s[...], gate_s[...], anorm_ref, gnorm_ref, wout_ref)


def _mixer_prompt(x, mnorm, win, sinks, vnorm, ws, bst, anorm, gnorm, wout, *, batch):
    rows = x.shape[0]
    tiles = rows // (batch * MIX_ROWS)
    assert rows == batch * tiles * MIX_ROWS
    band = 2 * WINDOW
    row_spec = pl.BlockSpec((MIX_ROWS, D_MODEL), lambda b, t: (b * tiles + t, 0))
    kv_spec = pl.BlockSpec((1, WINDOW, KV_WIDTH), lambda b, t: (b, 0, 0))
    kv_shape = jax.ShapeDtypeStruct((batch, WINDOW, KV_WIDTH), F32)
    return pl.pallas_call(
        _mixer_prompt_body,
        grid=(batch, tiles),
        in_specs=[row_spec, _resident((1, D_MODEL)), _resident(win.shape), _smem(), _resident(vnorm.shape),
                  _resident(ws.shape), _resident(bst.shape), _resident(anorm.shape), _resident(gnorm.shape),
                  _resident(wout.shape)],
        out_specs=[row_spec, kv_spec, kv_spec],
        out_shape=[jax.ShapeDtypeStruct(x.shape, F32), kv_shape, kv_shape],
        scratch_shapes=[
            pltpu.VMEM((WINDOW + MIX_ROWS, 4 * KV_WIDTH), BF16),
            pltpu.VMEM((MIX_ROWS, ATTN_WIDTH), F32),
            pltpu.VMEM((MIX_ROWS, ATTN_WIDTH), F32),
            pltpu.VMEM((MIX_ROWS, GMLP_WIDTH), F32),
            pltpu.VMEM((2, N_KV_HEADS, GQA_GROUP * WINDOW, band), F32),
            pltpu.VMEM((N_GATE_GROUPS, CHUNK, CHUNK), BF16),
        ],
        compiler_params=pltpu.CompilerParams(dimension_semantics=("arbitrary", "arbitrary"),
                                             vmem_limit_bytes=VMEM_LIMIT),
        name="mixer_prompt",
    )(x, mnorm, win, sinks, vnorm, ws, bst, anorm, gnorm, wout)


def _mixer_sample_body(x_ref, mnorm_ref, win_ref, sinks_ref, vnorm_ref, wsm_ref, bsm_ref, anorm_ref, gnorm_ref,
                       wout_ref, ck_ref, cv_ref, o_ref, gn_ref, sk_ref, sv_ref,
                       perm_s, qall_s, oall_s, kvt_s, gate_s, bias_s, sink_s, s_s, p_s, *, n_seq, n_tok):
    step = pl.program_id(0)
    n_steps = pl.num_programs(0)
    n_rows = n_seq * n_tok
    q_rows = N_HEADS * n_tok
    keys = 2 * WINDOW
    step_rows = SEQ_PER_STEP * q_rows

    def by_token(val):
        for c in range(val.shape[1] // LANES):
            perm_s[c] = val[:, c * LANES:(c + 1) * LANES]
        return jnp.concatenate(
            [jnp.concatenate([perm_s[c, pl.ds(i, n_seq, stride=n_tok), :] for i in range(n_tok)], axis=0)
             for c in range(val.shape[1] // LANES)], axis=1)

    def by_sequence(val):
        for c in range(val.shape[1] // LANES):
            for i in range(n_tok):
                perm_s[c, pl.ds(i, n_seq, stride=n_tok), :] = val[i * n_seq:(i + 1) * n_seq, c * LANES:(c + 1) * LANES]
        return jnp.concatenate([perm_s[c] for c in range(val.shape[1] // LANES)], axis=1)

    @pl.when(step == 0)
    def _project():
        h = _rms(x_ref[...], mnorm_ref[...])
        pkv = _dot(h.astype(BF16), win_ref[:, ATTN_WIDTH:ATTN_WIDTH + 2 * KV_WIDTH])
        kvt_s[...] = pkv.T
        h_in = by_token(h).astype(BF16)
        pq = _dot(h_in, win_ref[:, 0:ATTN_WIDTH]) * (ATTN_SCALE * LOG2E)
        half = lax.broadcasted_iota(jnp.int32, (n_rows, LANES), 1) >> 6
        for hg in range(N_HEADS):
            kvh = hg // GQA_GROUP
            slab = pq[:, (hg // 2) * LANES:(hg // 2 + 1) * LANES]
            if (hg % 2) != kvh:
                slab = pltpu.roll(slab, HEAD_DIM, 1)
            qall_s[hg * n_rows:(hg + 1) * n_rows, :] = jnp.where(half == kvh, slab, 0.0)

        u = jax.nn.gelu(_dot(h_in, win_ref[:, 768:1280]))
        gact = jax.nn.gelu(_dot(h_in, win_ref[:, 1280:1792]))
        gns = []
        for gi in range(N_GATE_GROUPS):
            lanes = slice(gi * GROUP_CH, (gi + 1) * GROUP_CH)
            gn = _gate_norm(gact, vnorm_ref, gi)
            gns.append(gn)
            for i in range(n_tok):
                rows = slice(i * n_seq, (i + 1) * n_seq)
                mixed = jnp.full((n_seq, GROUP_CH), bsm_ref[gi * n_tok + i], F32)
                for j in range(i + 1):
                    mixed = mixed + wsm_ref[(gi * n_tok + i) * n_tok + j] * gn[j * n_seq:(j + 1) * n_seq, :]
                gate_s[rows, lanes] = u[rows, lanes] * mixed
        gn_ref[...] = by_sequence(jnp.concatenate(gns, axis=1))

        row = lax.broadcasted_iota(jnp.int32, (step_rows, keys), 0)
        col = lax.broadcasted_iota(jnp.int32, (step_rows, keys), 1)
        tok = row & (n_tok - 1)
        head = (row >> 2) & (N_HEADS - 1)
        slope = jnp.zeros((step_rows, keys), F32)
        for hg in range(N_HEADS):
            slope = jnp.where(head == hg, ALIBI_SLOPES[hg], slope)
        shifted = col < WINDOW
        pos = jnp.where(shifted, col + n_tok, col - WINDOW)
        dist = WINDOW + tok - pos
        valid = (dist >= 0) & (dist < WINDOW) & (shifted | (pos < n_tok))
        bias_s[...] = jnp.where(valid, -(slope * dist.astype(F32)) * LOG2E, MASK_VALUE)
        head1 = (lax.broadcasted_iota(jnp.int32, (step_rows, 1), 0) >> 2) & (N_HEADS - 1)
        sink = jnp.zeros((step_rows, 1), F32)
        for hg in range(N_HEADS):
            sink = jnp.where(head1 == hg, sinks_ref[hg] * LOG2E, sink)
        sink_s[...] = sink

    col0 = pl.multiple_of(step * LANES, LANES)
    k_cols = kvt_s[0:KV_WIDTH, pl.ds(col0, LANES)]
    v_cols = kvt_s[KV_WIDTH:2 * KV_WIDTH, pl.ds(col0, LANES)]
    is_new = lax.broadcasted_iota(jnp.int32, (KV_WIDTH, WINDOW), 1) >= WINDOW - n_tok
    for bl in range(SEQ_PER_STEP):
        to_tail = (WINDOW - n_tok - n_tok * bl) % LANES
        sk_ref[bl] = jnp.where(is_new, pltpu.roll(k_cols, to_tail, 1), pltpu.roll(ck_ref[bl], WINDOW - n_tok, 1))
        sv_ref[bl] = jnp.where(is_new, pltpu.roll(v_cols, to_tail, 1), pltpu.roll(cv_ref[bl], WINDOW - n_tok, 1))

    seq0 = step * SEQ_PER_STEP
    for bl in range(SEQ_PER_STEP):
        qb = qall_s[pl.ds(seq0 + bl, q_rows, stride=n_seq), :].astype(BF16)
        k2 = jnp.concatenate([sk_ref[bl], ck_ref[bl]], axis=1).astype(BF16)
        s_s[bl * q_rows:(bl + 1) * q_rows, :] = _dot(qb, k2)
    s = s_s[...] + bias_s[...]
    sink = sink_s[...]
    m = jnp.maximum(jnp.max(s, axis=1, keepdims=True), sink)
    p = jnp.exp2(s - m)
    inv = 1.0 / (jnp.sum(p, axis=1, keepdims=True) + jnp.exp2(sink - m))
    p_s[...] = p.astype(BF16)
    for bl in range(SEQ_PER_STEP):
        rows = slice(bl * q_rows, (bl + 1) * q_rows)
        v2 = jnp.concatenate([sv_ref[bl], cv_ref[bl]], axis=1).astype(BF16)
        oall_s[pl.ds(seq0 + bl, q_rows, stride=n_seq), :] = _dot_nt(p_s[rows, :], v2) * inv[rows, :]

    @pl.when(step == n_steps - 1)
    def _merge_rows():
        left = lax.broadcasted_iota(jnp.int32, (n_rows, LANES), 1) < HEAD_DIM
        slabs = []
        for pair in range(N_HEADS // 2):
            halves = []
            for hg in (2 * pair, 2 * pair + 1):
                o = oall_s[hg * n_rows:(hg + 1) * n_rows, :]
                if (hg % 2) != (hg // GQA_GROUP):
                    o = pltpu.roll(o, HEAD_DIM, 1)
                halves.append(o)
            slabs.append(jnp.where(left, halves[0], halves[1]))
        attn = jnp.concatenate(slabs, axis=1)
        o_ref[...] = by_sequence(_merge(by_token(x_ref[...]), attn, gate_s[...], anorm_ref, gnorm_ref, wout_ref))


def _mixer_sample(x, mnorm, win, sinks, vnorm, wsm, bsm, anorm, gnorm, wout, ck, cv, *, n_seq, n_tok):
    n_rows = n_seq * n_tok
    q_rows = N_HEADS * n_tok
    assert x.shape[0] == n_rows and n_seq % SEQ_PER_STEP == 0 and n_tok == 4 and n_seq == LANES
    assert SEQ_PER_STEP * n_tok == LANES
    cache_spec = pl.BlockSpec((SEQ_PER_STEP, KV_WIDTH, WINDOW), lambda s: (s, 0, 0))
    cache_shape = jax.ShapeDtypeStruct(ck.shape, F32)
    rows_spec = pl.BlockSpec((n_rows, D_MODEL), lambda s: (0, 0))
    gn_spec = pl.BlockSpec((n_rows, GMLP_WIDTH), lambda s: (0, 0))
    return pl.pallas_call(
        functools.partial(_mixer_sample_body, n_seq=n_seq, n_tok=n_tok),
        grid=(n_seq // SEQ_PER_STEP,),
        in_specs=[rows_spec, _resident((1, D_MODEL)), _resident(win.shape), _smem(), _resident(vnorm.shape),
                  _smem(), _smem(), _resident(anorm.shape), _resident(gnorm.shape), _resident(wout.shape),
                  cache_spec, cache_spec],
        out_specs=[rows_spec, gn_spec, cache_spec, cache_spec],
        out_shape=[jax.ShapeDtypeStruct(x.shape, F32), jax.ShapeDtypeStruct((n_rows, GMLP_WIDTH), F32),
                   cache_shape, cache_shape],
        scratch_shapes=[
            pltpu.VMEM((D_MODEL // LANES, n_rows, LANES), F32),
            pltpu.VMEM((N_HEADS * n_rows, LANES), F32),
            pltpu.VMEM((N_HEADS * n_rows, LANES), F32),
            pltpu.VMEM((2 * KV_WIDTH, n_rows), F32),
            pltpu.VMEM((n_rows, GMLP_WIDTH), F32),
            pltpu.VMEM((SEQ_PER_STEP * q_rows, 2 * WINDOW), F32),
            pltpu.VMEM((SEQ_PER_STEP * q_rows, 1), F32),
            pltpu.VMEM((SEQ_PER_STEP * q_rows, 2 * WINDOW), F32),
            pltpu.VMEM((SEQ_PER_STEP * q_rows, 2 * WINDOW), BF16),
        ],
        compiler_params=pltpu.CompilerParams(dimension_semantics=("arbitrary",), vmem_limit_bytes=VMEM_LIMIT),
        name="mixer_sample",
    )(x, mnorm, win, sinks, vnorm, wsm, bsm, anorm, gnorm, wout, ck, cv)


def kernel(x_prompt, x_sample, cache_k, cache_v, ffn1_norm, ffn1_w_gate, ffn1_w_up, ffn1_w_down, mix_norm, w_in,
           attn_sinks, gmlp_v_norm, gmlp_w_spatial, gmlp_b_spatial, attn_out_norm, gmlp_out_norm, w_out, ffn2_norm,
           ffn2_w_gate, ffn2_w_up, ffn2_w_down, final_norm):
    batch, seq, _ = x_prompt.shape
    n_seq, n_tok, _ = x_sample.shape
    depth = cache_k.shape[0]
    assert depth == 1

    xp = x_prompt.reshape(batch * seq, D_MODEL)
    xs = x_sample.reshape(n_seq * n_tok, D_MODEL)
    row = lambda v: v.reshape(1, -1)
    fnorm = row(final_norm)
    l = 0
    sinks = attn_sinks[l]
    vnorm = gmlp_v_norm[l]
    ws = gmlp_w_spatial[l]
    bs = gmlp_b_spatial[l]
    mnorm, anorm, gnorm = row(mix_norm[l]), row(attn_out_norm[l]), row(gmlp_out_norm[l])

    later = (ffn2_w_gate[l], ffn2_w_up[l], ffn2_w_down[l], w_in[l], w_out[l])
    xp, xs, wg2, wu2, wd2, win, wout = _ffn_half(xp, xs, row(ffn1_norm[l]), ffn1_w_gate[l], ffn1_w_up[l],
                                                 ffn1_w_down[l], fnorm, final_norm=False, subtiles=1, to_cast=later)
    xp, pk, pv = _mixer_prompt(xp, mnorm, win, sinks, vnorm, ws, bs.T, anorm, gnorm, wout, batch=batch)
    to_cols = lambda c: jnp.transpose(c, (0, 2, 3, 1)).reshape(n_seq, KV_WIDTH, WINDOW)
    to_rows = lambda c: jnp.transpose(c.reshape(n_seq, N_KV_HEADS, HEAD_DIM, WINDOW), (0, 3, 1, 2))[None]
    xs, gn, sk, sv = _mixer_sample(xs, mnorm, win, sinks, vnorm, ws[:, :n_tok, :n_tok].reshape(-1),
                                   bs[:, :n_tok].reshape(-1), anorm, gnorm, wout, to_cols(cache_k[l]),
                                   to_cols(cache_v[l]), n_seq=n_seq, n_tok=n_tok)
    yp, ys = _ffn_half(xp, xs, row(ffn2_norm[l]), wg2, wu2, wd2, fnorm, final_norm=True, subtiles=2)

    kv5 = lambda a, n: a.reshape(1, n, WINDOW, N_KV_HEADS, HEAD_DIM)
    y_prompt = yp.reshape(batch, seq, D_MODEL)
    y_sample = ys.reshape(n_seq, n_tok, D_MODEL)
    chunk_v = gn.reshape(1, n_seq, n_tok, GMLP_WIDTH)
    return (y_prompt, y_sample, kv5(pk, batch), kv5(pv, batch), to_rows(sk), to_rows(sv), chunk_v)
```

```python
import functools

import jax
import jax.numpy as jnp
from jax import lax
from jax.experimental import pallas as pl
from jax.experimental.pallas import tpu as pltpu

F32 = jnp.float32
BF16 = jnp.bfloat16

D_MODEL = 1024
D_FF = 2816
N_HEADS = 8
N_KV_HEADS = 2
GQA_GROUP = 4
HEAD_DIM = 64
ATTN_WIDTH = 512
KV_WIDTH = 128
GMLP_WIDTH = 512
N_GATE_GROUPS = 4
GROUP_CH = 128
WINDOW = 128
CHUNK = 128
RMS_EPS = 1e-6
FFN_RESIDUAL = 0.5
ATTN_SCALE = 0.125
LOG2E = 1.4426950408889634
ALIBI_SLOPES = tuple(2.0 ** (-(i + 1)) for i in range(N_HEADS))
MASK_VALUE = -1e30

LANES = 128
BF16_SUBLANES = 16
FFN_ROWS = 512
FFN_LOAD_STEPS = 16
FFN_CHUNK = 256
MIX_ROWS = 512
SEQ_PER_STEP = 32
VMEM_LIMIT = 56 * 1024 * 1024


def _rms(x, g):
    ms = jnp.mean(x * x, axis=-1, keepdims=True)
    return x * lax.rsqrt(ms + RMS_EPS) * g


def _dot(a, b):
    return jnp.dot(a, b, preferred_element_type=F32)


def _dot_nt(a, b):
    return lax.dot_general(a, b, (((1,), (1,)), ((), ())), preferred_element_type=F32)


def _resident(shape):
    nd = len(shape)
    return pl.BlockSpec(shape, lambda *_: (0,) * nd, pipeline_mode=pl.Buffered(1))


def _smem():
    return pl.BlockSpec(memory_space=pltpu.SMEM)


def _ffn_body(*refs, final_norm, n_cast, subtiles, n_load_steps, tile_rows):
    xp_ref, xs_ref, norm_ref, wg_in, wu_in, wd_in, fnorm_ref = refs[:7]
    cast_in = refs[7:7 + n_cast]
    op_ref, os_ref = refs[7 + n_cast:9 + n_cast]
    cast_out = refs[9 + n_cast:9 + 2 * n_cast]
    act_ref = refs[9 + 2 * n_cast]
    i = pl.program_id(0)
    first = n_load_steps

    if n_load_steps:
        wg_ref, wu_ref, wd_ref = refs[10 + 2 * n_cast:]

        @pl.when(i < n_load_steps)
        def _load_weights():
            for src, dst in ((wg_in, wg_ref), (wu_in, wu_ref), (wd_in, wd_ref)):
                br = src.shape[0]
                dst[pl.ds(pl.multiple_of(i * br, br), br), :] = src[...].astype(BF16)
    else:
        wg_ref, wu_ref, wd_ref = wg_in, wu_in, wd_in

    for src, dst in zip(cast_in, cast_out):
        dst[...] = src[...].astype(BF16)

    def _sub_tile(k, carry):
        rows = pl.ds(pl.multiple_of(k * tile_rows, tile_rows), tile_rows)
        xs_rows = jnp.concatenate([xs_ref[...]] * (tile_rows // FFN_ROWS), axis=0)
        x = jnp.where(i == first, xs_rows, xp_ref[rows, :])
        h = _rms(x, norm_ref[...]).astype(BF16)
        for c in range(D_FF // FFN_CHUNK):
            cols = slice(c * FFN_CHUNK, (c + 1) * FFN_CHUNK)
            a = _dot(h, wg_ref[:, cols])
            b = _dot(h, wu_ref[:, cols])
            act_ref[:, cols] = (a * jax.nn.sigmoid(a) * b).astype(BF16)
        y = x + FFN_RESIDUAL * _dot(act_ref[...], wd_ref[...])
        if final_norm:
            y = _rms(y, fnorm_ref[...])
        op_ref[rows, :] = y
        return carry

    trips = jnp.where(i < first, 0, jnp.where(i == first, 1, subtiles))
    lax.fori_loop(0, trips, _sub_tile, 0)

    @pl.when(i == first)
    def _():
        os_ref[...] = op_ref[0:FFN_ROWS, :]


def _block_rows(rows, n_steps):
    br = BF16_SUBLANES
    while rows % br or rows // br > n_steps:
        br += BF16_SUBLANES
    return br


def _ffn_half(xp, xs, norm, wg, wu, wd, fnorm, *, final_norm, subtiles, to_cast=(), tile_rows=FFN_ROWS):
    block_rows = subtiles * tile_rows
    n_prompt_tiles = xp.shape[0] // block_rows
    assert xp.shape[0] == n_prompt_tiles * block_rows and xs.shape[0] == FFN_ROWS
    n_load_steps = FFN_LOAD_STEPS if wg.dtype == F32 else 0
    first = n_load_steps
    prompt_spec = pl.BlockSpec((block_rows, D_MODEL), lambda i: (jnp.maximum(i - first - 1, 0), 0))
    sample_spec = pl.BlockSpec((FFN_ROWS, D_MODEL), lambda i: (0, 0))

    def streamed(w, n_steps, start):
        br = _block_rows(w.shape[0], n_steps)
        return pl.BlockSpec((br, w.shape[1]), functools.partial(
            lambda i, last: (jnp.clip(i - start, 0, last), 0), last=w.shape[0] // br - 1))

    cast_specs = [streamed(w, n_prompt_tiles, first) for w in to_cast]
    cast_shapes = [jax.ShapeDtypeStruct(w.shape, BF16) for w in to_cast]
    scratch = [pltpu.VMEM((tile_rows, D_FF), BF16)]
    if n_load_steps:
        weight_specs = [streamed(w, n_load_steps, 0) for w in (wg, wu, wd)]
        scratch += [pltpu.VMEM(w.shape, BF16) for w in (wg, wu, wd)]
    else:
        weight_specs = [_resident(w.shape) for w in (wg, wu, wd)]
    return pl.pallas_call(
        functools.partial(_ffn_body, final_norm=final_norm, n_cast=len(to_cast), subtiles=subtiles,
                          n_load_steps=n_load_steps, tile_rows=tile_rows),
        grid=(n_load_steps + 1 + n_prompt_tiles,),
        in_specs=[prompt_spec, sample_spec, _resident((1, D_MODEL))] + weight_specs + [_resident((1, D_MODEL))]
        + cast_specs,
        out_specs=[prompt_spec, sample_spec] + cast_specs,
        out_shape=[jax.ShapeDtypeStruct(xp.shape, F32), jax.ShapeDtypeStruct(xs.shape, F32)] + cast_shapes,
        scratch_shapes=scratch,
        compiler_params=pltpu.CompilerParams(dimension_semantics=("arbitrary",), vmem_limit_bytes=VMEM_LIMIT),
        name="ffn_half_final" if final_norm else "ffn_half",
    )(xp, xs, norm, wg, wu, wd, fnorm, *to_cast)


def _pair_heads(t):
    swapped = pltpu.roll(t, HEAD_DIM, 1)
    left = lax.broadcasted_iota(jnp.int32, t.shape, 1) < HEAD_DIM
    return jnp.where(left, t, swapped), jnp.where(left, swapped, t)


def _gate_norm(g, vnorm_ref, gi):
    gg = g[:, gi * GROUP_CH:(gi + 1) * GROUP_CH]
    return _rms(gg, vnorm_ref[gi:gi + 1, :])


def _merge(x, attn, gate, anorm_ref, gnorm_ref, wout_ref):
    cat = jnp.concatenate([_rms(attn, anorm_ref[...]), _rms(gate, gnorm_ref[...])], axis=1)
    return x + _dot(cat.astype(BF16), wout_ref[...])


def _mixer_prompt_body(x_ref, mnorm_ref, win_ref, sinks_ref, vnorm_ref, ws_ref, bst_ref, anorm_ref, gnorm_ref,
                       wout_ref, o_ref, pk_ref, pv_ref, kv_s, q_s, attn_s, gate_s, bias_s, wtril_s):
    b = pl.program_id(0)
    t = pl.program_id(1)
    nblk = MIX_ROWS // WINDOW
    band = 2 * WINDOW
    qrows = GQA_GROUP * WINDOW

    @pl.when((b == 0) & (t == 0))
    def _init_tables():
        row = lax.broadcasted_iota(jnp.int32, (qrows, band), 0)
        key = lax.broadcasted_iota(jnp.int32, (qrows, band), 1)
        grp = row >> 7
        dist = WINDOW + (row & (WINDOW - 1)) - key
        valid = (dist >= 0) & (dist < WINDOW)
        distf = dist.astype(F32)
        for h in range(N_KV_HEADS):
            s = [ALIBI_SLOPES[h * GQA_GROUP + g] for g in range(GQA_GROUP)]
            slope = jnp.where(grp == 0, s[0], jnp.where(grp == 1, s[1], jnp.where(grp == 2, s[2], s[3])))
            bias = jnp.where(valid, -(slope * distf) * LOG2E, MASK_VALUE)
            bias_s[0, h] = bias
            bias_s[1, h] = jnp.where(key < WINDOW, MASK_VALUE, bias)
        r = lax.broadcasted_iota(jnp.int32, (CHUNK, CHUNK), 0)
        c = lax.broadcasted_iota(jnp.int32, (CHUNK, CHUNK), 1)
        for gi in range(N_GATE_GROUPS):
            wtril_s[gi] = jnp.where(r >= c, ws_ref[gi], 0.0).astype(BF16)

    @pl.when(t == 0)
    def _no_previous_block():
        kv_s[0:WINDOW, :] = jnp.zeros((WINDOW, 4 * KV_WIDTH), BF16)

    x = x_ref[...]
    h_in = _rms(x, mnorm_ref[...]).astype(BF16)
    q_s[...] = _dot(h_in, win_ref[:, 0:ATTN_WIDTH]) * (ATTN_SCALE * LOG2E)
    wkv = win_ref[:, ATTN_WIDTH:ATTN_WIDTH + 2 * KV_WIDTH]
    pkv = jnp.concatenate([_dot(h_in[0:MIX_ROWS // 2, :], wkv), _dot(h_in[MIX_ROWS // 2:, :], wkv)], axis=0)
    k_new = pkv[:, 0:KV_WIDTH]
    v_new = pkv[:, KV_WIDTH:2 * KV_WIDTH]
    pk_ref[0] = k_new[MIX_ROWS - WINDOW:, :]
    pv_ref[0] = v_new[MIX_ROWS - WINDOW:, :]
    k0, k1 = _pair_heads(k_new)
    v0, v1 = _pair_heads(v_new)
    kv_s[WINDOW:, :] = jnp.concatenate([k0, k1, v0, v1], axis=1).astype(BF16)

    table = jnp.where(t == 0, 1, 0)
    lane_grp = lax.broadcasted_iota(jnp.int32, (WINDOW, 2 * KV_WIDTH), 1) >> 6
    for j in range(nblk):
        rows = slice(j * WINDOW, (j + 1) * WINDOW)
        kvb = kv_s[j * WINDOW:j * WINDOW + band, :]
        for h in range(N_KV_HEADS):
            ka = kvb[:, h * KV_WIDTH:(h + 1) * KV_WIDTH]
            va = kvb[:, (2 + h) * KV_WIDTH:(3 + h) * KV_WIDTH]
            k4 = jnp.concatenate([ka, ka], axis=1)
            v4 = jnp.concatenate([va, va], axis=1)
            q = q_s[rows, h * 2 * KV_WIDTH:(h + 1) * 2 * KV_WIDTH]
            qstack = jnp.concatenate([jnp.where(lane_grp == g, q, 0.0) for g in range(GQA_GROUP)],
                                     axis=0).astype(BF16)
            bias = bias_s[table, h] if j == 0 else bias_s[0, h]
            s = _dot_nt(qstack, k4) + bias
            probs, inv = [], []
            for g in range(GQA_GROUP):
                sg = s[g * WINDOW:(g + 1) * WINDOW, :]
                sink = sinks_ref[h * GQA_GROUP + g] * LOG2E
                m = jnp.maximum(jnp.max(sg, axis=1, keepdims=True), sink)
                p = jnp.exp2(sg - m)
                inv.append(1.0 / (jnp.sum(p, axis=1, keepdims=True) + jnp.exp2(sink - m)))
                probs.append(p.astype(BF16))
            o = _dot(jnp.concatenate(probs, axis=0), v4)
            for g in range(GQA_GROUP):
                lanes = slice(g * HEAD_DIM, (g + 1) * HEAD_DIM)
                out_lanes = slice((h * GQA_GROUP + g) * HEAD_DIM, (h * GQA_GROUP + g + 1) * HEAD_DIM)
                attn_s[rows, out_lanes] = o[g * WINDOW:(g + 1) * WINDOW, lanes] * inv[g]
    kv_s[0:WINDOW, :] = kv_s[MIX_ROWS:MIX_ROWS + WINDOW, :]

    u = jax.nn.gelu(_dot(h_in, win_ref[:, 768:1280]))
    gact = jax.nn.gelu(_dot(h_in, win_ref[:, 1280:1792]))
    for gi in range(N_GATE_GROUPS):
        lanes = slice(gi * GROUP_CH, (gi + 1) * GROUP_CH)
        gn = _gate_norm(gact, vnorm_ref, gi).astype(BF16)
        chunks = jnp.concatenate([gn[c * CHUNK:(c + 1) * CHUNK, :] for c in range(nblk)], axis=1)
        mixed = _dot(wtril_s[gi], chunks) + bst_ref[:, gi:gi + 1]
        for c in range(nblk):
            rows = slice(c * CHUNK, (c + 1) * CHUNK)
            gate_s[rows, lanes] = u[rows, lanes] * mixed[:, c * GROUP_CH:(c + 1) * GROUP_CH]

    o_ref[...] = _merge(x, attn_s[...], gate_s[...], anorm_ref, gnorm_ref, wout_ref)


def _mixer_prompt(x, mnorm, win, sinks, vnorm, ws, bst, anorm, gnorm, wout, *, batch):
    rows = x.shape[0]
    tiles = rows // (batch * MIX_ROWS)
    assert rows == batch * tiles * MIX_ROWS
    band = 2 * WINDOW
    row_spec = pl.BlockSpec((MIX_ROWS, D_MODEL), lambda b, t: (b * tiles + t, 0))
    kv_spec = pl.BlockSpec((1, WINDOW, KV_WIDTH), lambda b, t: (b, 0, 0))
    kv_shape = jax.ShapeDtypeStruct((batch, WINDOW, KV_WIDTH), F32)
    return pl.pallas_call(
        _mixer_prompt_body,
        grid=(batch, tiles),
        in_specs=[row_spec, _resident((1, D_MODEL)), _resident(win.shape), _smem(), _resident(vnorm.shape),
                  _resident(ws.shape), _resident(bst.shape), _resident(anorm.shape), _resident(gnorm.shape),
                  _resident(wout.shape)],
        out_specs=[row_spec, kv_spec, kv_spec],
        out_shape=[jax.ShapeDtypeStruct(x.shape, F32), kv_shape, kv_shape],
        scratch_shapes=[
            pltpu.VMEM((WINDOW + MIX_ROWS, 4 * KV_WIDTH), BF16),
            pltpu.VMEM((MIX_ROWS, ATTN_WIDTH), F32),
            pltpu.VMEM((MIX_ROWS, ATTN_WIDTH), F32),
            pltpu.VMEM((MIX_ROWS, GMLP_WIDTH), F32),
            pltpu.VMEM((2, N_KV_HEADS, GQA_GROUP * WINDOW, band), F32),
            pltpu.VMEM((N_GATE_GROUPS, CHUNK, CHUNK), BF16),
        ],
        compiler_params=pltpu.CompilerParams(dimension_semantics=("arbitrary", "arbitrary"),
                                             vmem_limit_bytes=VMEM_LIMIT),
        name="mixer_prompt",
    )(x, mnorm, win, sinks, vnorm, ws, bst, anorm, gnorm, wout)


def _mixer_sample_body(x_ref, mnorm_ref, win_ref, sinks_ref, vnorm_ref, wsm_ref, bsm_ref, anorm_ref, gnorm_ref,
                       wout_ref, ck_ref, cv_ref, o_ref, gn_ref, sk_ref, sv_ref,
                       perm_s, qall_s, oall_s, kvt_s, gate_s, bias_s, sink_s, s_s, p_s, *, n_seq, n_tok):
    step = pl.program_id(0)
    n_steps = pl.num_programs(0)
    n_rows = n_seq * n_tok
    q_rows = N_HEADS * n_tok
    keys = 2 * WINDOW
    step_rows = SEQ_PER_STEP * q_rows

    def by_token(val):
        for c in range(val.shape[1] // LANES):
            perm_s[c] = val[:, c * LANES:(c + 1) * LANES]
        return jnp.concatenate(
            [jnp.concatenate([perm_s[c, pl.ds(i, n_seq, stride=n_tok), :] for i in range(n_tok)], axis=0)
             for c in range(val.shape[1] // LANES)], axis=1)

    def by_sequence(val):
        for c in range(val.shape[1] // LANES):
            for i in range(n_tok):
                perm_s[c, pl.ds(i, n_seq, stride=n_tok), :] = val[i * n_seq:(i + 1) * n_seq, c * LANES:(c + 1) * LANES]
        return jnp.concatenate([perm_s[c] for c in range(val.shape[1] // LANES)], axis=1)

    @pl.when(step == 0)
    def _project():
        h = _rms(x_ref[...], mnorm_ref[...])
        pkv = _dot(h.astype(BF16), win_ref[:, ATTN_WIDTH:ATTN_WIDTH + 2 * KV_WIDTH])
        kvt_s[...] = pkv.T
        h_in = by_token(h).astype(BF16)
        pq = _dot(h_in, win_ref[:, 0:ATTN_WIDTH]) * (ATTN_SCALE * LOG2E)
        half = lax.broadcasted_iota(jnp.int32, (n_rows, LANES), 1) >> 6
        for hg in range(N_HEADS):
            kvh = hg // GQA_GROUP
            slab = pq[:, (hg // 2) * LANES:(hg // 2 + 1) * LANES]
            if (hg % 2) != kvh:
                slab = pltpu.roll(slab, HEAD_DIM, 1)
            qall_s[hg * n_rows:(hg + 1) * n_rows, :] = jnp.where(half == kvh, slab, 0.0)

        u = jax.nn.gelu(_dot(h_in, win_ref[:, 768:1280]))
        gact = jax.nn.gelu(_dot(h_in, win_ref[:, 1280:1792]))
        gns = []
        for gi in range(N_GATE_GROUPS):
            lanes = slice(gi * GROUP_CH, (gi + 1) * GROUP_CH)
            gn = _gate_norm(gact, vnorm_ref, gi)
            gns.append(gn)
            for i in range(n_tok):
                rows = slice(i * n_seq, (i + 1) * n_seq)
                mixed = jnp.full((n_seq, GROUP_CH), bsm_ref[gi * n_tok + i], F32)
                for j in range(i + 1):
                    mixed = mixed + wsm_ref[(gi * n_tok + i) * n_tok + j] * gn[j * n_seq:(j + 1) * n_seq, :]
                gate_s[rows, lanes] = u[rows, lanes] * mixed
        gn_ref[...] = by_sequence(jnp.concatenate(gns, axis=1))

        row = lax.broadcasted_iota(jnp.int32, (step_rows, keys), 0)
        col = lax.broadcasted_iota(jnp.int32, (step_rows, keys), 1)
        tok = row & (n_tok - 1)
        head = (row >> 2) & (N_HEADS - 1)
        slope = jnp.zeros((step_rows, keys), F32)
        for hg in range(N_HEADS):
            slope = jnp.where(head == hg, ALIBI_SLOPES[hg], slope)
        shifted = col < WINDOW
        pos = jnp.where(shifted, col + n_tok, col - WINDOW)
        dist = WINDOW + tok - pos
        valid = (dist >= 0) & (dist < WINDOW) & (shifted | (pos < n_tok))
        bias_s[...] = jnp.where(valid, -(slope * dist.astype(F32)) * LOG2E, MASK_VALUE)
        head1 = (lax.broadcasted_iota(jnp.int32, (step_rows, 1), 0) >> 2) & (N_HEADS - 1)
        sink = jnp.zeros((step_rows, 1), F32)
        for hg in range(N_HEADS):
            sink = jnp.where(head1 == hg, sinks_ref[hg] * LOG2E, sink)
        sink_s[...] = sink

    col0 = pl.multiple_of(step * LANES, LANES)
    k_cols = kvt_s[0:KV_WIDTH, pl.ds(col0, LANES)]
    v_cols = kvt_s[KV_WIDTH:2 * KV_WIDTH, pl.ds(col0, LANES)]
    is_new = lax.broadcasted_iota(jnp.int32, (KV_WIDTH, WINDOW), 1) >= WINDOW - n_tok
    for bl in range(SEQ_PER_STEP):
        to_tail = (WINDOW - n_tok - n_tok * bl) % LANES
        sk_ref[bl] = jnp.where(is_new, pltpu.roll(k_cols, to_tail, 1), pltpu.roll(ck_ref[bl], WINDOW - n_tok, 1))
        sv_ref[bl] = jnp.where(is_new, pltpu.roll(v_cols, to_tail, 1), pltpu.roll(cv_ref[bl], WINDOW - n_tok, 1))

    seq0 = step * SEQ_PER_STEP
    for bl in range(SEQ_PER_STEP):
        qb = qall_s[pl.ds(seq0 + bl, q_rows, stride=n_seq), :].astype(BF16)
        k2 = jnp.concatenate([sk_ref[bl], ck_ref[bl]], axis=1).astype(BF16)
        s_s[bl * q_rows:(bl + 1) * q_rows, :] = _dot(qb, k2)
    s = s_s[...] + bias_s[...]
    sink = sink_s[...]
    m = jnp.maximum(jnp.max(s, axis=1, keepdims=True), sink)
    p = jnp.exp2(s - m)
    inv = 1.0 / (jnp.sum(p, axis=1, keepdims=True) + jnp.exp2(sink - m))
    p_s[...] = p.astype(BF16)
    for bl in range(SEQ_PER_STEP):
        rows = slice(bl * q_rows, (bl + 1) * q_rows)
        v2 = jnp.concatenate([sv_ref[bl], cv_ref[bl]], axis=1).astype(BF16)
        oall_s[pl.ds(seq0 + bl, q_rows, stride=n_seq), :] = _dot_nt(p_s[rows, :], v2) * inv[rows, :]

    @pl.when(step == n_steps - 1)
    def _merge_rows():
        left = lax.broadcasted_iota(jnp.int32, (n_rows, LANES), 1) < HEAD_DIM
        slabs = []
        for pair in range(N_HEADS // 2):
            halves = []
            for hg in (2 * pair, 2 * pair + 1):
                o = oall_s[hg * n_rows:(hg + 1) * n_rows, :]
                if (hg % 2) != (hg // GQA_GROUP):
                    o = pltpu.roll(o, HEAD_DIM, 1)
                halves.append(o)
            slabs.append(jnp.where(left, halves[0], halves[1]))
        attn = jnp.concatenate(slabs, axis=1)
        o_ref[...] = by_sequence(_merge(by_token(x_ref[...]), attn, gate_s[...], anorm_ref, gnorm_ref, wout_ref))


def _mixer_sample(x, mnorm, win, sinks, vnorm, wsm, bsm, anorm, gnorm, wout, ck, cv, *, n_seq, n_tok):
    n_rows = n_seq * n_tok
    q_rows = N_HEADS * n_tok
    assert x.shape[0] == n_rows and n_seq % SEQ_PER_STEP == 0 and n_tok == 4 and n_seq == LANES
    assert SEQ_PER_STEP * n_tok == LANES
    cache_spec = pl.BlockSpec((SEQ_PER_STEP, KV_WIDTH, WINDOW), lambda s: (s, 0, 0))
    cache_shape = jax.ShapeDtypeStruct(ck.shape, F32)
    rows_spec = pl.BlockSpec((n_rows, D_MODEL), lambda s: (0, 0))
    gn_spec = pl.BlockSpec((n_rows, GMLP_WIDTH), lambda s: (0, 0))
    return pl.pallas_call(
        functools.partial(_mixer_sample_body, n_seq=n_seq, n_tok=n_tok),
        grid=(n_seq // SEQ_PER_STEP,),
        in_specs=[rows_spec, _resident((1, D_MODEL)), _resident(win.shape), _smem(), _resident(vnorm.shape),
                  _smem(), _smem(), _resident(anorm.shape), _resident(gnorm.shape), _resident(wout.shape),
                  cache_spec, cache_spec],
        out_specs=[rows_spec, gn_spec, cache_spec, cache_spec],
        out_shape=[jax.ShapeDtypeStruct(x.shape, F32), jax.ShapeDtypeStruct((n_rows, GMLP_WIDTH), F32),
                   cache_shape, cache_shape],
        scratch_shapes=[
            pltpu.VMEM((D_MODEL // LANES, n_rows, LANES), F32),
            pltpu.VMEM((N_HEADS * n_rows, LANES), F32),
            pltpu.VMEM((N_HEADS * n_rows, LANES), F32),
            pltpu.VMEM((2 * KV_WIDTH, n_rows), F32),
            pltpu.VMEM((n_rows, GMLP_WIDTH), F32),
            pltpu.VMEM((SEQ_PER_STEP * q_rows, 2 * WINDOW), F32),
            pltpu.VMEM((SEQ_PER_STEP * q_rows, 1), F32),
            pltpu.VMEM((SEQ_PER_STEP * q_rows, 2 * WINDOW), F32),
            pltpu.VMEM((SEQ_PER_STEP * q_rows, 2 * WINDOW), BF16),
        ],
        compiler_params=pltpu.CompilerParams(dimension_semantics=("arbitrary",), vmem_limit_bytes=VMEM_LIMIT),
        name="mixer_sample",
    )(x, mnorm, win, sinks, vnorm, wsm, bsm, anorm, gnorm, wout, ck, cv)


def kernel(x_prompt, x_sample, cache_k, cache_v, ffn1_norm, ffn1_w_gate, ffn1_w_up, ffn1_w_down, mix_norm, w_in,
           attn_sinks, gmlp_v_norm, gmlp_w_spatial, gmlp_b_spatial, attn_out_norm, gmlp_out_norm, w_out, ffn2_norm,
           ffn2_w_gate, ffn2_w_up, ffn2_w_down, final_norm):
    batch, seq, _ = x_prompt.shape
    n_seq, n_tok, _ = x_sample.shape
    depth = cache_k.shape[0]
    assert depth == 1

    xp = x_prompt.reshape(batch * seq, D_MODEL)
    xs = x_sample.reshape(n_seq * n_tok, D_MODEL)
    row = lambda v: v.reshape(1, -1)
    fnorm = row(final_norm)
    l = 0
    sinks = attn_sinks[l]
    vnorm = gmlp_v_norm[l]
    ws = gmlp_w_spatial[l]
    bs = gmlp_b_spatial[l]
    mnorm, anorm, gnorm = row(mix_norm[l]), row(attn_out_norm[l]), row(gmlp_out_norm[l])

    later = (ffn2_w_gate[l], ffn2_w_up[l], ffn2_w_down[l], w_in[l], w_out[l])
    xp, xs, wg2, wu2, wd2, win, wout = _ffn_half(xp, xs, row(ffn1_norm[l]), ffn1_w_gate[l], ffn1_w_up[l],
                                                 ffn1_w_down[l], fnorm, final_norm=False, subtiles=1, to_cast=later)
    xp, pk, pv = _mixer_prompt(xp, mnorm, win, sinks, vnorm, ws, bs.T, anorm, gnorm, wout, batch=batch)
    to_cols = lambda c: jnp.transpose(c, (0, 2, 3, 1)).reshape(n_seq, KV_WIDTH, WINDOW)
    to_rows = lambda c: jnp.transpose(c.reshape(n_seq, N_KV_HEADS, HEAD_DIM, WINDOW), (0, 3, 1, 2))[None]
    xs, gn, sk, sv = _mixer_sample(xs, mnorm, win, sinks, vnorm, ws[:, :n_tok, :n_tok].reshape(-1),
                                   bs[:, :n_tok].reshape(-1), anorm, gnorm, wout, to_cols(cache_k[l]),
                                   to_cols(cache_v[l]), n_seq=n_seq, n_tok=n_tok)
    yp, ys = _ffn_half(xp, xs, row(ffn2_norm[l]), wg2, wu2, wd2, fnorm, final_norm=True, subtiles=1, tile_rows=1024)

    kv5 = lambda a, n: a.reshape(1, n, WINDOW, N_KV_HEADS, HEAD_DIM)
    y_prompt = yp.reshape(batch, seq, D_MODEL)
    y_sample = ys.reshape(n_seq, n_tok, D_MODEL)
    chunk_v = gn.reshape(1, n_seq, n_tok, GMLP_WIDTH)
    return (y_prompt, y_sample, kv5(pk, batch), kv5(pv, batch), to_rows(sk), to_rows(sv), chunk_v)
```

```python
import functools

import jax
import jax.numpy as jnp
from jax import lax
from jax.experimental import pallas as pl
from jax.experimental.pallas import tpu as pltpu

F32 = jnp.float32
BF16 = jnp.bfloat16

D_MODEL = 1024
D_FF = 2816
N_HEADS = 8
N_KV_HEADS = 2
GQA_GROUP = 4
HEAD_DIM = 64
ATTN_WIDTH = 512
KV_WIDTH = 128
GMLP_WIDTH = 512
N_GATE_GROUPS = 4
GROUP_CH = 128
WINDOW = 128
CHUNK = 128
RMS_EPS = 1e-6
FFN_RESIDUAL = 0.5
ATTN_SCALE = 0.125
LOG2E = 1.4426950408889634
ALIBI_SLOPES = tuple(2.0 ** (-(i + 1)) for i in range(N_HEADS))
MASK_VALUE = -1e30

LANES = 128
BF16_SUBLANES = 16
FFN_ROWS = 512
FFN_LOAD_STEPS = 16
FFN_CHUNK = 256
MIX_ROWS = 1024
SEQ_PER_STEP = 32
VMEM_LIMIT = 56 * 1024 * 1024


def _rms(x, g):
    ms = jnp.mean(x * x, axis=-1, keepdims=True)
    return x * lax.rsqrt(ms + RMS_EPS) * g


def _dot(a, b):
    return jnp.dot(a, b, preferred_element_type=F32)


def _dot_nt(a, b):
    return lax.dot_general(a, b, (((1,), (1,)), ((), ())), preferred_element_type=F32)


def _resident(shape):
    nd = len(shape)
    return pl.BlockSpec(shape, lambda *_: (0,) * nd, pipeline_mode=pl.Buffered(1))


def _smem():
    return pl.BlockSpec(memory_space=pltpu.SMEM)


def _ffn_body(*refs, final_norm, n_cast, subtiles, n_load_steps):
    xp_ref, xs_ref, norm_ref, wg_in, wu_in, wd_in, fnorm_ref = refs[:7]
    cast_in = refs[7:7 + n_cast]
    op_ref, os_ref = refs[7 + n_cast:9 + n_cast]
    cast_out = refs[9 + n_cast:9 + 2 * n_cast]
    act_ref = refs[9 + 2 * n_cast]
    i = pl.program_id(0)
    first = n_load_steps

    if n_load_steps:
        wg_ref, wu_ref, wd_ref = refs[10 + 2 * n_cast:]

        @pl.when(i < n_load_steps)
        def _load_weights():
            for src, dst in ((wg_in, wg_ref), (wu_in, wu_ref), (wd_in, wd_ref)):
                br = src.shape[0]
                dst[pl.ds(pl.multiple_of(i * br, br), br), :] = src[...].astype(BF16)
    else:
        wg_ref, wu_ref, wd_ref = wg_in, wu_in, wd_in

    for src, dst in zip(cast_in, cast_out):
        dst[...] = src[...].astype(BF16)

    def _sub_tile(k, carry):
        rows = pl.ds(pl.multiple_of(k * FFN_ROWS, FFN_ROWS), FFN_ROWS)
        x = jnp.where(i == first, xs_ref[...], xp_ref[rows, :])
        h = _rms(x, norm_ref[...]).astype(BF16)
        for c in range(D_FF // FFN_CHUNK):
            cols = slice(c * FFN_CHUNK, (c + 1) * FFN_CHUNK)
            a = _dot(h, wg_ref[:, cols])
            b = _dot(h, wu_ref[:, cols])
            act_ref[:, cols] = (a * jax.nn.sigmoid(a) * b).astype(BF16)
        y = x + FFN_RESIDUAL * _dot(act_ref[...], wd_ref[...])
        if final_norm:
            y = _rms(y, fnorm_ref[...])
        op_ref[rows, :] = y
        return carry

    trips = jnp.where(i < first, 0, jnp.where(i == first, 1, subtiles))
    lax.fori_loop(0, trips, _sub_tile, 0)

    @pl.when(i == first)
    def _():
        os_ref[...] = op_ref[0:FFN_ROWS, :]


def _block_rows(rows, n_steps):
    br = BF16_SUBLANES
    while rows % br or rows // br > n_steps:
        br += BF16_SUBLANES
    return br


def _ffn_half(xp, xs, norm, wg, wu, wd, fnorm, *, final_norm, subtiles, to_cast=()):
    block_rows = subtiles * FFN_ROWS
    n_prompt_tiles = xp.shape[0] // block_rows
    assert xp.shape[0] == n_prompt_tiles * block_rows and xs.shape[0] == FFN_ROWS
    n_load_steps = FFN_LOAD_STEPS if wg.dtype == F32 else 0
    first = n_load_steps
    prompt_spec = pl.BlockSpec((block_rows, D_MODEL), lambda i: (jnp.maximum(i - first - 1, 0), 0))
    sample_spec = pl.BlockSpec((FFN_ROWS, D_MODEL), lambda i: (0, 0))

    def streamed(w, n_steps, start):
        br = _block_rows(w.shape[0], n_steps)
        return pl.BlockSpec((br, w.shape[1]), functools.partial(
            lambda i, last: (jnp.clip(i - start, 0, last), 0), last=w.shape[0] // br - 1))

    cast_specs = [streamed(w, n_prompt_tiles, first) for w in to_cast]
    cast_shapes = [jax.ShapeDtypeStruct(w.shape, BF16) for w in to_cast]
    scratch = [pltpu.VMEM((FFN_ROWS, D_FF), BF16)]
    if n_load_steps:
        weight_specs = [streamed(w, n_load_steps, 0) for w in (wg, wu, wd)]
        scratch += [pltpu.VMEM(w.shape, BF16) for w in (wg, wu, wd)]
    else:
        weight_specs = [_resident(w.shape) for w in (wg, wu, wd)]
    return pl.pallas_call(
        functools.partial(_ffn_body, final_norm=final_norm, n_cast=len(to_cast), subtiles=subtiles,
                          n_load_steps=n_load_steps),
        grid=(n_load_steps + 1 + n_prompt_tiles,),
        in_specs=[prompt_spec, sample_spec, _resident((1, D_MODEL))] + weight_specs + [_resident((1, D_MODEL))]
        + cast_specs,
        out_specs=[prompt_spec, sample_spec] + cast_specs,
        out_shape=[jax.ShapeDtypeStruct(xp.shape, F32), jax.ShapeDtypeStruct(xs.shape, F32)] + cast_shapes,
        scratch_shapes=scratch,
        compiler_params=pltpu.CompilerParams(dimension_semantics=("arbitrary",), vmem_limit_bytes=VMEM_LIMIT),
        name="ffn_half_final" if final_norm else "ffn_half",
    )(xp, xs, norm, wg, wu, wd, fnorm, *to_cast)


def _pair_heads(t):
    swapped = pltpu.roll(t, HEAD_DIM, 1)
    left = lax.broadcasted_iota(jnp.int32, t.shape, 1) < HEAD_DIM
    return jnp.where(left, t, swapped), jnp.where(left, swapped, t)


def _gate_norm(g, vnorm_ref, gi):
    gg = g[:, gi * GROUP_CH:(gi + 1) * GROUP_CH]
    return _rms(gg, vnorm_ref[gi:gi + 1, :])


def _merge(x, attn, gate, anorm_ref, gnorm_ref, wout_ref):
    cat = jnp.concatenate([_rms(attn, anorm_ref[...]), _rms(gate, gnorm_ref[...])], axis=1)
    return x + _dot(cat.astype(BF16), wout_ref[...])


def _mixer_prompt_body(x_ref, mnorm_ref, win_ref, sinks_ref, vnorm_ref, ws_ref, bst_ref, anorm_ref, gnorm_ref,
                       wout_ref, o_ref, pk_ref, pv_ref, kv_s, q_s, attn_s, gate_s, bias_s, wtril_s):
    b = pl.program_id(0)
    t = pl.program_id(1)
    nblk = MIX_ROWS // WINDOW
    band = 2 * WINDOW
    qrows = GQA_GROUP * WINDOW

    @pl.when((b == 0) & (t == 0))
    def _init_tables():
        row = lax.broadcasted_iota(jnp.int32, (qrows, band), 0)
        key = lax.broadcasted_iota(jnp.int32, (qrows, band), 1)
        grp = row >> 7
        dist = WINDOW + (row & (WINDOW - 1)) - key
        valid = (dist >= 0) & (dist < WINDOW)
        distf = dist.astype(F32)
        for h in range(N_KV_HEADS):
            s = [ALIBI_SLOPES[h * GQA_GROUP + g] for g in range(GQA_GROUP)]
            slope = jnp.where(grp == 0, s[0], jnp.where(grp == 1, s[1], jnp.where(grp == 2, s[2], s[3])))
            bias = jnp.where(valid, -(slope * distf) * LOG2E, MASK_VALUE)
            bias_s[0, h] = bias
            bias_s[1, h] = jnp.where(key < WINDOW, MASK_VALUE, bias)
        r = lax.broadcasted_iota(jnp.int32, (CHUNK, CHUNK), 0)
        c = lax.broadcasted_iota(jnp.int32, (CHUNK, CHUNK), 1)
        for gi in range(N_GATE_GROUPS):
            wtril_s[gi] = jnp.where(r >= c, ws_ref[gi], 0.0).astype(BF16)

    @pl.when(t == 0)
    def _no_previous_block():
        kv_s[0:WINDOW, :] = jnp.zeros((WINDOW, 4 * KV_WIDTH), BF16)

    x = x_ref[...]
    h_in = _rms(x, mnorm_ref[...]).astype(BF16)
    q_s[...] = _dot(h_in, win_ref[:, 0:ATTN_WIDTH]) * (ATTN_SCALE * LOG2E)
    wkv = win_ref[:, ATTN_WIDTH:ATTN_WIDTH + 2 * KV_WIDTH]
    pkv = jnp.concatenate([_dot(h_in[0:MIX_ROWS // 2, :], wkv), _dot(h_in[MIX_ROWS // 2:, :], wkv)], axis=0)
    k_new = pkv[:, 0:KV_WIDTH]
    v_new = pkv[:, KV_WIDTH:2 * KV_WIDTH]
    pk_ref[0] = k_new[MIX_ROWS - WINDOW:, :]
    pv_ref[0] = v_new[MIX_ROWS - WINDOW:, :]
    k0, k1 = _pair_heads(k_new)
    v0, v1 = _pair_heads(v_new)
    kv_s[WINDOW:, :] = jnp.concatenate([k0, k1, v0, v1], axis=1).astype(BF16)

    table = jnp.where(t == 0, 1, 0)
    lane_grp = lax.broadcasted_iota(jnp.int32, (WINDOW, 2 * KV_WIDTH), 1) >> 6
    for j in range(nblk):
        rows = slice(j * WINDOW, (j + 1) * WINDOW)
        kvb = kv_s[j * WINDOW:j * WINDOW + band, :]
        for h in range(N_KV_HEADS):
            ka = kvb[:, h * KV_WIDTH:(h + 1) * KV_WIDTH]
            va = kvb[:, (2 + h) * KV_WIDTH:(3 + h) * KV_WIDTH]
            k4 = jnp.concatenate([ka, ka], axis=1)
            v4 = jnp.concatenate([va, va], axis=1)
            q = q_s[rows, h * 2 * KV_WIDTH:(h + 1) * 2 * KV_WIDTH]
            qstack = jnp.concatenate([jnp.where(lane_grp == g, q, 0.0) for g in range(GQA_GROUP)],
                                     axis=0).astype(BF16)
            bias = bias_s[table, h] if j == 0 else bias_s[0, h]
            s = _dot_nt(qstack, k4) + bias
            probs, inv = [], []
            for g in range(GQA_GROUP):
                sg = s[g * WINDOW:(g + 1) * WINDOW, :]
                sink = sinks_ref[h * GQA_GROUP + g] * LOG2E
                m = jnp.maximum(jnp.max(sg, axis=1, keepdims=True), sink)
                p = jnp.exp2(sg - m)
                inv.append(1.0 / (jnp.sum(p, axis=1, keepdims=True) + jnp.exp2(sink - m)))
                probs.append(p.astype(BF16))
            o = _dot(jnp.concatenate(probs, axis=0), v4)
            for g in range(GQA_GROUP):
                lanes = slice(g * HEAD_DIM, (g + 1) * HEAD_DIM)
                out_lanes = slice((h * GQA_GROUP + g) * HEAD_DIM, (h * GQA_GROUP + g + 1) * HEAD_DIM)
                attn_s[rows, out_lanes] = o[g * WINDOW:(g + 1) * WINDOW, lanes] * inv[g]
    kv_s[0:WINDOW, :] = kv_s[MIX_ROWS:MIX_ROWS + WINDOW, :]

    u = jax.nn.gelu(_dot(h_in, win_ref[:, 768:1280]))
    gact = jax.nn.gelu(_dot(h_in, win_ref[:, 1280:1792]))
    for gi in range(N_GATE_GROUPS):
        lanes = slice(gi * GROUP_CH, (gi + 1) * GROUP_CH)
        gn = _gate_norm(gact, vnorm_ref, gi).astype(BF16)
        chunks = jnp.concatenate([gn[c * CHUNK:(c + 1) * CHUNK, :] for c in range(nblk)], axis=1)
        mixed = _dot(wtril_s[gi], chunks) + bst_ref[:, gi:gi + 1]
        for c in range(nblk):
            rows = slice(c * CHUNK, (c + 1) * CHUNK)
            gate_s[rows, lanes] = u[rows, lanes] * mixed[:, c * GROUP_CH:(c + 1) * GROUP_CH]

    o_ref[...] = _merge(x, attn_s[...], gate_s[...], anorm_ref, gnorm_ref, wout_ref)


def _mixer_prompt(x, mnorm, win, sinks, vnorm, ws, bst, anorm, gnorm, wout, *, batch):
    rows = x.shape[0]
    tiles = rows // (batch * MIX_ROWS)
    assert rows == batch * tiles * MIX_ROWS
    band = 2 * WINDOW
    row_spec = pl.BlockSpec((MIX_ROWS, D_MODEL), lambda b, t: (b * tiles + t, 0))
    kv_spec = pl.BlockSpec((1, WINDOW, KV_WIDTH), lambda b, t: (b, 0, 0))
    kv_shape = jax.ShapeDtypeStruct((batch, WINDOW, KV_WIDTH), F32)
    return pl.pallas_call(
        _mixer_prompt_body,
        grid=(batch, tiles),
        in_specs=[row_spec, _resident((1, D_MODEL)), _resident(win.shape), _smem(), _resident(vnorm.shape),
                  _resident(ws.shape), _resident(bst.shape), _resident(anorm.shape), _resident(gnorm.shape),
                  _resident(wout.shape)],
        out_specs=[row_spec, kv_spec, kv_spec],
        out_shape=[jax.ShapeDtypeStruct(x.shape, F32), kv_shape, kv_shape],
        scratch_shapes=[
            pltpu.VMEM((WINDOW + MIX_ROWS, 4 * KV_WIDTH), BF16),
            pltpu.VMEM((MIX_ROWS, ATTN_WIDTH), F32),
            pltpu.VMEM((MIX_ROWS, ATTN_WIDTH), F32),
            pltpu.VMEM((MIX_ROWS, GMLP_WIDTH), F32),
            pltpu.VMEM((2, N_KV_HEADS, GQA_GROUP * WINDOW, band), F32),
            pltpu.VMEM((N_GATE_GROUPS, CHUNK, CHUNK), BF16),
        ],
        compiler_params=pltpu.CompilerParams(dimension_semantics=("arbitrary", "arbitrary"),
                                             vmem_limit_bytes=VMEM_LIMIT),
        name="mixer_prompt",
    )(x, mnorm, win, sinks, vnorm, ws, bst, anorm, gnorm, wout)


def _mixer_sample_body(x_ref, mnorm_ref, win_ref, sinks_ref, vnorm_ref, wsm_ref, bsm_ref, anorm_ref, gnorm_ref,
                       wout_ref, ck_ref, cv_ref, o_ref, gn_ref, sk_ref, sv_ref,
                       perm_s, qall_s, oall_s, kvt_s, gate_s, bias_s, sink_s, s_s, p_s, *, n_seq, n_tok):
    step = pl.program_id(0)
    n_steps = pl.num_programs(0)
    n_rows = n_seq * n_tok
    q_rows = N_HEADS * n_tok
    keys = 2 * WINDOW
    step_rows = SEQ_PER_STEP * q_rows

    def by_token(val):
        for c in range(val.shape[1] // LANES):
            perm_s[c] = val[:, c * LANES:(c + 1) * LANES]
        return jnp.concatenate(
            [jnp.concatenate([perm_s[c, pl.ds(i, n_seq, stride=n_tok), :] for i in range(n_tok)], axis=0)
             for c in range(val.shape[1] // LANES)], axis=1)

    def by_sequence(val):
        for c in range(val.shape[1] // LANES):
            for i in range(n_tok):
                perm_s[c, pl.ds(i, n_seq, stride=n_tok), :] = val[i * n_seq:(i + 1) * n_seq, c * LANES:(c + 1) * LANES]
        return jnp.concatenate([perm_s[c] for c in range(val.shape[1] // LANES)], axis=1)

    @pl.when(step == 0)
    def _project():
        h = _rms(x_ref[...], mnorm_ref[...])
        pkv = _dot(h.astype(BF16), win_ref[:, ATTN_WIDTH:ATTN_WIDTH + 2 * KV_WIDTH])
        kvt_s[...] = pkv.T
        h_in = by_token(h).astype(BF16)
        pq = _dot(h_in, win_ref[:, 0:ATTN_WIDTH]) * (ATTN_SCALE * LOG2E)
        half = lax.broadcasted_iota(jnp.int32, (n_rows, LANES), 1) >> 6
        for hg in range(N_HEADS):
            kvh = hg // GQA_GROUP
            slab = pq[:, (hg // 2) * LANES:(hg // 2 + 1) * LANES]
            if (hg % 2) != kvh:
                slab = pltpu.roll(slab, HEAD_DIM, 1)
            qall_s[hg * n_rows:(hg + 1) * n_rows, :] = jnp.where(half == kvh, slab, 0.0)

        u = jax.nn.gelu(_dot(h_in, win_ref[:, 768:1280]))
        gact = jax.nn.gelu(_dot(h_in, win_ref[:, 1280:1792]))
        gns = []
        for gi in range(N_GATE_GROUPS):
            lanes = slice(gi * GROUP_CH, (gi + 1) * GROUP_CH)
            gn = _gate_norm(gact, vnorm_ref, gi)
            gns.append(gn)
            for i in range(n_tok):
                rows = slice(i * n_seq, (i + 1) * n_seq)
                mixed = jnp.full((n_seq, GROUP_CH), bsm_ref[gi * n_tok + i], F32)
                for j in range(i + 1):
                    mixed = mixed + wsm_ref[(gi * n_tok + i) * n_tok + j] * gn[j * n_seq:(j + 1) * n_seq, :]
                gate_s[rows, lanes] = u[rows, lanes] * mixed
        gn_ref[...] = by_sequence(jnp.concatenate(gns, axis=1))

        row = lax.broadcasted_iota(jnp.int32, (step_rows, keys), 0)
        col = lax.broadcasted_iota(jnp.int32, (step_rows, keys), 1)
        tok = row & (n_tok - 1)
        head = (row >> 2) & (N_HEADS - 1)
        slope = jnp.zeros((step_rows, keys), F32)
        for hg in range(N_HEADS):
            slope = jnp.where(head == hg, ALIBI_SLOPES[hg], slope)
        shifted = col < WINDOW
        pos = jnp.where(shifted, col + n_tok, col - WINDOW)
        dist = WINDOW + tok - pos
        valid = (dist >= 0) & (dist < WINDOW) & (shifted | (pos < n_tok))
        bias_s[...] = jnp.where(valid, -(slope * dist.astype(F32)) * LOG2E, MASK_VALUE)
        head1 = (lax.broadcasted_iota(jnp.int32, (step_rows, 1), 0) >> 2) & (N_HEADS - 1)
        sink = jnp.zeros((step_rows, 1), F32)
        for hg in range(N_HEADS):
            sink = jnp.where(head1 == hg, sinks_ref[hg] * LOG2E, sink)
        sink_s[...] = sink

    col0 = pl.multiple_of(step * LANES, LANES)
    k_cols = kvt_s[0:KV_WIDTH, pl.ds(col0, LANES)]
    v_cols = kvt_s[KV_WIDTH:2 * KV_WIDTH, pl.ds(col0, LANES)]
    is_new = lax.broadcasted_iota(jnp.int32, (KV_WIDTH, WINDOW), 1) >= WINDOW - n_tok
    for bl in range(SEQ_PER_STEP):
        to_tail = (WINDOW - n_tok - n_tok * bl) % LANES
        sk_ref[bl] = jnp.where(is_new, pltpu.roll(k_cols, to_tail, 1), pltpu.roll(ck_ref[bl], WINDOW - n_tok, 1))
        sv_ref[bl] = jnp.where(is_new, pltpu.roll(v_cols, to_tail, 1), pltpu.roll(cv_ref[bl], WINDOW - n_tok, 1))

    seq0 = step * SEQ_PER_STEP
    for bl in range(SEQ_PER_STEP):
        qb = qall_s[pl.ds(seq0 + bl, q_rows, stride=n_seq), :].astype(BF16)
        k2 = jnp.concatenate([sk_ref[bl], ck_ref[bl]], axis=1).astype(BF16)
        s_s[bl * q_rows:(bl + 1) * q_rows, :] = _dot(qb, k2)
    s = s_s[...] + bias_s[...]
    sink = sink_s[...]
    m = jnp.maximum(jnp.max(s, axis=1, keepdims=True), sink)
    p = jnp.exp2(s - m)
    inv = 1.0 / (jnp.sum(p, axis=1, keepdims=True) + jnp.exp2(sink - m))
    p_s[...] = p.astype(BF16)
    for bl in range(SEQ_PER_STEP):
        rows = slice(bl * q_rows, (bl + 1) * q_rows)
        v2 = jnp.concatenate([sv_ref[bl], cv_ref[bl]], axis=1).astype(BF16)
        oall_s[pl.ds(seq0 + bl, q_rows, stride=n_seq), :] = _dot_nt(p_s[rows, :], v2) * inv[rows, :]

    @pl.when(step == n_steps - 1)
    def _merge_rows():
        left = lax.broadcasted_iota(jnp.int32, (n_rows, LANES), 1) < HEAD_DIM
        slabs = []
        for pair in range(N_HEADS // 2):
            halves = []
            for hg in (2 * pair, 2 * pair + 1):
                o = oall_s[hg * n_rows:(hg + 1) * n_rows, :]
                if (hg % 2) != (hg // GQA_GROUP):
                    o = pltpu.roll(o, HEAD_DIM, 1)
                halves.append(o)
            slabs.append(jnp.where(left, halves[0], halves[1]))
        attn = jnp.concatenate(slabs, axis=1)
        o_ref[...] = by_sequence(_merge(by_token(x_ref[...]), attn, gate_s[...], anorm_ref, gnorm_ref, wout_ref))


def _mixer_sample(x, mnorm, win, sinks, vnorm, wsm, bsm, anorm, gnorm, wout, ck, cv, *, n_seq, n_tok):
    n_rows = n_seq * n_tok
    q_rows = N_HEADS * n_tok
    assert x.shape[0] == n_rows and n_seq % SEQ_PER_STEP == 0 and n_tok == 4 and n_seq == LANES
    assert SEQ_PER_STEP * n_tok == LANES
    cache_spec = pl.BlockSpec((SEQ_PER_STEP, KV_WIDTH, WINDOW), lambda s: (s, 0, 0))
    cache_shape = jax.ShapeDtypeStruct(ck.shape, F32)
    rows_spec = pl.BlockSpec((n_rows, D_MODEL), lambda s: (0, 0))
    gn_spec = pl.BlockSpec((n_rows, GMLP_WIDTH), lambda s: (0, 0))
    return pl.pallas_call(
        functools.partial(_mixer_sample_body, n_seq=n_seq, n_tok=n_tok),
        grid=(n_seq // SEQ_PER_STEP,),
        in_specs=[rows_spec, _resident((1, D_MODEL)), _resident(win.shape), _smem(), _resident(vnorm.shape),
                  _smem(), _smem(), _resident(anorm.shape), _resident(gnorm.shape), _resident(wout.shape),
                  cache_spec, cache_spec],
        out_specs=[rows_spec, gn_spec, cache_spec, cache_spec],
        out_shape=[jax.ShapeDtypeStruct(x.shape, F32), jax.ShapeDtypeStruct((n_rows, GMLP_WIDTH), F32),
                   cache_shape, cache_shape],
        scratch_shapes=[
            pltpu.VMEM((D_MODEL // LANES, n_rows, LANES), F32),
            pltpu.VMEM((N_HEADS * n_rows, LANES), F32),
            pltpu.VMEM((N_HEADS * n_rows, LANES), F32),
            pltpu.VMEM((2 * KV_WIDTH, n_rows), F32),
            pltpu.VMEM((n_rows, GMLP_WIDTH), F32),
            pltpu.VMEM((SEQ_PER_STEP * q_rows, 2 * WINDOW), F32),
            pltpu.VMEM((SEQ_PER_STEP * q_rows, 1), F32),
            pltpu.VMEM((SEQ_PER_STEP * q_rows, 2 * WINDOW), F32),
            pltpu.VMEM((SEQ_PER_STEP * q_rows, 2 * WINDOW), BF16),
        ],
        compiler_params=pltpu.CompilerParams(dimension_semantics=("arbitrary",), vmem_limit_bytes=VMEM_LIMIT),
        name="mixer_sample",
    )(x, mnorm, win, sinks, vnorm, wsm, bsm, anorm, gnorm, wout, ck, cv)


def kernel(x_prompt, x_sample, cache_k, cache_v, ffn1_norm, ffn1_w_gate, ffn1_w_up, ffn1_w_down, mix_norm, w_in,
           attn_sinks, gmlp_v_norm, gmlp_w_spatial, gmlp_b_spatial, attn_out_norm, gmlp_out_norm, w_out, ffn2_norm,
           ffn2_w_gate, ffn2_w_up, ffn2_w_down, final_norm):
    batch, seq, _ = x_prompt.shape
    n_seq, n_tok, _ = x_sample.shape
    depth = cache_k.shape[0]
    assert depth == 1

    xp = x_prompt.reshape(batch * seq, D_MODEL)
    xs = x_sample.reshape(n_seq * n_tok, D_MODEL)
    row = lambda v: v.reshape(1, -1)
    fnorm = row(final_norm)
    l = 0
    sinks = attn_sinks[l]
    vnorm = gmlp_v_norm[l]
    ws = gmlp_w_spatial[l]
    bs = gmlp_b_spatial[l]
    mnorm, anorm, gnorm = row(mix_norm[l]), row(attn_out_norm[l]), row(gmlp_out_norm[l])

    later = (ffn2_w_gate[l], ffn2_w_up[l], ffn2_w_down[l], w_in[l], w_out[l])
    xp, xs, wg2, wu2, wd2, win, wout = _ffn_half(xp, xs, row(ffn1_norm[l]), ffn1_w_gate[l], ffn1_w_up[l],
                                                 ffn1_w_down[l], fnorm, final_norm=False, subtiles=1, to_cast=later)
    xp, pk, pv = _mixer_prompt(xp, mnorm, win, sinks, vnorm, ws, bs.T, anorm, gnorm, wout, batch=batch)
    to_cols = lambda c: jnp.transpose(c, (0, 2, 3, 1)).reshape(n_seq, KV_WIDTH, WINDOW)
    to_rows = lambda c: jnp.transpose(c.reshape(n_seq, N_KV_HEADS, HEAD_DIM, WINDOW), (0, 3, 1, 2))[None]
    xs, gn, sk, sv = _mixer_sample(xs, mnorm, win, sinks, vnorm, ws[:, :n_tok, :n_tok].reshape(-1),
                                   bs[:, :n_tok].reshape(-1), anorm, gnorm, wout, to_cols(cache_k[l]),
                                   to_cols(cache_v[l]), n_seq=n_seq, n_tok=n_tok)
    yp, ys = _ffn_half(xp, xs, row(ffn2_norm[l]), wg2, wu2, wd2, fnorm, final_norm=True, subtiles=2)

    kv5 = lambda a, n: a.reshape(1, n, WINDOW, N_KV_HEADS, HEAD_DIM)
    y_prompt = yp.reshape(batch, seq, D_MODEL)
    y_sample = ys.reshape(n_seq, n_tok, D_MODEL)
    chunk_v = gn.reshape(1, n_seq, n_tok, GMLP_WIDTH)
    return (y_prompt, y_sample, kv5(pk, batch), kv5(pv, batch), to_rows(sk), to_rows(sv), chunk_v)
```

```python
import functools

import jax
import jax.numpy as jnp
from jax import lax
from jax.experimental import pallas as pl
from jax.experimental.pallas import tpu as pltpu

F32 = jnp.float32
BF16 = jnp.bfloat16

D_MODEL = 1024
D_FF = 2816
N_HEADS = 8
N_KV_HEADS = 2
GQA_GROUP = 4
HEAD_DIM = 64
ATTN_WIDTH = 512
KV_WIDTH = 128
GMLP_WIDTH = 512
N_GATE_GROUPS = 4
GROUP_CH = 128
WINDOW = 128
CHUNK = 128
RMS_EPS = 1e-6
FFN_RESIDUAL = 0.5
ATTN_SCALE = 0.125
LOG2E = 1.4426950408889634
ALIBI_SLOPES = tuple(2.0 ** (-(i + 1)) for i in range(N_HEADS))
MASK_VALUE = -1e30

LANES = 128
BF16_SUBLANES = 16
FFN_ROWS = 512
FFN_LOAD_STEPS = 8
FFN_CHUNK = 256
MIX_ROWS = 1024
SEQ_PER_STEP = 32
VMEM_LIMIT = 56 * 1024 * 1024


def _rms(x, g):
    ms = jnp.mean(x * x, axis=-1, keepdims=True)
    return x * lax.rsqrt(ms + RMS_EPS) * g


def _dot(a, b):
    return jnp.dot(a, b, preferred_element_type=F32)


def _dot_nt(a, b):
    return lax.dot_general(a, b, (((1,), (1,)), ((), ())), preferred_element_type=F32)


def _resident(shape):
    nd = len(shape)
    return pl.BlockSpec(shape, lambda *_: (0,) * nd, pipeline_mode=pl.Buffered(1))


def _smem():
    return pl.BlockSpec(memory_space=pltpu.SMEM)


def _ffn_body(*refs, final_norm, n_cast, subtiles, n_load_steps):
    xp_ref, xs_ref, norm_ref, wg_in, wu_in, wd_in, fnorm_ref = refs[:7]
    cast_in = refs[7:7 + n_cast]
    op_ref, os_ref = refs[7 + n_cast:9 + n_cast]
    cast_out = refs[9 + n_cast:9 + 2 * n_cast]
    act_ref = refs[9 + 2 * n_cast]
    i = pl.program_id(0)
    first = n_load_steps

    if n_load_steps:
        wg_ref, wu_ref, wd_ref = refs[10 + 2 * n_cast:]

        @pl.when(i < n_load_steps)
        def _load_weights():
            for src, dst in ((wg_in, wg_ref), (wu_in, wu_ref), (wd_in, wd_ref)):
                br = src.shape[0]
                dst[pl.ds(pl.multiple_of(i * br, br), br), :] = src[...].astype(BF16)
    else:
        wg_ref, wu_ref, wd_ref = wg_in, wu_in, wd_in

    for src, dst in zip(cast_in, cast_out):
        dst[...] = src[...].astype(BF16)

    def _sub_tile(k, carry):
        rows = pl.ds(pl.multiple_of(k * FFN_ROWS, FFN_ROWS), FFN_ROWS)
        x = jnp.where(i == first, xs_ref[...], xp_ref[rows, :])
        h = _rms(x, norm_ref[...]).astype(BF16)
        for c in range(D_FF // FFN_CHUNK):
            cols = slice(c * FFN_CHUNK, (c + 1) * FFN_CHUNK)
            a = _dot(h, wg_ref[:, cols])
            b = _dot(h, wu_ref[:, cols])
            act_ref[:, cols] = (a * jax.nn.sigmoid(a) * b).astype(BF16)
        y = x + FFN_RESIDUAL * _dot(act_ref[...], wd_ref[...])
        if final_norm:
            y = _rms(y, fnorm_ref[...])
        op_ref[rows, :] = y
        return carry

    trips = jnp.where(i < first, 0, jnp.where(i == first, 1, subtiles))
    lax.fori_loop(0, trips, _sub_tile, 0)

    @pl.when(i == first)
    def _():
        os_ref[...] = op_ref[0:FFN_ROWS, :]


def _block_rows(rows, n_steps):
    br = BF16_SUBLANES
    while rows % br or rows // br > n_steps:
        br += BF16_SUBLANES
    return br


def _ffn_half(xp, xs, norm, wg, wu, wd, fnorm, *, final_norm, subtiles, to_cast=()):
    block_rows = subtiles * FFN_ROWS
    n_prompt_tiles = xp.shape[0] // block_rows
    assert xp.shape[0] == n_prompt_tiles * block_rows and xs.shape[0] == FFN_ROWS
    n_load_steps = FFN_LOAD_STEPS if wg.dtype == F32 else 0
    first = n_load_steps
    prompt_spec = pl.BlockSpec((block_rows, D_MODEL), lambda i: (jnp.maximum(i - first - 1, 0), 0))
    sample_spec = pl.BlockSpec((FFN_ROWS, D_MODEL), lambda i: (0, 0))

    def streamed(w, n_steps, start):
        br = _block_rows(w.shape[0], n_steps)
        return pl.BlockSpec((br, w.shape[1]), functools.partial(
            lambda i, last: (jnp.clip(i - start, 0, last), 0), last=w.shape[0] // br - 1))

    cast_specs = [streamed(w, n_prompt_tiles, first) for w in to_cast]
    cast_shapes = [jax.ShapeDtypeStruct(w.shape, BF16) for w in to_cast]
    scratch = [pltpu.VMEM((FFN_ROWS, D_FF), BF16)]
    if n_load_steps:
        weight_specs = [streamed(w, n_load_steps, 0) for w in (wg, wu, wd)]
        scratch += [pltpu.VMEM(w.shape, BF16) for w in (wg, wu, wd)]
    else:
        weight_specs = [_resident(w.shape) for w in (wg, wu, wd)]
    return pl.pallas_call(
        functools.partial(_ffn_body, final_norm=final_norm, n_cast=len(to_cast), subtiles=subtiles,
                          n_load_steps=n_load_steps),
        grid=(n_load_steps + 1 + n_prompt_tiles,),
        in_specs=[prompt_spec, sample_spec, _resident((1, D_MODEL))] + weight_specs + [_resident((1, D_MODEL))]
        + cast_specs,
        out_specs=[prompt_spec, sample_spec] + cast_specs,
        out_shape=[jax.ShapeDtypeStruct(xp.shape, F32), jax.ShapeDtypeStruct(xs.shape, F32)] + cast_shapes,
        scratch_shapes=scratch,
        compiler_params=pltpu.CompilerParams(dimension_semantics=("arbitrary",), vmem_limit_bytes=VMEM_LIMIT),
        name="ffn_half_final" if final_norm else "ffn_half",
    )(xp, xs, norm, wg, wu, wd, fnorm, *to_cast)


def _pair_heads(t):
    swapped = pltpu.roll(t, HEAD_DIM, 1)
    left = lax.broadcasted_iota(jnp.int32, t.shape, 1) < HEAD_DIM
    return jnp.where(left, t, swapped), jnp.where(left, swapped, t)


def _gate_norm(g, vnorm_ref, gi):
    gg = g[:, gi * GROUP_CH:(gi + 1) * GROUP_CH]
    return _rms(gg, vnorm_ref[gi:gi + 1, :])


def _merge(x, attn, gate, anorm_ref, gnorm_ref, wout_ref):
    cat = jnp.concatenate([_rms(attn, anorm_ref[...]), _rms(gate, gnorm_ref[...])], axis=1)
    return x + _dot(cat.astype(BF16), wout_ref[...])


def _mixer_prompt_body(x_ref, mnorm_ref, win_ref, sinks_ref, vnorm_ref, ws_ref, bst_ref, anorm_ref, gnorm_ref,
                       wout_ref, o_ref, pk_ref, pv_ref, kv_s, q_s, attn_s, gate_s, bias_s, wtril_s):
    b = pl.program_id(0)
    t = pl.program_id(1)
    nblk = MIX_ROWS // WINDOW
    band = 2 * WINDOW
    qrows = GQA_GROUP * WINDOW

    @pl.when((b == 0) & (t == 0))
    def _init_tables():
        row = lax.broadcasted_iota(jnp.int32, (qrows, band), 0)
        key = lax.broadcasted_iota(jnp.int32, (qrows, band), 1)
        grp = row >> 7
        dist = WINDOW + (row & (WINDOW - 1)) - key
        valid = (dist >= 0) & (dist < WINDOW)
        distf = dist.astype(F32)
        for h in range(N_KV_HEADS):
            s = [ALIBI_SLOPES[h * GQA_GROUP + g] for g in range(GQA_GROUP)]
            slope = jnp.where(grp == 0, s[0], jnp.where(grp == 1, s[1], jnp.where(grp == 2, s[2], s[3])))
            bias = jnp.where(valid, -(slope * distf) * LOG2E, MASK_VALUE)
            bias_s[0, h] = bias
            bias_s[1, h] = jnp.where(key < WINDOW, MASK_VALUE, bias)
        r = lax.broadcasted_iota(jnp.int32, (CHUNK, CHUNK), 0)
        c = lax.broadcasted_iota(jnp.int32, (CHUNK, CHUNK), 1)
        for gi in range(N_GATE_GROUPS):
            wtril_s[gi] = jnp.where(r >= c, ws_ref[gi], 0.0).astype(BF16)

    @pl.when(t == 0)
    def _no_previous_block():
        kv_s[0:WINDOW, :] = jnp.zeros((WINDOW, 4 * KV_WIDTH), BF16)

    x = x_ref[...]
    h_in = _rms(x, mnorm_ref[...]).astype(BF16)
    q_s[...] = _dot(h_in, win_ref[:, 0:ATTN_WIDTH]) * (ATTN_SCALE * LOG2E)
    wkv = win_ref[:, ATTN_WIDTH:ATTN_WIDTH + 2 * KV_WIDTH]
    pkv = jnp.concatenate([_dot(h_in[0:MIX_ROWS // 2, :], wkv), _dot(h_in[MIX_ROWS // 2:, :], wkv)], axis=0)
    k_new = pkv[:, 0:KV_WIDTH]
    v_new = pkv[:, KV_WIDTH:2 * KV_WIDTH]
    pk_ref[0] = k_new[MIX_ROWS - WINDOW:, :].T
    pv_ref[0] = v_new[MIX_ROWS - WINDOW:, :].T
    k0, k1 = _pair_heads(k_new)
    v0, v1 = _pair_heads(v_new)
    kv_s[WINDOW:, :] = jnp.concatenate([k0, k1, v0, v1], axis=1).astype(BF16)

    table = jnp.where(t == 0, 1, 0)
    lane_grp = lax.broadcasted_iota(jnp.int32, (WINDOW, 2 * KV_WIDTH), 1) >> 6
    for j in range(nblk):
        rows = slice(j * WINDOW, (j + 1) * WINDOW)
        kvb = kv_s[j * WINDOW:j * WINDOW + band, :]
        for h in range(N_KV_HEADS):
            ka = kvb[:, h * KV_WIDTH:(h + 1) * KV_WIDTH]
            va = kvb[:, (2 + h) * KV_WIDTH:(3 + h) * KV_WIDTH]
            k4 = jnp.concatenate([ka, ka], axis=1)
            v4 = jnp.concatenate([va, va], axis=1)
            q = q_s[rows, h * 2 * KV_WIDTH:(h + 1) * 2 * KV_WIDTH]
            qstack = jnp.concatenate([jnp.where(lane_grp == g, q, 0.0) for g in range(GQA_GROUP)],
                                     axis=0).astype(BF16)
            bias = bias_s[table, h] if j == 0 else bias_s[0, h]
            s = _dot_nt(qstack, k4) + bias
            probs, inv = [], []
            for g in range(GQA_GROUP):
                sg = s[g * WINDOW:(g + 1) * WINDOW, :]
                sink = sinks_ref[h * GQA_GROUP + g] * LOG2E
                m = jnp.maximum(jnp.max(sg, axis=1, keepdims=True), sink)
                p = jnp.exp2(sg - m)
                inv.append(1.0 / (jnp.sum(p, axis=1, keepdims=True) + jnp.exp2(sink - m)))
                probs.append(p.astype(BF16))
            o = _dot(jnp.concatenate(probs, axis=0), v4)
            for g in range(GQA_GROUP):
                lanes = slice(g * HEAD_DIM, (g + 1) * HEAD_DIM)
                out_lanes = slice((h * GQA_GROUP + g) * HEAD_DIM, (h * GQA_GROUP + g + 1) * HEAD_DIM)
                attn_s[rows, out_lanes] = o[g * WINDOW:(g + 1) * WINDOW, lanes] * inv[g]
    kv_s[0:WINDOW, :] = kv_s[MIX_ROWS:MIX_ROWS + WINDOW, :]

    u = jax.nn.gelu(_dot(h_in, win_ref[:, 768:1280]))
    gact = jax.nn.gelu(_dot(h_in, win_ref[:, 1280:1792]))
    for gi in range(N_GATE_GROUPS):
        lanes = slice(gi * GROUP_CH, (gi + 1) * GROUP_CH)
        gn = _gate_norm(gact, vnorm_ref, gi).astype(BF16)
        chunks = jnp.concatenate([gn[c * CHUNK:(c + 1) * CHUNK, :] for c in range(nblk)], axis=1)
        mixed = _dot(wtril_s[gi], chunks) + bst_ref[:, gi:gi + 1]
        for c in range(nblk):
            rows = slice(c * CHUNK, (c + 1) * CHUNK)
            gate_s[rows, lanes] = u[rows, lanes] * mixed[:, c * GROUP_CH:(c + 1) * GROUP_CH]

    o_ref[...] = _merge(x, attn_s[...], gate_s[...], anorm_ref, gnorm_ref, wout_ref)


def _mixer_prompt(x, mnorm, win, sinks, vnorm, ws, bst, anorm, gnorm, wout, *, batch):
    rows = x.shape[0]
    tiles = rows // (batch * MIX_ROWS)
    assert rows == batch * tiles * MIX_ROWS
    band = 2 * WINDOW
    row_spec = pl.BlockSpec((MIX_ROWS, D_MODEL), lambda b, t: (b * tiles + t, 0))
    kv_spec = pl.BlockSpec((1, KV_WIDTH, WINDOW), lambda b, t: (b, 0, 0))
    kv_shape = jax.ShapeDtypeStruct((batch, KV_WIDTH, WINDOW), F32)
    return pl.pallas_call(
        _mixer_prompt_body,
        grid=(batch, tiles),
        in_specs=[row_spec, _resident((1, D_MODEL)), _resident(win.shape), _smem(), _resident(vnorm.shape),
                  _resident(ws.shape), _resident(bst.shape), _resident(anorm.shape), _resident(gnorm.shape),
                  _resident(wout.shape)],
        out_specs=[row_spec, kv_spec, kv_spec],
        out_shape=[jax.ShapeDtypeStruct(x.shape, F32), kv_shape, kv_shape],
        scratch_shapes=[
            pltpu.VMEM((WINDOW + MIX_ROWS, 4 * KV_WIDTH), BF16),
            pltpu.VMEM((MIX_ROWS, ATTN_WIDTH), F32),
            pltpu.VMEM((MIX_ROWS, ATTN_WIDTH), F32),
            pltpu.VMEM((MIX_ROWS, GMLP_WIDTH), F32),
            pltpu.VMEM((2, N_KV_HEADS, GQA_GROUP * WINDOW, band), F32),
            pltpu.VMEM((N_GATE_GROUPS, CHUNK, CHUNK), BF16),
        ],
        compiler_params=pltpu.CompilerParams(dimension_semantics=("arbitrary", "arbitrary"),
                                             vmem_limit_bytes=VMEM_LIMIT),
        name="mixer_prompt",
    )(x, mnorm, win, sinks, vnorm, ws, bst, anorm, gnorm, wout)


def _mixer_sample_body(x_ref, mnorm_ref, win_ref, sinks_ref, vnorm_ref, wsm_ref, bsm_ref, anorm_ref, gnorm_ref,
                       wout_ref, ck_ref, cv_ref, o_ref, gn_ref, sk_ref, sv_ref,
                       perm_s, qall_s, oall_s, kvt_s, gate_s, bias_s, sink_s, s_s, p_s, *, n_seq, n_tok):
    step = pl.program_id(0)
    n_steps = pl.num_programs(0)
    n_rows = n_seq * n_tok
    q_rows = N_HEADS * n_tok
    keys = 2 * WINDOW
    step_rows = SEQ_PER_STEP * q_rows

    def by_token(val):
        for c in range(val.shape[1] // LANES):
            perm_s[c] = val[:, c * LANES:(c + 1) * LANES]
        return jnp.concatenate(
            [jnp.concatenate([perm_s[c, pl.ds(i, n_seq, stride=n_tok), :] for i in range(n_tok)], axis=0)
             for c in range(val.shape[1] // LANES)], axis=1)

    def by_sequence(val):
        for c in range(val.shape[1] // LANES):
            for i in range(n_tok):
                perm_s[c, pl.ds(i, n_seq, stride=n_tok), :] = val[i * n_seq:(i + 1) * n_seq, c * LANES:(c + 1) * LANES]
        return jnp.concatenate([perm_s[c] for c in range(val.shape[1] // LANES)], axis=1)

    @pl.when(step == 0)
    def _project():
        h = _rms(x_ref[...], mnorm_ref[...])
        pkv = _dot(h.astype(BF16), win_ref[:, ATTN_WIDTH:ATTN_WIDTH + 2 * KV_WIDTH])
        kvt_s[...] = pkv.T
        h_in = by_token(h).astype(BF16)
        pq = _dot(h_in, win_ref[:, 0:ATTN_WIDTH]) * (ATTN_SCALE * LOG2E)
        half = lax.broadcasted_iota(jnp.int32, (n_rows, LANES), 1) >> 6
        for hg in range(N_HEADS):
            kvh = hg // GQA_GROUP
            slab = pq[:, (hg // 2) * LANES:(hg // 2 + 1) * LANES]
            if (hg % 2) != kvh:
                slab = pltpu.roll(slab, HEAD_DIM, 1)
            qall_s[hg * n_rows:(hg + 1) * n_rows, :] = jnp.where(half == kvh, slab, 0.0)

        u = jax.nn.gelu(_dot(h_in, win_ref[:, 768:1280]))
        gact = jax.nn.gelu(_dot(h_in, win_ref[:, 1280:1792]))
        gns = []
        for gi in range(N_GATE_GROUPS):
            lanes = slice(gi * GROUP_CH, (gi + 1) * GROUP_CH)
            gn = _gate_norm(gact, vnorm_ref, gi)
            gns.append(gn)
            for i in range(n_tok):
                rows = slice(i * n_seq, (i + 1) * n_seq)
                mixed = jnp.full((n_seq, GROUP_CH), bsm_ref[gi * n_tok + i], F32)
                for j in range(i + 1):
                    mixed = mixed + wsm_ref[(gi * n_tok + i) * n_tok + j] * gn[j * n_seq:(j + 1) * n_seq, :]
                gate_s[rows, lanes] = u[rows, lanes] * mixed
        gn_ref[...] = by_sequence(jnp.concatenate(gns, axis=1))

        row = lax.broadcasted_iota(jnp.int32, (step_rows, keys), 0)
        col = lax.broadcasted_iota(jnp.int32, (step_rows, keys), 1)
        tok = row & (n_tok - 1)
        head = (row >> 2) & (N_HEADS - 1)
        slope = jnp.zeros((step_rows, keys), F32)
        for hg in range(N_HEADS):
            slope = jnp.where(head == hg, ALIBI_SLOPES[hg], slope)
        shifted = col < WINDOW
        pos = jnp.where(shifted, col + n_tok, col - WINDOW)
        dist = WINDOW + tok - pos
        valid = (dist >= 0) & (dist < WINDOW) & (shifted | (pos < n_tok))
        bias_s[...] = jnp.where(valid, -(slope * dist.astype(F32)) * LOG2E, MASK_VALUE)
        head1 = (lax.broadcasted_iota(jnp.int32, (step_rows, 1), 0) >> 2) & (N_HEADS - 1)
        sink = jnp.zeros((step_rows, 1), F32)
        for hg in range(N_HEADS):
            sink = jnp.where(head1 == hg, sinks_ref[hg] * LOG2E, sink)
        sink_s[...] = sink

    col0 = pl.multiple_of(step * LANES, LANES)
    k_cols = kvt_s[0:KV_WIDTH, pl.ds(col0, LANES)]
    v_cols = kvt_s[KV_WIDTH:2 * KV_WIDTH, pl.ds(col0, LANES)]
    is_new = lax.broadcasted_iota(jnp.int32, (KV_WIDTH, WINDOW), 1) >= WINDOW - n_tok
    for bl in range(SEQ_PER_STEP):
        to_tail = (WINDOW - n_tok - n_tok * bl) % LANES
        sk_ref[bl] = jnp.where(is_new, pltpu.roll(k_cols, to_tail, 1), pltpu.roll(ck_ref[bl], WINDOW - n_tok, 1))
        sv_ref[bl] = jnp.where(is_new, pltpu.roll(v_cols, to_tail, 1), pltpu.roll(cv_ref[bl], WINDOW - n_tok, 1))

    seq0 = step * SEQ_PER_STEP
    for bl in range(SEQ_PER_STEP):
        qb = qall_s[pl.ds(seq0 + bl, q_rows, stride=n_seq), :].astype(BF16)
        k2 = jnp.concatenate([sk_ref[bl], ck_ref[bl]], axis=1).astype(BF16)
        s_s[bl * q_rows:(bl + 1) * q_rows, :] = _dot(qb, k2)
    s = s_s[...] + bias_s[...]
    sink = sink_s[...]
    m = jnp.maximum(jnp.max(s, axis=1, keepdims=True), sink)
    p = jnp.exp2(s - m)
    inv = 1.0 / (jnp.sum(p, axis=1, keepdims=True) + jnp.exp2(sink - m))
    p_s[...] = p.astype(BF16)
    for bl in range(SEQ_PER_STEP):
        rows = slice(bl * q_rows, (bl + 1) * q_rows)
        v2 = jnp.concatenate([sv_ref[bl], cv_ref[bl]], axis=1).astype(BF16)
        oall_s[pl.ds(seq0 + bl, q_rows, stride=n_seq), :] = _dot_nt(p_s[rows, :], v2) * inv[rows, :]

    @pl.when(step == n_steps - 1)
    def _merge_rows():
        left = lax.broadcasted_iota(jnp.int32, (n_rows, LANES), 1) < HEAD_DIM
        slabs = []
        for pair in range(N_HEADS // 2):
            halves = []
            for hg in (2 * pair, 2 * pair + 1):
                o = oall_s[hg * n_rows:(hg + 1) * n_rows, :]
                if (hg % 2) != (hg // GQA_GROUP):
                    o = pltpu.roll(o, HEAD_DIM, 1)
                halves.append(o)
            slabs.append(jnp.where(left, halves[0], halves[1]))
        attn = jnp.concatenate(slabs, axis=1)
        o_ref[...] = by_sequence(_merge(by_token(x_ref[...]), attn, gate_s[...], anorm_ref, gnorm_ref, wout_ref))


def _mixer_sample(x, mnorm, win, sinks, vnorm, wsm, bsm, anorm, gnorm, wout, ck, cv, *, n_seq, n_tok):
    n_rows = n_seq * n_tok
    q_rows = N_HEADS * n_tok
    assert x.shape[0] == n_rows and n_seq % SEQ_PER_STEP == 0 and n_tok == 4 and n_seq == LANES
    assert SEQ_PER_STEP * n_tok == LANES
    cache_spec = pl.BlockSpec((SEQ_PER_STEP, KV_WIDTH, WINDOW), lambda s: (s, 0, 0))
    cache_shape = jax.ShapeDtypeStruct(ck.shape, F32)
    rows_spec = pl.BlockSpec((n_rows, D_MODEL), lambda s: (0, 0))
    gn_spec = pl.BlockSpec((n_rows, GMLP_WIDTH), lambda s: (0, 0))
    return pl.pallas_call(
        functools.partial(_mixer_sample_body, n_seq=n_seq, n_tok=n_tok),
        grid=(n_seq // SEQ_PER_STEP,),
        in_specs=[rows_spec, _resident((1, D_MODEL)), _resident(win.shape), _smem(), _resident(vnorm.shape),
                  _smem(), _smem(), _resident(anorm.shape), _resident(gnorm.shape), _resident(wout.shape),
                  cache_spec, cache_spec],
        out_specs=[rows_spec, gn_spec, cache_spec, cache_spec],
        out_shape=[jax.ShapeDtypeStruct(x.shape, F32), jax.ShapeDtypeStruct((n_rows, GMLP_WIDTH), F32),
                   cache_shape, cache_shape],
        scratch_shapes=[
            pltpu.VMEM((D_MODEL // LANES, n_rows, LANES), F32),
            pltpu.VMEM((N_HEADS * n_rows, LANES), F32),
            pltpu.VMEM((N_HEADS * n_rows, LANES), F32),
            pltpu.VMEM((2 * KV_WIDTH, n_rows), F32),
            pltpu.VMEM((n_rows, GMLP_WIDTH), F32),
            pltpu.VMEM((SEQ_PER_STEP * q_rows, 2 * WINDOW), F32),
            pltpu.VMEM((SEQ_PER_STEP * q_rows, 1), F32),
            pltpu.VMEM((SEQ_PER_STEP * q_rows, 2 * WINDOW), F32),
            pltpu.VMEM((SEQ_PER_STEP * q_rows, 2 * WINDOW), BF16),
        ],
        compiler_params=pltpu.CompilerParams(dimension_semantics=("arbitrary",), vmem_limit_bytes=VMEM_LIMIT),
        name="mixer_sample",
    )(x, mnorm, win, sinks, vnorm, wsm, bsm, anorm, gnorm, wout, ck, cv)


def kernel(x_prompt, x_sample, cache_k, cache_v, ffn1_norm, ffn1_w_gate, ffn1_w_up, ffn1_w_down, mix_norm, w_in,
           attn_sinks, gmlp_v_norm, gmlp_w_spatial, gmlp_b_spatial, attn_out_norm, gmlp_out_norm, w_out, ffn2_norm,
           ffn2_w_gate, ffn2_w_up, ffn2_w_down, final_norm):
    batch, seq, _ = x_prompt.shape
    n_seq, n_tok, _ = x_sample.shape
    depth = cache_k.shape[0]
    assert depth == 1

    xp = x_prompt.reshape(batch * seq, D_MODEL)
    xs = x_sample.reshape(n_seq * n_tok, D_MODEL)
    row = lambda v: v.reshape(1, -1)
    fnorm = row(final_norm)
    l = 0
    sinks = attn_sinks[l]
    vnorm = gmlp_v_norm[l]
    ws = gmlp_w_spatial[l]
    bs = gmlp_b_spatial[l]
    mnorm, anorm, gnorm = row(mix_norm[l]), row(attn_out_norm[l]), row(gmlp_out_norm[l])

    later = (ffn2_w_gate[l], ffn2_w_up[l], ffn2_w_down[l], w_in[l], w_out[l])
    xp, xs, wg2, wu2, wd2, win, wout = _ffn_half(xp, xs, row(ffn1_norm[l]), ffn1_w_gate[l], ffn1_w_up[l],
                                                 ffn1_w_down[l], fnorm, final_norm=False, subtiles=1, to_cast=later)
    xp, pk, pv = _mixer_prompt(xp, mnorm, win, sinks, vnorm, ws, bs.T, anorm, gnorm, wout, batch=batch)
    to_cols = lambda c: jnp.transpose(c, (0, 2, 3, 1)).reshape(n_seq, KV_WIDTH, WINDOW)
    to_rows = lambda c: jnp.transpose(c.reshape(-1, N_KV_HEADS, HEAD_DIM, WINDOW), (0, 3, 1, 2))[None]
    xs, gn, sk, sv = _mixer_sample(xs, mnorm, win, sinks, vnorm, ws[:, :n_tok, :n_tok].reshape(-1),
                                   bs[:, :n_tok].reshape(-1), anorm, gnorm, wout, to_cols(cache_k[l]),
                                   to_cols(cache_v[l]), n_seq=n_seq, n_tok=n_tok)
    yp, ys = _ffn_half(xp, xs, row(ffn2_norm[l]), wg2, wu2, wd2, fnorm, final_norm=True, subtiles=2)

    y_prompt = yp.reshape(batch, seq, D_MODEL)
    y_sample = ys.reshape(n_seq, n_tok, D_MODEL)
    chunk_v = gn.reshape(1, n_seq, n_tok, GMLP_WIDTH)
    return (y_prompt, y_sample, to_rows(pk), to_rows(pv), to_rows(sk), to_rows(sv), chunk_v)
```

```python
import functools

import jax
import jax.numpy as jnp
from jax import lax
from jax.experimental import pallas as pl
from jax.experimental.pallas import tpu as pltpu

F32 = jnp.float32
BF16 = jnp.bfloat16

D_MODEL = 1024
D_FF = 2816
N_HEADS = 8
N_KV_HEADS = 2
GQA_GROUP = 4
HEAD_DIM = 64
ATTN_WIDTH = 512
KV_WIDTH = 128
GMLP_WIDTH = 512
N_GATE_GROUPS = 4
GROUP_CH = 128
WINDOW = 128
CHUNK = 128
RMS_EPS = 1e-6
FFN_RESIDUAL = 0.5
ATTN_SCALE = 0.125
LOG2E = 1.4426950408889634
ALIBI_SLOPES = tuple(2.0 ** (-(i + 1)) for i in range(N_HEADS))
MASK_VALUE = -1e30

LANES = 128
BF16_SUBLANES = 16
FFN_ROWS = 512
FFN_LOAD_STEPS = 8
FFN_CHUNK = 256
MIX_ROWS = 1024
SEQ_PER_STEP = 32
VMEM_LIMIT = 56 * 1024 * 1024


def _rms(x, g):
    ms = jnp.mean(x * x, axis=-1, keepdims=True)
    return x * lax.rsqrt(ms + RMS_EPS) * g


def _dot(a, b):
    return jnp.dot(a, b, preferred_element_type=F32)


def _dot_nt(a, b):
    return lax.dot_general(a, b, (((1,), (1,)), ((), ())), preferred_element_type=F32)


def _resident(shape):
    nd = len(shape)
    return pl.BlockSpec(shape, lambda *_: (0,) * nd, pipeline_mode=pl.Buffered(1))


def _smem():
    return pl.BlockSpec(memory_space=pltpu.SMEM)


def _ffn_body(*refs, final_norm, n_cast, subtiles, n_load_steps):
    xp_ref, xs_ref, norm_ref, wg_in, wu_in, wd_in, fnorm_ref = refs[:7]
    cast_in = refs[7:7 + n_cast]
    op_ref, os_ref = refs[7 + n_cast:9 + n_cast]
    cast_out = refs[9 + n_cast:9 + 2 * n_cast]
    act_ref = refs[9 + 2 * n_cast]
    scratch = list(refs[10 + 2 * n_cast:])
    i = pl.program_id(0)
    first = n_load_steps

    if len(xs_ref.shape) == 3:
        xs_rows = scratch.pop()

        @pl.when(i == first)
        def _flatten_sample_rows():
            xs_rows[...] = xs_ref[...].reshape(FFN_ROWS, D_MODEL)
    else:
        xs_rows = xs_ref

    if n_load_steps:
        wg_ref, wu_ref, wd_ref = scratch

        @pl.when(i < n_load_steps)
        def _load_weights():
            for src, dst in ((wg_in, wg_ref), (wu_in, wu_ref), (wd_in, wd_ref)):
                br = src.shape[0]
                dst[pl.ds(pl.multiple_of(i * br, br), br), :] = src[...].astype(BF16)
    else:
        wg_ref, wu_ref, wd_ref = wg_in, wu_in, wd_in

    for src, dst in zip(cast_in, cast_out):
        dst[...] = src[...].astype(BF16)

    def _sub_tile(k, carry):
        rows = pl.ds(pl.multiple_of(k * FFN_ROWS, FFN_ROWS), FFN_ROWS)
        x = jnp.where(i == first, xs_rows[...], xp_ref[rows, :])
        h = _rms(x, norm_ref[...]).astype(BF16)
        for c in range(D_FF // FFN_CHUNK):
            cols = slice(c * FFN_CHUNK, (c + 1) * FFN_CHUNK)
            a = _dot(h, wg_ref[:, cols])
            b = _dot(h, wu_ref[:, cols])
            act_ref[:, cols] = (a * jax.nn.sigmoid(a) * b).astype(BF16)
        y = x + FFN_RESIDUAL * _dot(act_ref[...], wd_ref[...])
        if final_norm:
            y = _rms(y, fnorm_ref[...])
        op_ref[rows, :] = y
        return carry

    trips = jnp.where(i < first, 0, jnp.where(i == first, 1, subtiles))
    lax.fori_loop(0, trips, _sub_tile, 0)

    @pl.when(i == first)
    def _():
        os_ref[...] = op_ref[0:FFN_ROWS, :].reshape(os_ref.shape)


def _block_rows(rows, n_steps):
    br = BF16_SUBLANES
    while rows % br or rows // br > n_steps:
        br += BF16_SUBLANES
    return br


def _ffn_half(xp, xs, norm, wg, wu, wd, fnorm, *, final_norm, subtiles, to_cast=(), sample_out_shape=None):
    block_rows = subtiles * FFN_ROWS
    n_prompt_tiles = xp.shape[0] // block_rows
    sample_out_shape = sample_out_shape or (FFN_ROWS, D_MODEL)
    assert xp.shape[0] == n_prompt_tiles * block_rows and xs.size == FFN_ROWS * D_MODEL
    n_load_steps = FFN_LOAD_STEPS if wg.dtype == F32 else 0
    first = n_load_steps
    prompt_spec = pl.BlockSpec((block_rows, D_MODEL), lambda i: (jnp.maximum(i - first - 1, 0), 0))
    sample_in_spec = pl.BlockSpec(xs.shape, lambda i: (0,) * xs.ndim)
    sample_out_spec = pl.BlockSpec(sample_out_shape, lambda i: (0,) * len(sample_out_shape))

    def streamed(w, n_steps, start):
        br = _block_rows(w.shape[0], n_steps)
        return pl.BlockSpec((br, w.shape[1]), functools.partial(
            lambda i, last: (jnp.clip(i - start, 0, last), 0), last=w.shape[0] // br - 1))

    cast_specs = [streamed(w, n_prompt_tiles, first) for w in to_cast]
    cast_shapes = [jax.ShapeDtypeStruct(w.shape, BF16) for w in to_cast]
    scratch = [pltpu.VMEM((FFN_ROWS, D_FF), BF16)]
    if n_load_steps:
        weight_specs = [streamed(w, n_load_steps, 0) for w in (wg, wu, wd)]
        scratch += [pltpu.VMEM(w.shape, BF16) for w in (wg, wu, wd)]
    else:
        weight_specs = [_resident(w.shape) for w in (wg, wu, wd)]
    if xs.ndim == 3:
        scratch += [pltpu.VMEM((FFN_ROWS, D_MODEL), F32)]
    return pl.pallas_call(
        functools.partial(_ffn_body, final_norm=final_norm, n_cast=len(to_cast), subtiles=subtiles,
                          n_load_steps=n_load_steps),
        grid=(n_load_steps + 1 + n_prompt_tiles,),
        in_specs=[prompt_spec, sample_in_spec, _resident((1, D_MODEL))] + weight_specs + [_resident((1, D_MODEL))]
        + cast_specs,
        out_specs=[prompt_spec, sample_out_spec] + cast_specs,
        out_shape=[jax.ShapeDtypeStruct(xp.shape, F32), jax.ShapeDtypeStruct(sample_out_shape, F32)] + cast_shapes,
        scratch_shapes=scratch,
        compiler_params=pltpu.CompilerParams(dimension_semantics=("arbitrary",), vmem_limit_bytes=VMEM_LIMIT),
        name="ffn_half_final" if final_norm else "ffn_half",
    )(xp, xs, norm, wg, wu, wd, fnorm, *to_cast)


def _pair_heads(t):
    swapped = pltpu.roll(t, HEAD_DIM, 1)
    left = lax.broadcasted_iota(jnp.int32, t.shape, 1) < HEAD_DIM
    return jnp.where(left, t, swapped), jnp.where(left, swapped, t)


def _gate_norm(g, vnorm_ref, gi):
    gg = g[:, gi * GROUP_CH:(gi + 1) * GROUP_CH]
    return _rms(gg, vnorm_ref[gi:gi + 1, :])


def _merge(x, attn, gate, anorm_ref, gnorm_ref, wout_ref):
    cat = jnp.concatenate([_rms(attn, anorm_ref[...]), _rms(gate, gnorm_ref[...])], axis=1)
    return x + _dot(cat.astype(BF16), wout_ref[...])


def _mixer_prompt_body(x_ref, mnorm_ref, win_ref, sinks_ref, vnorm_ref, ws_ref, bst_ref, anorm_ref, gnorm_ref,
                       wout_ref, o_ref, pk_ref, pv_ref, kv_s, q_s, attn_s, gate_s, bias_s, wtril_s):
    b = pl.program_id(0)
    t = pl.program_id(1)
    nblk = MIX_ROWS // WINDOW
    band = 2 * WINDOW
    qrows = GQA_GROUP * WINDOW

    @pl.when((b == 0) & (t == 0))
    def _init_tables():
        row = lax.broadcasted_iota(jnp.int32, (qrows, band), 0)
        key = lax.broadcasted_iota(jnp.int32, (qrows, band), 1)
        grp = row >> 7
        dist = WINDOW + (row & (WINDOW - 1)) - key
        valid = (dist >= 0) & (dist < WINDOW)
        distf = dist.astype(F32)
        for h in range(N_KV_HEADS):
            s = [ALIBI_SLOPES[h * GQA_GROUP + g] for g in range(GQA_GROUP)]
            slope = jnp.where(grp == 0, s[0], jnp.where(grp == 1, s[1], jnp.where(grp == 2, s[2], s[3])))
            bias = jnp.where(valid, -(slope * distf) * LOG2E, MASK_VALUE)
            bias_s[0, h] = bias
            bias_s[1, h] = jnp.where(key < WINDOW, MASK_VALUE, bias)
        r = lax.broadcasted_iota(jnp.int32, (CHUNK, CHUNK), 0)
        c = lax.broadcasted_iota(jnp.int32, (CHUNK, CHUNK), 1)
        for gi in range(N_GATE_GROUPS):
            wtril_s[gi] = jnp.where(r >= c, ws_ref[gi], 0.0).astype(BF16)

    @pl.when(t == 0)
    def _no_previous_block():
        kv_s[0:WINDOW, :] = jnp.zeros((WINDOW, 4 * KV_WIDTH), BF16)

    x = x_ref[...]
    h_in = _rms(x, mnorm_ref[...]).astype(BF16)
    q_s[...] = _dot(h_in, win_ref[:, 0:ATTN_WIDTH]) * (ATTN_SCALE * LOG2E)
    wkv = win_ref[:, ATTN_WIDTH:ATTN_WIDTH + 2 * KV_WIDTH]
    pkv = jnp.concatenate([_dot(h_in[0:MIX_ROWS // 2, :], wkv), _dot(h_in[MIX_ROWS // 2:, :], wkv)], axis=0)
    k_new = pkv[:, 0:KV_WIDTH]
    v_new = pkv[:, KV_WIDTH:2 * KV_WIDTH]
    pk_ref[0] = k_new[MIX_ROWS - WINDOW:, :].T
    pv_ref[0] = v_new[MIX_ROWS - WINDOW:, :].T
    k0, k1 = _pair_heads(k_new)
    v0, v1 = _pair_heads(v_new)
    kv_s[WINDOW:, :] = jnp.concatenate([k0, k1, v0, v1], axis=1).astype(BF16)

    table = jnp.where(t == 0, 1, 0)
    lane_grp = lax.broadcasted_iota(jnp.int32, (WINDOW, 2 * KV_WIDTH), 1) >> 6
    for j in range(nblk):
        rows = slice(j * WINDOW, (j + 1) * WINDOW)
        kvb = kv_s[j * WINDOW:j * WINDOW + band, :]
        for h in range(N_KV_HEADS):
            ka = kvb[:, h * KV_WIDTH:(h + 1) * KV_WIDTH]
            va = kvb[:, (2 + h) * KV_WIDTH:(3 + h) * KV_WIDTH]
            k4 = jnp.concatenate([ka, ka], axis=1)
            v4 = jnp.concatenate([va, va], axis=1)
            q = q_s[rows, h * 2 * KV_WIDTH:(h + 1) * 2 * KV_WIDTH]
            qstack = jnp.concatenate([jnp.where(lane_grp == g, q, 0.0) for g in range(GQA_GROUP)],
                                     axis=0).astype(BF16)
            bias = bias_s[table, h] if j == 0 else bias_s[0, h]
            s = _dot_nt(qstack, k4) + bias
            probs, inv = [], []
            for g in range(GQA_GROUP):
                sg = s[g * WINDOW:(g + 1) * WINDOW, :]
                sink = sinks_ref[h * GQA_GROUP + g] * LOG2E
                m = jnp.maximum(jnp.max(sg, axis=1, keepdims=True), sink)
                p = jnp.exp2(sg - m)
                inv.append(1.0 / (jnp.sum(p, axis=1, keepdims=True) + jnp.exp2(sink - m)))
                probs.append(p.astype(BF16))
            o = _dot(jnp.concatenate(probs, axis=0), v4)
            for g in range(GQA_GROUP):
                lanes = slice(g * HEAD_DIM, (g + 1) * HEAD_DIM)
                out_lanes = slice((h * GQA_GROUP + g) * HEAD_DIM, (h * GQA_GROUP + g + 1) * HEAD_DIM)
                attn_s[rows, out_lanes] = o[g * WINDOW:(g + 1) * WINDOW, lanes] * inv[g]
    kv_s[0:WINDOW, :] = kv_s[MIX_ROWS:MIX_ROWS + WINDOW, :]

    u = jax.nn.gelu(_dot(h_in, win_ref[:, 768:1280]))
    gact = jax.nn.gelu(_dot(h_in, win_ref[:, 1280:1792]))
    for gi in range(N_GATE_GROUPS):
        lanes = slice(gi * GROUP_CH, (gi + 1) * GROUP_CH)
        gn = _gate_norm(gact, vnorm_ref, gi).astype(BF16)
        chunks = jnp.concatenate([gn[c * CHUNK:(c + 1) * CHUNK, :] for c in range(nblk)], axis=1)
        mixed = _dot(wtril_s[gi], chunks) + bst_ref[:, gi:gi + 1]
        for c in range(nblk):
            rows = slice(c * CHUNK, (c + 1) * CHUNK)
            gate_s[rows, lanes] = u[rows, lanes] * mixed[:, c * GROUP_CH:(c + 1) * GROUP_CH]

    o_ref[...] = _merge(x, attn_s[...], gate_s[...], anorm_ref, gnorm_ref, wout_ref)


def _mixer_prompt(x, mnorm, win, sinks, vnorm, ws, bst, anorm, gnorm, wout, *, batch):
    rows = x.shape[0]
    tiles = rows // (batch * MIX_ROWS)
    assert rows == batch * tiles * MIX_ROWS
    band = 2 * WINDOW
    row_spec = pl.BlockSpec((MIX_ROWS, D_MODEL), lambda b, t: (b * tiles + t, 0))
    kv_spec = pl.BlockSpec((1, KV_WIDTH, WINDOW), lambda b, t: (b, 0, 0))
    kv_shape = jax.ShapeDtypeStruct((batch, KV_WIDTH, WINDOW), F32)
    return pl.pallas_call(
        _mixer_prompt_body,
        grid=(batch, tiles),
        in_specs=[row_spec, _resident((1, D_MODEL)), _resident(win.shape), _smem(), _resident(vnorm.shape),
                  _resident(ws.shape), _resident(bst.shape), _resident(anorm.shape), _resident(gnorm.shape),
                  _resident(wout.shape)],
        out_specs=[row_spec, kv_spec, kv_spec],
        out_shape=[jax.ShapeDtypeStruct(x.shape, F32), kv_shape, kv_shape],
        scratch_shapes=[
            pltpu.VMEM((WINDOW + MIX_ROWS, 4 * KV_WIDTH), BF16),
            pltpu.VMEM((MIX_ROWS, ATTN_WIDTH), F32),
            pltpu.VMEM((MIX_ROWS, ATTN_WIDTH), F32),
            pltpu.VMEM((MIX_ROWS, GMLP_WIDTH), F32),
            pltpu.VMEM((2, N_KV_HEADS, GQA_GROUP * WINDOW, band), F32),
            pltpu.VMEM((N_GATE_GROUPS, CHUNK, CHUNK), BF16),
        ],
        compiler_params=pltpu.CompilerParams(dimension_semantics=("arbitrary", "arbitrary"),
                                             vmem_limit_bytes=VMEM_LIMIT),
        name="mixer_prompt",
    )(x, mnorm, win, sinks, vnorm, ws, bst, anorm, gnorm, wout)


def _mixer_sample_body(x_ref, mnorm_ref, win_ref, sinks_ref, vnorm_ref, wsm_ref, bsm_ref, anorm_ref, gnorm_ref,
                       wout_ref, ck_ref, cv_ref, o_ref, gn_ref, sk_ref, sv_ref,
                       perm_s, qall_s, oall_s, kvt_s, gate_s, bias_s, sink_s, s_s, p_s, *, n_seq, n_tok):
    step = pl.program_id(0)
    n_steps = pl.num_programs(0)
    n_rows = n_seq * n_tok
    q_rows = N_HEADS * n_tok
    keys = 2 * WINDOW
    step_rows = SEQ_PER_STEP * q_rows

    def by_token(val):
        for c in range(val.shape[1] // LANES):
            perm_s[c] = val[:, c * LANES:(c + 1) * LANES]
        return jnp.concatenate(
            [jnp.concatenate([perm_s[c, pl.ds(i, n_seq, stride=n_tok), :] for i in range(n_tok)], axis=0)
             for c in range(val.shape[1] // LANES)], axis=1)

    def by_sequence(val):
        for c in range(val.shape[1] // LANES):
            for i in range(n_tok):
                perm_s[c, pl.ds(i, n_seq, stride=n_tok), :] = val[i * n_seq:(i + 1) * n_seq, c * LANES:(c + 1) * LANES]
        return jnp.concatenate([perm_s[c] for c in range(val.shape[1] // LANES)], axis=1)

    @pl.when(step == 0)
    def _project():
        h = _rms(x_ref[...], mnorm_ref[...])
        pkv = _dot(h.astype(BF16), win_ref[:, ATTN_WIDTH:ATTN_WIDTH + 2 * KV_WIDTH])
        kvt_s[...] = pkv.T
        h_in = by_token(h).astype(BF16)
        pq = _dot(h_in, win_ref[:, 0:ATTN_WIDTH]) * (ATTN_SCALE * LOG2E)
        half = lax.broadcasted_iota(jnp.int32, (n_rows, LANES), 1) >> 6
        for hg in range(N_HEADS):
            kvh = hg // GQA_GROUP
            slab = pq[:, (hg // 2) * LANES:(hg // 2 + 1) * LANES]
            if (hg % 2) != kvh:
                slab = pltpu.roll(slab, HEAD_DIM, 1)
            qall_s[hg * n_rows:(hg + 1) * n_rows, :] = jnp.where(half == kvh, slab, 0.0)

        u = jax.nn.gelu(_dot(h_in, win_ref[:, 768:1280]))
        gact = jax.nn.gelu(_dot(h_in, win_ref[:, 1280:1792]))
        gns = []
        for gi in range(N_GATE_GROUPS):
            lanes = slice(gi * GROUP_CH, (gi + 1) * GROUP_CH)
            gn = _gate_norm(gact, vnorm_ref, gi)
            gns.append(gn)
            for i in range(n_tok):
                rows = slice(i * n_seq, (i + 1) * n_seq)
                mixed = jnp.full((n_seq, GROUP_CH), bsm_ref[gi * n_tok + i], F32)
                for j in range(i + 1):
                    mixed = mixed + wsm_ref[(gi * n_tok + i) * n_tok + j] * gn[j * n_seq:(j + 1) * n_seq, :]
                gate_s[rows, lanes] = u[rows, lanes] * mixed
        gn_ref[...] = by_sequence(jnp.concatenate(gns, axis=1)).reshape(gn_ref.shape)

        row = lax.broadcasted_iota(jnp.int32, (step_rows, keys), 0)
        col = lax.broadcasted_iota(jnp.int32, (step_rows, keys), 1)
        tok = row & (n_tok - 1)
        head = (row >> 2) & (N_HEADS - 1)
        slope = jnp.zeros((step_rows, keys), F32)
        for hg in range(N_HEADS):
            slope = jnp.where(head == hg, ALIBI_SLOPES[hg], slope)
        shifted = col < WINDOW
        pos = jnp.where(shifted, col + n_tok, col - WINDOW)
        dist = WINDOW + tok - pos
        valid = (dist >= 0) & (dist < WINDOW) & (shifted | (pos < n_tok))
        bias_s[...] = jnp.where(valid, -(slope * dist.astype(F32)) * LOG2E, MASK_VALUE)
        head1 = (lax.broadcasted_iota(jnp.int32, (step_rows, 1), 0) >> 2) & (N_HEADS - 1)
        sink = jnp.zeros((step_rows, 1), F32)
        for hg in range(N_HEADS):
            sink = jnp.where(head1 == hg, sinks_ref[hg] * LOG2E, sink)
        sink_s[...] = sink

    col0 = pl.multiple_of(step * LANES, LANES)
    k_cols = kvt_s[0:KV_WIDTH, pl.ds(col0, LANES)]
    v_cols = kvt_s[KV_WIDTH:2 * KV_WIDTH, pl.ds(col0, LANES)]
    is_new = lax.broadcasted_iota(jnp.int32, (KV_WIDTH, WINDOW), 1) >= WINDOW - n_tok
    for bl in range(SEQ_PER_STEP):
        to_tail = (WINDOW - n_tok - n_tok * bl) % LANES
        sk_ref[bl] = jnp.where(is_new, pltpu.roll(k_cols, to_tail, 1), pltpu.roll(ck_ref[bl], WINDOW - n_tok, 1))
        sv_ref[bl] = jnp.where(is_new, pltpu.roll(v_cols, to_tail, 1), pltpu.roll(cv_ref[bl], WINDOW - n_tok, 1))

    seq0 = step * SEQ_PER_STEP
    for bl in range(SEQ_PER_STEP):
        qb = qall_s[pl.ds(seq0 + bl, q_rows, stride=n_seq), :].astype(BF16)
        k2 = jnp.concatenate([sk_ref[bl], ck_ref[bl]], axis=1).astype(BF16)
        s_s[bl * q_rows:(bl + 1) * q_rows, :] = _dot(qb, k2)
    s = s_s[...] + bias_s[...]
    sink = sink_s[...]
    m = jnp.maximum(jnp.max(s, axis=1, keepdims=True), sink)
    p = jnp.exp2(s - m)
    inv = 1.0 / (jnp.sum(p, axis=1, keepdims=True) + jnp.exp2(sink - m))
    p_s[...] = p.astype(BF16)
    for bl in range(SEQ_PER_STEP):
        rows = slice(bl * q_rows, (bl + 1) * q_rows)
        v2 = jnp.concatenate([sv_ref[bl], cv_ref[bl]], axis=1).astype(BF16)
        oall_s[pl.ds(seq0 + bl, q_rows, stride=n_seq), :] = _dot_nt(p_s[rows, :], v2) * inv[rows, :]

    @pl.when(step == n_steps - 1)
    def _merge_rows():
        left = lax.broadcasted_iota(jnp.int32, (n_rows, LANES), 1) < HEAD_DIM
        slabs = []
        for pair in range(N_HEADS // 2):
            halves = []
            for hg in (2 * pair, 2 * pair + 1):
                o = oall_s[hg * n_rows:(hg + 1) * n_rows, :]
                if (hg % 2) != (hg // GQA_GROUP):
                    o = pltpu.roll(o, HEAD_DIM, 1)
                halves.append(o)
            slabs.append(jnp.where(left, halves[0], halves[1]))
        attn = jnp.concatenate(slabs, axis=1)
        o_ref[...] = by_sequence(_merge(by_token(x_ref[...]), attn, gate_s[...], anorm_ref, gnorm_ref, wout_ref))


def _mixer_sample(x, mnorm, win, sinks, vnorm, wsm, bsm, anorm, gnorm, wout, ck, cv, *, n_seq, n_tok):
    n_rows = n_seq * n_tok
    q_rows = N_HEADS * n_tok
    assert x.shape[0] == n_rows and n_seq % SEQ_PER_STEP == 0 and n_tok == 4 and n_seq == LANES
    assert SEQ_PER_STEP * n_tok == LANES
    cache_spec = pl.BlockSpec((SEQ_PER_STEP, KV_WIDTH, WINDOW), lambda s: (s, 0, 0))
    cache_shape = jax.ShapeDtypeStruct(ck.shape, F32)
    rows_spec = pl.BlockSpec((n_rows, D_MODEL), lambda s: (0, 0))
    gn_spec = pl.BlockSpec((n_seq, n_tok, GMLP_WIDTH), lambda s: (0, 0, 0))
    return pl.pallas_call(
        functools.partial(_mixer_sample_body, n_seq=n_seq, n_tok=n_tok),
        grid=(n_seq // SEQ_PER_STEP,),
        in_specs=[rows_spec, _resident((1, D_MODEL)), _resident(win.shape), _smem(), _resident(vnorm.shape),
                  _smem(), _smem(), _resident(anorm.shape), _resident(gnorm.shape), _resident(wout.shape),
                  cache_spec, cache_spec],
        out_specs=[rows_spec, gn_spec, cache_spec, cache_spec],
        out_shape=[jax.ShapeDtypeStruct(x.shape, F32), jax.ShapeDtypeStruct((n_seq, n_tok, GMLP_WIDTH), F32),
                   cache_shape, cache_shape],
        scratch_shapes=[
            pltpu.VMEM((D_MODEL // LANES, n_rows, LANES), F32),
            pltpu.VMEM((N_HEADS * n_rows, LANES), F32),
            pltpu.VMEM((N_HEADS * n_rows, LANES), F32),
            pltpu.VMEM((2 * KV_WIDTH, n_rows), F32),
            pltpu.VMEM((n_rows, GMLP_WIDTH), F32),
            pltpu.VMEM((SEQ_PER_STEP * q_rows, 2 * WINDOW), F32),
            pltpu.VMEM((SEQ_PER_STEP * q_rows, 1), F32),
            pltpu.VMEM((SEQ_PER_STEP * q_rows, 2 * WINDOW), F32),
            pltpu.VMEM((SEQ_PER_STEP * q_rows, 2 * WINDOW), BF16),
        ],
        compiler_params=pltpu.CompilerParams(dimension_semantics=("arbitrary",), vmem_limit_bytes=VMEM_LIMIT),
        name="mixer_sample",
    )(x, mnorm, win, sinks, vnorm, wsm, bsm, anorm, gnorm, wout, ck, cv)


def kernel(x_prompt, x_sample, cache_k, cache_v, ffn1_norm, ffn1_w_gate, ffn1_w_up, ffn1_w_down, mix_norm, w_in,
           attn_sinks, gmlp_v_norm, gmlp_w_spatial, gmlp_b_spatial, attn_out_norm, gmlp_out_norm, w_out, ffn2_norm,
           ffn2_w_gate, ffn2_w_up, ffn2_w_down, final_norm):
    batch, seq, _ = x_prompt.shape
    n_seq, n_tok, _ = x_sample.shape
    depth = cache_k.shape[0]
    assert depth == 1

    xp = x_prompt.reshape(batch * seq, D_MODEL)
    row = lambda v: v.reshape(1, -1)
    fnorm = row(final_norm)
    l = 0
    sinks = attn_sinks[l]
    vnorm = gmlp_v_norm[l]
    ws = gmlp_w_spatial[l]
    bs = gmlp_b_spatial[l]
    mnorm, anorm, gnorm = row(mix_norm[l]), row(attn_out_norm[l]), row(gmlp_out_norm[l])

    later = (ffn2_w_gate[l], ffn2_w_up[l], ffn2_w_down[l], w_in[l], w_out[l])
    xp, xs, wg2, wu2, wd2, win, wout = _ffn_half(xp, x_sample, row(ffn1_norm[l]), ffn1_w_gate[l], ffn1_w_up[l],
                                                 ffn1_w_down[l], fnorm, final_norm=False, subtiles=1, to_cast=later)
    xp, pk, pv = _mixer_prompt(xp, mnorm, win, sinks, vnorm, ws, bs.T, anorm, gnorm, wout, batch=batch)
    to_cols = lambda c: jnp.transpose(c, (0, 2, 3, 1)).reshape(n_seq, KV_WIDTH, WINDOW)
    to_rows = lambda c: jnp.transpose(c.reshape(-1, N_KV_HEADS, HEAD_DIM, WINDOW), (0, 3, 1, 2))[None]
    xs, gn, sk, sv = _mixer_sample(xs, mnorm, win, sinks, vnorm, ws[:, :n_tok, :n_tok].reshape(-1),
                                   bs[:, :n_tok].reshape(-1), anorm, gnorm, wout, to_cols(cache_k[l]),
                                   to_cols(cache_v[l]), n_seq=n_seq, n_tok=n_tok)
    yp, y_sample = _ffn_half(xp, xs, row(ffn2_norm[l]), wg2, wu2, wd2, fnorm, final_norm=True, subtiles=2,
                             sample_out_shape=x_sample.shape)

    y_prompt = yp.reshape(batch, seq, D_MODEL)
    return (y_prompt, y_sample, to_rows(pk), to_rows(pv), to_rows(sk), to_rows(sv), gn[None])
```

```python
import functools

import jax
import jax.numpy as jnp
from jax import lax
from jax.experimental import pallas as pl
from jax.experimental.pallas import tpu as pltpu

F32 = jnp.float32
BF16 = jnp.bfloat16

D_MODEL = 1024
D_FF = 2816
N_HEADS = 8
N_KV_HEADS = 2
GQA_GROUP = 4
HEAD_DIM = 64
ATTN_WIDTH = 512
KV_WIDTH = 128
GMLP_WIDTH = 512
N_GATE_GROUPS = 4
GROUP_CH = 128
WINDOW = 128
CHUNK = 128
RMS_EPS = 1e-6
FFN_RESIDUAL = 0.5
ATTN_SCALE = 0.125
LOG2E = 1.4426950408889634
ALIBI_SLOPES = tuple(2.0 ** (-(i + 1)) for i in range(N_HEADS))
MASK_VALUE = -1e30

LANES = 128
BF16_SUBLANES = 16
FFN_ROWS = 512
FFN_LOAD_STEPS = 8
FFN_CHUNK = 256
MIX_ROWS = 1024
SEQ_PER_STEP = 32
VMEM_LIMIT = 56 * 1024 * 1024


def _rms(x, g):
    ms = jnp.mean(x * x, axis=-1, keepdims=True)
    return x * lax.rsqrt(ms + RMS_EPS) * g


def _dot(a, b):
    return jnp.dot(a, b, preferred_element_type=F32)


def _dot_nt(a, b):
    return lax.dot_general(a, b, (((1,), (1,)), ((), ())), preferred_element_type=F32)


def _resident(shape):
    nd = len(shape)
    return pl.BlockSpec(shape, lambda *_: (0,) * nd, pipeline_mode=pl.Buffered(1))


def _smem():
    return pl.BlockSpec(memory_space=pltpu.SMEM)


def _ffn_body(*refs, final_norm, n_cast, subtiles, n_load_steps):
    xp_ref, xs_ref, norm_ref, wg_in, wu_in, wd_in, fnorm_ref = refs[:7]
    cast_in = refs[7:7 + n_cast]
    op_ref, os_ref = refs[7 + n_cast:9 + n_cast]
    cast_out = refs[9 + n_cast:9 + 2 * n_cast]
    act_ref = refs[9 + 2 * n_cast]
    scratch = list(refs[10 + 2 * n_cast:])
    i = pl.program_id(0)
    first = n_load_steps

    if len(xs_ref.shape) == 3:
        xs_rows = scratch.pop()

        @pl.when(i == first)
        def _flatten_sample_rows():
            xs_rows[...] = xs_ref[...].reshape(FFN_ROWS, D_MODEL)
    else:
        xs_rows = xs_ref

    if n_load_steps:
        wg_ref, wu_ref, wd_ref = scratch

        @pl.when(i < n_load_steps)
        def _load_weights():
            for src, dst in ((wg_in, wg_ref), (wu_in, wu_ref), (wd_in, wd_ref)):
                br = src.shape[0]
                dst[pl.ds(pl.multiple_of(i * br, br), br), :] = src[...].astype(BF16)
    else:
        wg_ref, wu_ref, wd_ref = wg_in, wu_in, wd_in

    for src, dst in zip(cast_in, cast_out):
        dst[...] = src[...].astype(BF16)

    def _sub_tile(k, carry):
        rows = pl.ds(pl.multiple_of(k * FFN_ROWS, FFN_ROWS), FFN_ROWS)
        x = jnp.where(i == first, xs_rows[...], xp_ref[rows, :])
        h = _rms(x, norm_ref[...]).astype(BF16)
        for c in range(D_FF // FFN_CHUNK):
            cols = slice(c * FFN_CHUNK, (c + 1) * FFN_CHUNK)
            a = _dot(h, wg_ref[:, cols])
            b = _dot(h, wu_ref[:, cols])
            act_ref[:, cols] = (a * jax.nn.sigmoid(a) * b).astype(BF16)
        y = x + FFN_RESIDUAL * _dot(act_ref[...], wd_ref[...])
        if final_norm:
            y = _rms(y, fnorm_ref[...])
        op_ref[rows, :] = y
        return carry

    trips = jnp.where(i < first, 0, jnp.where(i == first, 1, subtiles))
    lax.fori_loop(0, trips, _sub_tile, 0)

    @pl.when(i == first)
    def _():
        os_ref[...] = op_ref[0:FFN_ROWS, :].reshape(os_ref.shape)


def _block_rows(rows, n_steps):
    br = BF16_SUBLANES
    while rows % br or rows // br > n_steps:
        br += BF16_SUBLANES
    return br


def _ffn_half(xp, xs, norm, wg, wu, wd, fnorm, *, final_norm, subtiles, to_cast=(), sample_out_shape=None):
    block_rows = subtiles * FFN_ROWS
    n_prompt_tiles = xp.shape[0] // block_rows
    sample_out_shape = sample_out_shape or (FFN_ROWS, D_MODEL)
    assert xp.shape[0] == n_prompt_tiles * block_rows and xs.size == FFN_ROWS * D_MODEL
    n_load_steps = FFN_LOAD_STEPS if wg.dtype == F32 else 0
    first = n_load_steps
    prompt_spec = pl.BlockSpec((block_rows, D_MODEL), lambda i: (jnp.maximum(i - first - 1, 0), 0))
    sample_in_spec = pl.BlockSpec(xs.shape, lambda i: (0,) * xs.ndim)
    sample_out_spec = pl.BlockSpec(sample_out_shape, lambda i: (0,) * len(sample_out_shape))

    def streamed(w, n_steps, start):
        br = _block_rows(w.shape[0], n_steps)
        return pl.BlockSpec((br, w.shape[1]), functools.partial(
            lambda i, last: (jnp.clip(i - start, 0, last), 0), last=w.shape[0] // br - 1))

    cast_specs = [streamed(w, n_prompt_tiles, first) for w in to_cast]
    cast_shapes = [jax.ShapeDtypeStruct(w.shape, BF16) for w in to_cast]
    scratch = [pltpu.VMEM((FFN_ROWS, D_FF), BF16)]
    if n_load_steps:
        weight_specs = [streamed(w, n_load_steps, 0) for w in (wg, wu, wd)]
        scratch += [pltpu.VMEM(w.shape, BF16) for w in (wg, wu, wd)]
    else:
        weight_specs = [_resident(w.shape) for w in (wg, wu, wd)]
    if xs.ndim == 3:
        scratch += [pltpu.VMEM((FFN_ROWS, D_MODEL), F32)]
    return pl.pallas_call(
        functools.partial(_ffn_body, final_norm=final_norm, n_cast=len(to_cast), subtiles=subtiles,
                          n_load_steps=n_load_steps),
        grid=(n_load_steps + 1 + n_prompt_tiles,),
        in_specs=[prompt_spec, sample_in_spec, _resident((1, D_MODEL))] + weight_specs + [_resident((1, D_MODEL))]
        + cast_specs,
        out_specs=[prompt_spec, sample_out_spec] + cast_specs,
        out_shape=[jax.ShapeDtypeStruct(xp.shape, F32), jax.ShapeDtypeStruct(sample_out_shape, F32)] + cast_shapes,
        scratch_shapes=scratch,
        compiler_params=pltpu.CompilerParams(dimension_semantics=("arbitrary",), vmem_limit_bytes=VMEM_LIMIT),
        name="ffn_half_final" if final_norm else "ffn_half",
    )(xp, xs, norm, wg, wu, wd, fnorm, *to_cast)


def _pair_heads(t):
    swapped = pltpu.roll(t, HEAD_DIM, 1)
    left = lax.broadcasted_iota(jnp.int32, t.shape, 1) < HEAD_DIM
    return jnp.where(left, t, swapped), jnp.where(left, swapped, t)


def _gate_norm(g, vnorm_ref, gi):
    gg = g[:, gi * GROUP_CH:(gi + 1) * GROUP_CH]
    return _rms(gg, vnorm_ref[gi:gi + 1, :])


def _merge(x, attn, gate, anorm_ref, gnorm_ref, wout_ref):
    cat = jnp.concatenate([_rms(attn, anorm_ref[...]), _rms(gate, gnorm_ref[...])], axis=1)
    return x + _dot(cat.astype(BF16), wout_ref[...])


def _mixer_prompt_body(x_ref, mnorm_ref, win_ref, sinks_ref, vnorm_ref, ws_ref, bs_ref, anorm_ref, gnorm_ref,
                       wout_ref, o_ref, pk_ref, pv_ref, kv_s, q_s, attn_s, gate_s, bias_s, wtril_s, bcol_s):
    b = pl.program_id(0)
    t = pl.program_id(1)
    nblk = MIX_ROWS // WINDOW
    band = 2 * WINDOW
    qrows = GQA_GROUP * WINDOW

    @pl.when((b == 0) & (t == 0))
    def _init_tables():
        row = lax.broadcasted_iota(jnp.int32, (qrows, band), 0)
        key = lax.broadcasted_iota(jnp.int32, (qrows, band), 1)
        grp = row >> 7
        dist = WINDOW + (row & (WINDOW - 1)) - key
        valid = (dist >= 0) & (dist < WINDOW)
        distf = dist.astype(F32)
        for h in range(N_KV_HEADS):
            s = [ALIBI_SLOPES[h * GQA_GROUP + g] for g in range(GQA_GROUP)]
            slope = jnp.where(grp == 0, s[0], jnp.where(grp == 1, s[1], jnp.where(grp == 2, s[2], s[3])))
            bias = jnp.where(valid, -(slope * distf) * LOG2E, MASK_VALUE)
            bias_s[0, h] = bias
            bias_s[1, h] = jnp.where(key < WINDOW, MASK_VALUE, bias)
        r = lax.broadcasted_iota(jnp.int32, (CHUNK, CHUNK), 0)
        c = lax.broadcasted_iota(jnp.int32, (CHUNK, CHUNK), 1)
        for gi in range(N_GATE_GROUPS):
            wtril_s[gi] = jnp.where(r >= c, ws_ref[gi], 0.0).astype(BF16)
            bcol_s[:, gi:gi + 1] = jnp.sum(jnp.where(r == c, bs_ref[gi:gi + 1, :], 0.0), axis=1, keepdims=True)

    @pl.when(t == 0)
    def _no_previous_block():
        kv_s[0:WINDOW, :] = jnp.zeros((WINDOW, 4 * KV_WIDTH), BF16)

    x = x_ref[...]
    h_in = _rms(x, mnorm_ref[...]).astype(BF16)
    q_s[...] = _dot(h_in, win_ref[:, 0:ATTN_WIDTH]) * (ATTN_SCALE * LOG2E)
    wkv = win_ref[:, ATTN_WIDTH:ATTN_WIDTH + 2 * KV_WIDTH]
    pkv = jnp.concatenate([_dot(h_in[0:MIX_ROWS // 2, :], wkv), _dot(h_in[MIX_ROWS // 2:, :], wkv)], axis=0)
    k_new = pkv[:, 0:KV_WIDTH]
    v_new = pkv[:, KV_WIDTH:2 * KV_WIDTH]
    pk_ref[0] = k_new[MIX_ROWS - WINDOW:, :].T
    pv_ref[0] = v_new[MIX_ROWS - WINDOW:, :].T
    k0, k1 = _pair_heads(k_new)
    v0, v1 = _pair_heads(v_new)
    kv_s[WINDOW:, :] = jnp.concatenate([k0, k1, v0, v1], axis=1).astype(BF16)

    table = jnp.where(t == 0, 1, 0)
    lane_grp = lax.broadcasted_iota(jnp.int32, (WINDOW, 2 * KV_WIDTH), 1) >> 6
    for j in range(nblk):
        rows = slice(j * WINDOW, (j + 1) * WINDOW)
        kvb = kv_s[j * WINDOW:j * WINDOW + band, :]
        for h in range(N_KV_HEADS):
            ka = kvb[:, h * KV_WIDTH:(h + 1) * KV_WIDTH]
            va = kvb[:, (2 + h) * KV_WIDTH:(3 + h) * KV_WIDTH]
            k4 = jnp.concatenate([ka, ka], axis=1)
            v4 = jnp.concatenate([va, va], axis=1)
            q = q_s[rows, h * 2 * KV_WIDTH:(h + 1) * 2 * KV_WIDTH]
            qstack = jnp.concatenate([jnp.where(lane_grp == g, q, 0.0) for g in range(GQA_GROUP)],
                                     axis=0).astype(BF16)
            bias = bias_s[table, h] if j == 0 else bias_s[0, h]
            s = _dot_nt(qstack, k4) + bias
            probs, inv = [], []
            for g in range(GQA_GROUP):
                sg = s[g * WINDOW:(g + 1) * WINDOW, :]
                sink = sinks_ref[h * GQA_GROUP + g] * LOG2E
                m = jnp.maximum(jnp.max(sg, axis=1, keepdims=True), sink)
                p = jnp.exp2(sg - m)
                inv.append(1.0 / (jnp.sum(p, axis=1, keepdims=True) + jnp.exp2(sink - m)))
                probs.append(p.astype(BF16))
            o = _dot(jnp.concatenate(probs, axis=0), v4)
            for g in range(GQA_GROUP):
                lanes = slice(g * HEAD_DIM, (g + 1) * HEAD_DIM)
                out_lanes = slice((h * GQA_GROUP + g) * HEAD_DIM, (h * GQA_GROUP + g + 1) * HEAD_DIM)
                attn_s[rows, out_lanes] = o[g * WINDOW:(g + 1) * WINDOW, lanes] * inv[g]
    kv_s[0:WINDOW, :] = kv_s[MIX_ROWS:MIX_ROWS + WINDOW, :]

    u = jax.nn.gelu(_dot(h_in, win_ref[:, 768:1280]))
    gact = jax.nn.gelu(_dot(h_in, win_ref[:, 1280:1792]))
    for gi in range(N_GATE_GROUPS):
        lanes = slice(gi * GROUP_CH, (gi + 1) * GROUP_CH)
        gn = _gate_norm(gact, vnorm_ref, gi).astype(BF16)
        chunks = jnp.concatenate([gn[c * CHUNK:(c + 1) * CHUNK, :] for c in range(nblk)], axis=1)
        mixed = _dot(wtril_s[gi], chunks) + bcol_s[:, gi:gi + 1]
        for c in range(nblk):
            rows = slice(c * CHUNK, (c + 1) * CHUNK)
            gate_s[rows, lanes] = u[rows, lanes] * mixed[:, c * GROUP_CH:(c + 1) * GROUP_CH]

    o_ref[...] = _merge(x, attn_s[...], gate_s[...], anorm_ref, gnorm_ref, wout_ref)


def _mixer_prompt(x, mnorm, win, sinks, vnorm, ws, bs, anorm, gnorm, wout, *, batch):
    rows = x.shape[0]
    tiles = rows // (batch * MIX_ROWS)
    assert rows == batch * tiles * MIX_ROWS
    band = 2 * WINDOW
    row_spec = pl.BlockSpec((MIX_ROWS, D_MODEL), lambda b, t: (b * tiles + t, 0))
    kv_spec = pl.BlockSpec((1, KV_WIDTH, WINDOW), lambda b, t: (b, 0, 0))
    kv_shape = jax.ShapeDtypeStruct((batch, KV_WIDTH, WINDOW), F32)
    return pl.pallas_call(
        _mixer_prompt_body,
        grid=(batch, tiles),
        in_specs=[row_spec, _resident((1, D_MODEL)), _resident(win.shape), _smem(), _resident(vnorm.shape),
                  _resident(ws.shape), _resident(bs.shape), _resident(anorm.shape), _resident(gnorm.shape),
                  _resident(wout.shape)],
        out_specs=[row_spec, kv_spec, kv_spec],
        out_shape=[jax.ShapeDtypeStruct(x.shape, F32), kv_shape, kv_shape],
        scratch_shapes=[
            pltpu.VMEM((WINDOW + MIX_ROWS, 4 * KV_WIDTH), BF16),
            pltpu.VMEM((MIX_ROWS, ATTN_WIDTH), F32),
            pltpu.VMEM((MIX_ROWS, ATTN_WIDTH), F32),
            pltpu.VMEM((MIX_ROWS, GMLP_WIDTH), F32),
            pltpu.VMEM((2, N_KV_HEADS, GQA_GROUP * WINDOW, band), F32),
            pltpu.VMEM((N_GATE_GROUPS, CHUNK, CHUNK), BF16),
            pltpu.VMEM((CHUNK, N_GATE_GROUPS), F32),
        ],
        compiler_params=pltpu.CompilerParams(dimension_semantics=("arbitrary", "arbitrary"),
                                             vmem_limit_bytes=VMEM_LIMIT),
        name="mixer_prompt",
    )(x, mnorm, win, sinks, vnorm, ws, bs, anorm, gnorm, wout)


def _mixer_sample_body(x_ref, mnorm_ref, win_ref, sinks_ref, vnorm_ref, ws_ref, bs_ref, anorm_ref, gnorm_ref,
                       wout_ref, ck_ref, cv_ref, o_ref, gn_ref, sk_ref, sv_ref,
                       perm_s, qall_s, oall_s, kvt_s, gate_s, bias_s, sink_s, s_s, p_s, *, n_seq, n_tok):
    step = pl.program_id(0)
    n_steps = pl.num_programs(0)
    n_rows = n_seq * n_tok
    q_rows = N_HEADS * n_tok
    keys = 2 * WINDOW
    step_rows = SEQ_PER_STEP * q_rows

    def by_token(val):
        for c in range(val.shape[1] // LANES):
            perm_s[c] = val[:, c * LANES:(c + 1) * LANES]
        return jnp.concatenate(
            [jnp.concatenate([perm_s[c, pl.ds(i, n_seq, stride=n_tok), :] for i in range(n_tok)], axis=0)
             for c in range(val.shape[1] // LANES)], axis=1)

    def by_sequence(val):
        for c in range(val.shape[1] // LANES):
            for i in range(n_tok):
                perm_s[c, pl.ds(i, n_seq, stride=n_tok), :] = val[i * n_seq:(i + 1) * n_seq, c * LANES:(c + 1) * LANES]
        return jnp.concatenate([perm_s[c] for c in range(val.shape[1] // LANES)], axis=1)

    @pl.when(step == 0)
    def _project():
        h = _rms(x_ref[...], mnorm_ref[...])
        pkv = _dot(h.astype(BF16), win_ref[:, ATTN_WIDTH:ATTN_WIDTH + 2 * KV_WIDTH])
        kvt_s[...] = pkv.T
        h_in = by_token(h).astype(BF16)
        pq = _dot(h_in, win_ref[:, 0:ATTN_WIDTH]) * (ATTN_SCALE * LOG2E)
        half = lax.broadcasted_iota(jnp.int32, (n_rows, LANES), 1) >> 6
        for hg in range(N_HEADS):
            kvh = hg // GQA_GROUP
            slab = pq[:, (hg // 2) * LANES:(hg // 2 + 1) * LANES]
            if (hg % 2) != kvh:
                slab = pltpu.roll(slab, HEAD_DIM, 1)
            qall_s[hg * n_rows:(hg + 1) * n_rows, :] = jnp.where(half == kvh, slab, 0.0)

        u = jax.nn.gelu(_dot(h_in, win_ref[:, 768:1280]))
        gact = jax.nn.gelu(_dot(h_in, win_ref[:, 1280:1792]))
        gns = []
        for gi in range(N_GATE_GROUPS):
            lanes = slice(gi * GROUP_CH, (gi + 1) * GROUP_CH)
            gn = _gate_norm(gact, vnorm_ref, gi)
            gns.append(gn)
            for i in range(n_tok):
                rows = slice(i * n_seq, (i + 1) * n_seq)
                w_row = ws_ref[gi, i:i + 1, :]
                mixed = bs_ref[gi:gi + 1, i:i + 1]
                for j in range(i + 1):
                    mixed = mixed + w_row[:, j:j + 1] * gn[j * n_seq:(j + 1) * n_seq, :]
                gate_s[rows, lanes] = u[rows, lanes] * mixed
        gn_ref[...] = by_sequence(jnp.concatenate(gns, axis=1)).reshape(gn_ref.shape)

        row = lax.broadcasted_iota(jnp.int32, (step_rows, keys), 0)
        col = lax.broadcasted_iota(jnp.int32, (step_rows, keys), 1)
        tok = row & (n_tok - 1)
        head = (row >> 2) & (N_HEADS - 1)
        slope = jnp.zeros((step_rows, keys), F32)
        for hg in range(N_HEADS):
            slope = jnp.where(head == hg, ALIBI_SLOPES[hg], slope)
        shifted = col < WINDOW
        pos = jnp.where(shifted, col + n_tok, col - WINDOW)
        dist = WINDOW + tok - pos
        valid = (dist >= 0) & (dist < WINDOW) & (shifted | (pos < n_tok))
        bias_s[...] = jnp.where(valid, -(slope * dist.astype(F32)) * LOG2E, MASK_VALUE)
        head1 = (lax.broadcasted_iota(jnp.int32, (step_rows, 1), 0) >> 2) & (N_HEADS - 1)
        sink = jnp.zeros((step_rows, 1), F32)
        for hg in range(N_HEADS):
            sink = jnp.where(head1 == hg, sinks_ref[hg] * LOG2E, sink)
        sink_s[...] = sink

    col0 = pl.multiple_of(step * LANES, LANES)
    k_cols = kvt_s[0:KV_WIDTH, pl.ds(col0, LANES)]
    v_cols = kvt_s[KV_WIDTH:2 * KV_WIDTH, pl.ds(col0, LANES)]
    is_new = lax.broadcasted_iota(jnp.int32, (KV_WIDTH, WINDOW), 1) >= WINDOW - n_tok
    for bl in range(SEQ_PER_STEP):
        to_tail = (WINDOW - n_tok - n_tok * bl) % LANES
        sk_ref[bl] = jnp.where(is_new, pltpu.roll(k_cols, to_tail, 1), pltpu.roll(ck_ref[bl], WINDOW - n_tok, 1))
        sv_ref[bl] = jnp.where(is_new, pltpu.roll(v_cols, to_tail, 1), pltpu.roll(cv_ref[bl], WINDOW - n_tok, 1))

    seq0 = step * SEQ_PER_STEP
    for bl in range(SEQ_PER_STEP):
        qb = qall_s[pl.ds(seq0 + bl, q_rows, stride=n_seq), :].astype(BF16)
        k2 = jnp.concatenate([sk_ref[bl], ck_ref[bl]], axis=1).astype(BF16)
        s_s[bl * q_rows:(bl + 1) * q_rows, :] = _dot(qb, k2)
    s = s_s[...] + bias_s[...]
    sink = sink_s[...]
    m = jnp.maximum(jnp.max(s, axis=1, keepdims=True), sink)
    p = jnp.exp2(s - m)
    inv = 1.0 / (jnp.sum(p, axis=1, keepdims=True) + jnp.exp2(sink - m))
    p_s[...] = p.astype(BF16)
    for bl in range(SEQ_PER_STEP):
        rows = slice(bl * q_rows, (bl + 1) * q_rows)
        v2 = jnp.concatenate([sv_ref[bl], cv_ref[bl]], axis=1).astype(BF16)
        oall_s[pl.ds(seq0 + bl, q_rows, stride=n_seq), :] = _dot_nt(p_s[rows, :], v2) * inv[rows, :]

    @pl.when(step == n_steps - 1)
    def _merge_rows():
        left = lax.broadcasted_iota(jnp.int32, (n_rows, LANES), 1) < HEAD_DIM
        slabs = []
        for pair in range(N_HEADS // 2):
            halves = []
            for hg in (2 * pair, 2 * pair + 1):
                o = oall_s[hg * n_rows:(hg + 1) * n_rows, :]
                if (hg % 2) != (hg // GQA_GROUP):
                    o = pltpu.roll(o, HEAD_DIM, 1)
                halves.append(o)
            slabs.append(jnp.where(left, halves[0], halves[1]))
        attn = jnp.concatenate(slabs, axis=1)
        o_ref[...] = by_sequence(_merge(by_token(x_ref[...]), attn, gate_s[...], anorm_ref, gnorm_ref, wout_ref))


def _mixer_sample(x, mnorm, win, sinks, vnorm, ws, bs, anorm, gnorm, wout, ck, cv, *, n_seq, n_tok):
    n_rows = n_seq * n_tok
    q_rows = N_HEADS * n_tok
    assert x.shape[0] == n_rows and n_seq % SEQ_PER_STEP == 0 and n_tok == 4 and n_seq == LANES
    assert SEQ_PER_STEP * n_tok == LANES
    cache_spec = pl.BlockSpec((SEQ_PER_STEP, KV_WIDTH, WINDOW), lambda s: (s, 0, 0))
    cache_shape = jax.ShapeDtypeStruct(ck.shape, F32)
    rows_spec = pl.BlockSpec((n_rows, D_MODEL), lambda s: (0, 0))
    gn_spec = pl.BlockSpec((n_seq, n_tok, GMLP_WIDTH), lambda s: (0, 0, 0))
    return pl.pallas_call(
        functools.partial(_mixer_sample_body, n_seq=n_seq, n_tok=n_tok),
        grid=(n_seq // SEQ_PER_STEP,),
        in_specs=[rows_spec, _resident((1, D_MODEL)), _resident(win.shape), _smem(), _resident(vnorm.shape),
                  _resident(ws.shape), _resident(bs.shape), _resident(anorm.shape), _resident(gnorm.shape), _resident(wout.shape),
                  cache_spec, cache_spec],
        out_specs=[rows_spec, gn_spec, cache_spec, cache_spec],
        out_shape=[jax.ShapeDtypeStruct(x.shape, F32), jax.ShapeDtypeStruct((n_seq, n_tok, GMLP_WIDTH), F32),
                   cache_shape, cache_shape],
        scratch_shapes=[
            pltpu.VMEM((D_MODEL // LANES, n_rows, LANES), F32),
            pltpu.VMEM((N_HEADS * n_rows, LANES), F32),
            pltpu.VMEM((N_HEADS * n_rows, LANES), F32),
            pltpu.VMEM((2 * KV_WIDTH, n_rows), F32),
            pltpu.VMEM((n_rows, GMLP_WIDTH), F32),
            pltpu.VMEM((SEQ_PER_STEP * q_rows, 2 * WINDOW), F32),
            pltpu.VMEM((SEQ_PER_STEP * q_rows, 1), F32),
            pltpu.VMEM((SEQ_PER_STEP * q_rows, 2 * WINDOW), F32),
            pltpu.VMEM((SEQ_PER_STEP * q_rows, 2 * WINDOW), BF16),
        ],
        compiler_params=pltpu.CompilerParams(dimension_semantics=("arbitrary",), vmem_limit_bytes=VMEM_LIMIT),
        name="mixer_sample",
    )(x, mnorm, win, sinks, vnorm, ws, bs, anorm, gnorm, wout, ck, cv)


def kernel(x_prompt, x_sample, cache_k, cache_v, ffn1_norm, ffn1_w_gate, ffn1_w_up, ffn1_w_down, mix_norm, w_in,
           attn_sinks, gmlp_v_norm, gmlp_w_spatial, gmlp_b_spatial, attn_out_norm, gmlp_out_norm, w_out, ffn2_norm,
           ffn2_w_gate, ffn2_w_up, ffn2_w_down, final_norm):
    batch, seq, _ = x_prompt.shape
    n_seq, n_tok, _ = x_sample.shape
    depth = cache_k.shape[0]
    assert depth == 1

    xp = x_prompt.reshape(batch * seq, D_MODEL)
    row = lambda v: v.reshape(1, -1)
    fnorm = row(final_norm)
    l = 0
    sinks = attn_sinks[l]
    vnorm = gmlp_v_norm[l]
    ws = gmlp_w_spatial[l]
    bs = gmlp_b_spatial[l]
    mnorm, anorm, gnorm = row(mix_norm[l]), row(attn_out_norm[l]), row(gmlp_out_norm[l])

    later = (ffn2_w_gate[l], ffn2_w_up[l], ffn2_w_down[l], w_in[l], w_out[l])
    xp, xs, wg2, wu2, wd2, win, wout = _ffn_half(xp, x_sample, row(ffn1_norm[l]), ffn1_w_gate[l], ffn1_w_up[l],
                                                 ffn1_w_down[l], fnorm, final_norm=False, subtiles=1, to_cast=later)
    xp, pk, pv = _mixer_prompt(xp, mnorm, win, sinks, vnorm, ws, bs, anorm, gnorm, wout, batch=batch)
    to_cols = lambda c: jnp.transpose(c, (0, 2, 3, 1)).reshape(n_seq, KV_WIDTH, WINDOW)
    to_rows = lambda c: jnp.transpose(c.reshape(-1, N_KV_HEADS, HEAD_DIM, WINDOW), (0, 3, 1, 2))[None]
    xs, gn, sk, sv = _mixer_sample(xs, mnorm, win, sinks, vnorm, ws, bs, anorm, gnorm, wout, to_cols(cache_k[l]),
                                   to_cols(cache_v[l]), n_seq=n_seq, n_tok=n_tok)
    yp, y_sample = _ffn_half(xp, xs, row(ffn2_norm[l]), wg2, wu2, wd2, fnorm, final_norm=True, subtiles=2,
                             sample_out_shape=x_sample.shape)

    y_prompt = yp.reshape(batch, seq, D_MODEL)
    return (y_prompt, y_sample, to_rows(pk), to_rows(pv), to_rows(sk), to_rows(sv), gn[None])
```

```python
import functools

import jax
import jax.numpy as jnp
from jax import lax
from jax.experimental import pallas as pl
from jax.experimental.pallas import tpu as pltpu

F32 = jnp.float32
BF16 = jnp.bfloat16

D_MODEL = 1024
D_FF = 2816
N_HEADS = 8
N_KV_HEADS = 2
GQA_GROUP = 4
HEAD_DIM = 64
ATTN_WIDTH = 512
KV_WIDTH = 128
GMLP_WIDTH = 512
N_GATE_GROUPS = 4
GROUP_CH = 128
WINDOW = 128
CHUNK = 128
RMS_EPS = 1e-6
FFN_RESIDUAL = 0.5
ATTN_SCALE = 0.125
LOG2E = 1.4426950408889634
ALIBI_SLOPES = tuple(2.0 ** (-(i + 1)) for i in range(N_HEADS))
MASK_VALUE = -1e30

LANES = 128
BF16_SUBLANES = 16
FFN_ROWS = 512
FFN_CHUNK = 256
MIX_ROWS = 1024
SEQ_PER_STEP = 32
VMEM_LIMIT = 56 * 1024 * 1024


def _rms(x, g):
    ms = jnp.mean(x * x, axis=-1, keepdims=True)
    return x * lax.rsqrt(ms + RMS_EPS) * g


def _dot(a, b):
    return jnp.dot(a, b, preferred_element_type=F32)


def _dot_nt(a, b):
    return lax.dot_general(a, b, (((1,), (1,)), ((), ())), preferred_element_type=F32)


def _resident(shape):
    nd = len(shape)
    return pl.BlockSpec(shape, lambda *_: (0,) * nd, pipeline_mode=pl.Buffered(1))


def _smem():
    return pl.BlockSpec(memory_space=pltpu.SMEM)


def _ffn_body(*refs, final_norm, n_cast, subtiles):
    xp_ref, xs_ref, norm_ref, wg_in, wu_in, wd_in, fnorm_ref = refs[:7]
    cast_in = refs[7:7 + n_cast]
    op_ref, os_ref = refs[7 + n_cast:9 + n_cast]
    cast_out = refs[9 + n_cast:9 + 2 * n_cast]
    act_ref, wg_ref, wu_ref, wd_ref, hs_ref, xs_rows = refs[9 + 2 * n_cast:]
    i = pl.program_id(0)
    n_chunks = D_FF // FFN_CHUNK

    @pl.when(i == 0)
    def _sample_input():
        xs_rows[...] = xs_ref[...].reshape(FFN_ROWS, D_MODEL)
        hs_ref[...] = _rms(xs_rows[...], norm_ref[...]).astype(BF16)

    @pl.when(i < n_chunks)
    def _weight_chunk():
        cols = pl.ds(pl.multiple_of(i * FFN_CHUNK, FFN_CHUNK), FFN_CHUNK)
        wg_c = wg_in[...].astype(BF16)
        wu_c = wu_in[...].astype(BF16)
        wg_ref[:, cols] = wg_c
        wu_ref[:, cols] = wu_c
        wd_ref[cols, :] = wd_in[...].astype(BF16)
        a = _dot(hs_ref[...], wg_c)
        b = _dot(hs_ref[...], wu_c)
        act_ref[:, cols] = (a * jax.nn.sigmoid(a) * b).astype(BF16)

    @pl.when(i == n_chunks)
    def _finish_sample_tile():
        y = xs_rows[...] + FFN_RESIDUAL * _dot(act_ref[...], wd_ref[...])
        if final_norm:
            y = _rms(y, fnorm_ref[...])
        os_ref[...] = y.reshape(os_ref.shape)

    for src, dst in zip(cast_in, cast_out):
        dst[...] = src[...].astype(BF16)

    def _sub_tile(k, carry):
        rows = pl.ds(pl.multiple_of(k * FFN_ROWS, FFN_ROWS), FFN_ROWS)
        x = xp_ref[rows, :]
        h = _rms(x, norm_ref[...]).astype(BF16)
        for c in range(n_chunks):
            cols = slice(c * FFN_CHUNK, (c + 1) * FFN_CHUNK)
            a = _dot(h, wg_ref[:, cols])
            b = _dot(h, wu_ref[:, cols])
            act_ref[:, cols] = (a * jax.nn.sigmoid(a) * b).astype(BF16)
        y = x + FFN_RESIDUAL * _dot(act_ref[...], wd_ref[...])
        if final_norm:
            y = _rms(y, fnorm_ref[...])
        op_ref[rows, :] = y
        return carry

    lax.fori_loop(0, jnp.where(i <= n_chunks, 0, subtiles), _sub_tile, 0)


def _block_rows(rows, n_steps):
    br = BF16_SUBLANES
    while rows % br or rows // br > n_steps:
        br += BF16_SUBLANES
    return br


def _ffn_half(xp, xs, norm, wg, wu, wd, fnorm, *, final_norm, subtiles, to_cast=(), sample_out_shape=None):
    block_rows = subtiles * FFN_ROWS
    n_prompt_tiles = xp.shape[0] // block_rows
    n_chunks = D_FF // FFN_CHUNK
    first = n_chunks + 1
    sample_out_shape = sample_out_shape or (FFN_ROWS, D_MODEL)
    assert xp.shape[0] == n_prompt_tiles * block_rows and xs.size == FFN_ROWS * D_MODEL
    prompt_spec = pl.BlockSpec((block_rows, D_MODEL), lambda i: (jnp.maximum(i - first, 0), 0))
    sample_in_spec = pl.BlockSpec(xs.shape, lambda i: (0,) * xs.ndim)
    sample_out_spec = pl.BlockSpec(sample_out_shape, lambda i: (0,) * len(sample_out_shape))
    up_chunk_spec = pl.BlockSpec((D_MODEL, FFN_CHUNK), lambda i: (0, jnp.minimum(i, n_chunks - 1)))
    down_chunk_spec = pl.BlockSpec((FFN_CHUNK, D_MODEL), lambda i: (jnp.minimum(i, n_chunks - 1), 0))

    def row_streamed(w):
        br = _block_rows(w.shape[0], n_prompt_tiles)
        return pl.BlockSpec((br, w.shape[1]), functools.partial(
            lambda i, last: (jnp.clip(i - first, 0, last), 0), last=w.shape[0] // br - 1))

    cast_specs = [row_streamed(w) for w in to_cast]
    cast_shapes = [jax.ShapeDtypeStruct(w.shape, BF16) for w in to_cast]
    return pl.pallas_call(
        functools.partial(_ffn_body, final_norm=final_norm, n_cast=len(to_cast), subtiles=subtiles),
        grid=(first + n_prompt_tiles,),
        in_specs=[prompt_spec, sample_in_spec, _resident((1, D_MODEL)), up_chunk_spec, up_chunk_spec, down_chunk_spec,
                  _resident((1, D_MODEL))] + cast_specs,
        out_specs=[prompt_spec, sample_out_spec] + cast_specs,
        out_shape=[jax.ShapeDtypeStruct(xp.shape, F32), jax.ShapeDtypeStruct(sample_out_shape, F32)] + cast_shapes,
        scratch_shapes=[
            pltpu.VMEM((FFN_ROWS, D_FF), BF16),
            pltpu.VMEM((D_MODEL, D_FF), BF16),
            pltpu.VMEM((D_MODEL, D_FF), BF16),
            pltpu.VMEM((D_FF, D_MODEL), BF16),
            pltpu.VMEM((FFN_ROWS, D_MODEL), BF16),
            pltpu.VMEM((FFN_ROWS, D_MODEL), F32),
        ],
        compiler_params=pltpu.CompilerParams(dimension_semantics=("arbitrary",), vmem_limit_bytes=VMEM_LIMIT),
        name="ffn_half_final" if final_norm else "ffn_half",
    )(xp, xs, norm, wg, wu, wd, fnorm, *to_cast)


def _pair_heads(t):
    swapped = pltpu.roll(t, HEAD_DIM, 1)
    left = lax.broadcasted_iota(jnp.int32, t.shape, 1) < HEAD_DIM
    return jnp.where(left, t, swapped), jnp.where(left, swapped, t)


def _gate_norm(g, vnorm_ref, gi):
    gg = g[:, gi * GROUP_CH:(gi + 1) * GROUP_CH]
    return _rms(gg, vnorm_ref[gi:gi + 1, :])


def _merge(x, attn, gate, anorm_ref, gnorm_ref, wout_ref):
    cat = jnp.concatenate([_rms(attn, anorm_ref[...]), _rms(gate, gnorm_ref[...])], axis=1)
    return x + _dot(cat.astype(BF16), wout_ref[...])


def _mixer_prompt_body(x_ref, mnorm_ref, win_ref, sinks_ref, vnorm_ref, ws_ref, bs_ref, anorm_ref, gnorm_ref,
                       wout_ref, o_ref, pk_ref, pv_ref, kv_s, q_s, attn_s, gate_s, bias_s, wtril_s, bcol_s):
    b = pl.program_id(0)
    t = pl.program_id(1)
    nblk = MIX_ROWS // WINDOW
    band = 2 * WINDOW
    qrows = GQA_GROUP * WINDOW

    @pl.when((b == 0) & (t == 0))
    def _init_tables():
        row = lax.broadcasted_iota(jnp.int32, (qrows, band), 0)
        key = lax.broadcasted_iota(jnp.int32, (qrows, band), 1)
        grp = row >> 7
        dist = WINDOW + (row & (WINDOW - 1)) - key
        valid = (dist >= 0) & (dist < WINDOW)
        distf = dist.astype(F32)
        for h in range(N_KV_HEADS):
            s = [ALIBI_SLOPES[h * GQA_GROUP + g] for g in range(GQA_GROUP)]
            slope = jnp.where(grp == 0, s[0], jnp.where(grp == 1, s[1], jnp.where(grp == 2, s[2], s[3])))
            bias = jnp.where(valid, -(slope * distf) * LOG2E, MASK_VALUE)
            bias_s[0, h] = bias
            bias_s[1, h] = jnp.where(key < WINDOW, MASK_VALUE, bias)
        r = lax.broadcasted_iota(jnp.int32, (CHUNK, CHUNK), 0)
        c = lax.broadcasted_iota(jnp.int32, (CHUNK, CHUNK), 1)
        for gi in range(N_GATE_GROUPS):
            wtril_s[gi] = jnp.where(r >= c, ws_ref[gi], 0.0).astype(BF16)
            bcol_s[:, gi:gi + 1] = jnp.sum(jnp.where(r == c, bs_ref[gi:gi + 1, :], 0.0), axis=1, keepdims=True)

    @pl.when(t == 0)
    def _no_previous_block():
        kv_s[0:WINDOW, :] = jnp.zeros((WINDOW, 4 * KV_WIDTH), BF16)

    x = x_ref[...]
    h_in = _rms(x, mnorm_ref[...]).astype(BF16)
    q_s[...] = _dot(h_in, win_ref[:, 0:ATTN_WIDTH]) * (ATTN_SCALE * LOG2E)
    wkv = win_ref[:, ATTN_WIDTH:ATTN_WIDTH + 2 * KV_WIDTH]
    pkv = jnp.concatenate([_dot(h_in[0:MIX_ROWS // 2, :], wkv), _dot(h_in[MIX_ROWS // 2:, :], wkv)], axis=0)
    k_new = pkv[:, 0:KV_WIDTH]
    v_new = pkv[:, KV_WIDTH:2 * KV_WIDTH]
    pk_ref[0] = k_new[MIX_ROWS - WINDOW:, :].T
    pv_ref[0] = v_new[MIX_ROWS - WINDOW:, :].T
    k0, k1 = _pair_heads(k_new)
    v0, v1 = _pair_heads(v_new)
    kv_s[WINDOW:, :] = jnp.concatenate([k0, k1, v0, v1], axis=1).astype(BF16)

    table = jnp.where(t == 0, 1, 0)
    lane_grp = lax.broadcasted_iota(jnp.int32, (WINDOW, 2 * KV_WIDTH), 1) >> 6
    for j in range(nblk):
        rows = slice(j * WINDOW, (j + 1) * WINDOW)
        kvb = kv_s[j * WINDOW:j * WINDOW + band, :]
        for h in range(N_KV_HEADS):
            ka = kvb[:, h * KV_WIDTH:(h + 1) * KV_WIDTH]
            va = kvb[:, (2 + h) * KV_WIDTH:(3 + h) * KV_WIDTH]
            k4 = jnp.concatenate([ka, ka], axis=1)
            v4 = jnp.concatenate([va, va], axis=1)
            q = q_s[rows, h * 2 * KV_WIDTH:(h + 1) * 2 * KV_WIDTH]
            qstack = jnp.concatenate([jnp.where(lane_grp == g, q, 0.0) for g in range(GQA_GROUP)],
                                     axis=0).astype(BF16)
            bias = bias_s[table, h] if j == 0 else bias_s[0, h]
            s = _dot_nt(qstack, k4) + bias
            probs, inv = [], []
            for g in range(GQA_GROUP):
                sg = s[g * WINDOW:(g + 1) * WINDOW, :]
                sink = sinks_ref[h * GQA_GROUP + g] * LOG2E
                m = jnp.maximum(jnp.max(sg, axis=1, keepdims=True), sink)
                p = jnp.exp2(sg - m)
                inv.append(1.0 / (jnp.sum(p, axis=1, keepdims=True) + jnp.exp2(sink - m)))
                probs.append(p.astype(BF16))
            o = _dot(jnp.concatenate(probs, axis=0), v4)
            for g in range(GQA_GROUP):
                lanes = slice(g * HEAD_DIM, (g + 1) * HEAD_DIM)
                out_lanes = slice((h * GQA_GROUP + g) * HEAD_DIM, (h * GQA_GROUP + g + 1) * HEAD_DIM)
                attn_s[rows, out_lanes] = o[g * WINDOW:(g + 1) * WINDOW, lanes] * inv[g]
    kv_s[0:WINDOW, :] = kv_s[MIX_ROWS:MIX_ROWS + WINDOW, :]

    u = jax.nn.gelu(_dot(h_in, win_ref[:, 768:1280]))
    gact = jax.nn.gelu(_dot(h_in, win_ref[:, 1280:1792]))
    for gi in range(N_GATE_GROUPS):
        lanes = slice(gi * GROUP_CH, (gi + 1) * GROUP_CH)
        gn = _gate_norm(gact, vnorm_ref, gi).astype(BF16)
        chunks = jnp.concatenate([gn[c * CHUNK:(c + 1) * CHUNK, :] for c in range(nblk)], axis=1)
        mixed = _dot(wtril_s[gi], chunks) + bcol_s[:, gi:gi + 1]
        for c in range(nblk):
            rows = slice(c * CHUNK, (c + 1) * CHUNK)
            gate_s[rows, lanes] = u[rows, lanes] * mixed[:, c * GROUP_CH:(c + 1) * GROUP_CH]

    o_ref[...] = _merge(x, attn_s[...], gate_s[...], anorm_ref, gnorm_ref, wout_ref)


def _mixer_prompt(x, mnorm, win, sinks, vnorm, ws, bs, anorm, gnorm, wout, *, batch):
    rows = x.shape[0]
    tiles = rows // (batch * MIX_ROWS)
    assert rows == batch * tiles * MIX_ROWS
    band = 2 * WINDOW
    row_spec = pl.BlockSpec((MIX_ROWS, D_MODEL), lambda b, t: (b * tiles + t, 0))
    kv_spec = pl.BlockSpec((1, KV_WIDTH, WINDOW), lambda b, t: (b, 0, 0))
    kv_shape = jax.ShapeDtypeStruct((batch, KV_WIDTH, WINDOW), F32)
    return pl.pallas_call(
        _mixer_prompt_body,
        grid=(batch, tiles),
        in_specs=[row_spec, _resident((1, D_MODEL)), _resident(win.shape), _smem(), _resident(vnorm.shape),
                  _resident(ws.shape), _resident(bs.shape), _resident(anorm.shape), _resident(gnorm.shape),
                  _resident(wout.shape)],
        out_specs=[row_spec, kv_spec, kv_spec],
        out_shape=[jax.ShapeDtypeStruct(x.shape, F32), kv_shape, kv_shape],
        scratch_shapes=[
            pltpu.VMEM((WINDOW + MIX_ROWS, 4 * KV_WIDTH), BF16),
            pltpu.VMEM((MIX_ROWS, ATTN_WIDTH), F32),
            pltpu.VMEM((MIX_ROWS, ATTN_WIDTH), F32),
            pltpu.VMEM((MIX_ROWS, GMLP_WIDTH), F32),
            pltpu.VMEM((2, N_KV_HEADS, GQA_GROUP * WINDOW, band), F32),
            pltpu.VMEM((N_GATE_GROUPS, CHUNK, CHUNK), BF16),
            pltpu.VMEM((CHUNK, N_GATE_GROUPS), F32),
        ],
        compiler_params=pltpu.CompilerParams(dimension_semantics=("arbitrary", "arbitrary"),
                                             vmem_limit_bytes=VMEM_LIMIT),
        name="mixer_prompt",
    )(x, mnorm, win, sinks, vnorm, ws, bs, anorm, gnorm, wout)


def _mixer_sample_body(x_ref, mnorm_ref, win_ref, sinks_ref, vnorm_ref, ws_ref, bs_ref, anorm_ref, gnorm_ref,
                       wout_ref, ck_ref, cv_ref, o_ref, gn_ref, sk_ref, sv_ref,
                       perm_s, qall_s, oall_s, kvt_s, gate_s, bias_s, sink_s, s_s, p_s, *, n_seq, n_tok):
    step = pl.program_id(0)
    n_steps = pl.num_programs(0)
    n_rows = n_seq * n_tok
    q_rows = N_HEADS * n_tok
    keys = 2 * WINDOW
    step_rows = SEQ_PER_STEP * q_rows

    def by_token(val):
        for c in range(val.shape[1] // LANES):
            perm_s[c] = val[:, c * LANES:(c + 1) * LANES]
        return jnp.concatenate(
            [jnp.concatenate([perm_s[c, pl.ds(i, n_seq, stride=n_tok), :] for i in range(n_tok)], axis=0)
             for c in range(val.shape[1] // LANES)], axis=1)

    def by_sequence(val):
        for c in range(val.shape[1] // LANES):
            for i in range(n_tok):
                perm_s[c, pl.ds(i, n_seq, stride=n_tok), :] = val[i * n_seq:(i + 1) * n_seq, c * LANES:(c + 1) * LANES]
        return jnp.concatenate([perm_s[c] for c in range(val.shape[1] // LANES)], axis=1)

    @pl.when(step == 0)
    def _project():
        h = _rms(x_ref[...], mnorm_ref[...])
        pkv = _dot(h.astype(BF16), win_ref[:, ATTN_WIDTH:ATTN_WIDTH + 2 * KV_WIDTH])
        kvt_s[...] = pkv.T
        h_in = by_token(h).astype(BF16)
        pq = _dot(h_in, win_ref[:, 0:ATTN_WIDTH]) * (ATTN_SCALE * LOG2E)
        half = lax.broadcasted_iota(jnp.int32, (n_rows, LANES), 1) >> 6
        for hg in range(N_HEADS):
            kvh = hg // GQA_GROUP
            slab = pq[:, (hg // 2) * LANES:(hg // 2 + 1) * LANES]
            if (hg % 2) != kvh:
                slab = pltpu.roll(slab, HEAD_DIM, 1)
            qall_s[hg * n_rows:(hg + 1) * n_rows, :] = jnp.where(half == kvh, slab, 0.0)

        u = jax.nn.gelu(_dot(h_in, win_ref[:, 768:1280]))
        gact = jax.nn.gelu(_dot(h_in, win_ref[:, 1280:1792]))
        gns = []
        for gi in range(N_GATE_GROUPS):
            lanes = slice(gi * GROUP_CH, (gi + 1) * GROUP_CH)
            gn = _gate_norm(gact, vnorm_ref, gi)
            gns.append(gn)
            for i in range(n_tok):
                rows = slice(i * n_seq, (i + 1) * n_seq)
                w_row = ws_ref[gi, i:i + 1, :]
                mixed = bs_ref[gi:gi + 1, i:i + 1]
                for j in range(i + 1):
                    mixed = mixed + w_row[:, j:j + 1] * gn[j * n_seq:(j + 1) * n_seq, :]
                gate_s[rows, lanes] = u[rows, lanes] * mixed
        gn_ref[...] = by_sequence(jnp.concatenate(gns, axis=1)).reshape(gn_ref.shape)

        row = lax.broadcasted_iota(jnp.int32, (step_rows, keys), 0)
        col = lax.broadcasted_iota(jnp.int32, (step_rows, keys), 1)
        tok = row & (n_tok - 1)
        head = (row >> 2) & (N_HEADS - 1)
        slope = jnp.zeros((step_rows, keys), F32)
        for hg in range(N_HEADS):
            slope = jnp.where(head == hg, ALIBI_SLOPES[hg], slope)
        shifted = col < WINDOW
        pos = jnp.where(shifted, col + n_tok, col - WINDOW)
        dist = WINDOW + tok - pos
        valid = (dist >= 0) & (dist < WINDOW) & (shifted | (pos < n_tok))
        bias_s[...] = jnp.where(valid, -(slope * dist.astype(F32)) * LOG2E, MASK_VALUE)
        head1 = (lax.broadcasted_iota(jnp.int32, (step_rows, 1), 0) >> 2) & (N_HEADS - 1)
        sink = jnp.zeros((step_rows, 1), F32)
        for hg in range(N_HEADS):
            sink = jnp.where(head1 == hg, sinks_ref[hg] * LOG2E, sink)
        sink_s[...] = sink

    col0 = pl.multiple_of(step * LANES, LANES)
    k_cols = kvt_s[0:KV_WIDTH, pl.ds(col0, LANES)]
    v_cols = kvt_s[KV_WIDTH:2 * KV_WIDTH, pl.ds(col0, LANES)]
    is_new = lax.broadcasted_iota(jnp.int32, (KV_WIDTH, WINDOW), 1) >= WINDOW - n_tok
    for bl in range(SEQ_PER_STEP):
        to_tail = (WINDOW - n_tok - n_tok * bl) % LANES
        sk_ref[bl] = jnp.where(is_new, pltpu.roll(k_cols, to_tail, 1), pltpu.roll(ck_ref[bl], WINDOW - n_tok, 1))
        sv_ref[bl] = jnp.where(is_new, pltpu.roll(v_cols, to_tail, 1), pltpu.roll(cv_ref[bl], WINDOW - n_tok, 1))

    seq0 = step * SEQ_PER_STEP
    for bl in range(SEQ_PER_STEP):
        qb = qall_s[pl.ds(seq0 + bl, q_rows, stride=n_seq), :].astype(BF16)
        k2 = jnp.concatenate([sk_ref[bl], ck_ref[bl]], axis=1).astype(BF16)
        s_s[bl * q_rows:(bl + 1) * q_rows, :] = _dot(qb, k2)
    s = s_s[...] + bias_s[...]
    sink = sink_s[...]
    m = jnp.maximum(jnp.max(s, axis=1, keepdims=True), sink)
    p = jnp.exp2(s - m)
    inv = 1.0 / (jnp.sum(p, axis=1, keepdims=True) + jnp.exp2(sink - m))
    p_s[...] = p.astype(BF16)
    for bl in range(SEQ_PER_STEP):
        rows = slice(bl * q_rows, (bl + 1) * q_rows)
        v2 = jnp.concatenate([sv_ref[bl], cv_ref[bl]], axis=1).astype(BF16)
        oall_s[pl.ds(seq0 + bl, q_rows, stride=n_seq), :] = _dot_nt(p_s[rows, :], v2) * inv[rows, :]

    @pl.when(step == n_steps - 1)
    def _merge_rows():
        left = lax.broadcasted_iota(jnp.int32, (n_rows, LANES), 1) < HEAD_DIM
        slabs = []
        for pair in range(N_HEADS // 2):
            halves = []
            for hg in (2 * pair, 2 * pair + 1):
                o = oall_s[hg * n_rows:(hg + 1) * n_rows, :]
                if (hg % 2) != (hg // GQA_GROUP):
                    o = pltpu.roll(o, HEAD_DIM, 1)
                halves.append(o)
            slabs.append(jnp.where(left, halves[0], halves[1]))
        attn = jnp.concatenate(slabs, axis=1)
        o_ref[...] = by_sequence(_merge(by_token(x_ref[...]), attn, gate_s[...], anorm_ref, gnorm_ref, wout_ref))


def _mixer_sample(x, mnorm, win, sinks, vnorm, ws, bs, anorm, gnorm, wout, ck, cv, *, n_seq, n_tok):
    n_rows = n_seq * n_tok
    q_rows = N_HEADS * n_tok
    assert x.shape[0] == n_rows and n_seq % SEQ_PER_STEP == 0 and n_tok == 4 and n_seq == LANES
    assert SEQ_PER_STEP * n_tok == LANES
    cache_spec = pl.BlockSpec((SEQ_PER_STEP, KV_WIDTH, WINDOW), lambda s: (s, 0, 0))
    cache_shape = jax.ShapeDtypeStruct(ck.shape, F32)
    rows_spec = pl.BlockSpec((n_rows, D_MODEL), lambda s: (0, 0))
    gn_spec = pl.BlockSpec((n_seq, n_tok, GMLP_WIDTH), lambda s: (0, 0, 0))
    return pl.pallas_call(
        functools.partial(_mixer_sample_body, n_seq=n_seq, n_tok=n_tok),
        grid=(n_seq // SEQ_PER_STEP,),
        in_specs=[rows_spec, _resident((1, D_MODEL)), _resident(win.shape), _smem(), _resident(vnorm.shape),
                  _resident(ws.shape), _resident(bs.shape), _resident(anorm.shape), _resident(gnorm.shape), _resident(wout.shape),
                  cache_spec, cache_spec],
        out_specs=[rows_spec, gn_spec, cache_spec, cache_spec],
        out_shape=[jax.ShapeDtypeStruct(x.shape, F32), jax.ShapeDtypeStruct((n_seq, n_tok, GMLP_WIDTH), F32),
                   cache_shape, cache_shape],
        scratch_shapes=[
            pltpu.VMEM((D_MODEL // LANES, n_rows, LANES), F32),
            pltpu.VMEM((N_HEADS * n_rows, LANES), F32),
            pltpu.VMEM((N_HEADS * n_rows, LANES), F32),
            pltpu.VMEM((2 * KV_WIDTH, n_rows), F32),
            pltpu.VMEM((n_rows, GMLP_WIDTH), F32),
            pltpu.VMEM((SEQ_PER_STEP * q_rows, 2 * WINDOW), F32),
            pltpu.VMEM((SEQ_PER_STEP * q_rows, 1), F32),
            pltpu.VMEM((SEQ_PER_STEP * q_rows, 2 * WINDOW), F32),
            pltpu.VMEM((SEQ_PER_STEP * q_rows, 2 * WINDOW), BF16),
        ],
        compiler_params=pltpu.CompilerParams(dimension_semantics=("arbitrary",), vmem_limit_bytes=VMEM_LIMIT),
        name="mixer_sample",
    )(x, mnorm, win, sinks, vnorm, ws, bs, anorm, gnorm, wout, ck, cv)


def kernel(x_prompt, x_sample, cache_k, cache_v, ffn1_norm, ffn1_w_gate, ffn1_w_up, ffn1_w_down, mix_norm, w_in,
           attn_sinks, gmlp_v_norm, gmlp_w_spatial, gmlp_b_spatial, attn_out_norm, gmlp_out_norm, w_out, ffn2_norm,
           ffn2_w_gate, ffn2_w_up, ffn2_w_down, final_norm):
    batch, seq, _ = x_prompt.shape
    n_seq, n_tok, _ = x_sample.shape
    depth = cache_k.shape[0]
    assert depth == 1

    xp = x_prompt.reshape(batch * seq, D_MODEL)
    row = lambda v: v.reshape(1, -1)
    fnorm = row(final_norm)
    l = 0
    sinks = attn_sinks[l]
    vnorm = gmlp_v_norm[l]
    ws = gmlp_w_spatial[l]
    bs = gmlp_b_spatial[l]
    mnorm, anorm, gnorm = row(mix_norm[l]), row(attn_out_norm[l]), row(gmlp_out_norm[l])

    later = (ffn2_w_gate[l], ffn2_w_up[l], ffn2_w_down[l], w_in[l], w_out[l])
    xp, xs, wg2, wu2, wd2, win, wout = _ffn_half(xp, x_sample, row(ffn1_norm[l]), ffn1_w_gate[l], ffn1_w_up[l],
                                                 ffn1_w_down[l], fnorm, final_norm=False, subtiles=1, to_cast=later)
    xp, pk, pv = _mixer_prompt(xp, mnorm, win, sinks, vnorm, ws, bs, anorm, gnorm, wout, batch=batch)
    to_cols = lambda c: jnp.transpose(c, (0, 2, 3, 1)).reshape(n_seq, KV_WIDTH, WINDOW)
    to_rows = lambda c: jnp.transpose(c.reshape(-1, N_KV_HEADS, HEAD_DIM, WINDOW), (0, 3, 1, 2))[None]
    xs, gn, sk, sv = _mixer_sample(xs, mnorm, win, sinks, vnorm, ws, bs, anorm, gnorm, wout, to_cols(cache_k[l]),
                                   to_cols(cache_v[l]), n_seq=n_seq, n_tok=n_tok)
    yp, y_sample = _ffn_half(xp, xs, row(ffn2_norm[l]), wg2, wu2, wd2, fnorm, final_norm=True, subtiles=2,
                             sample_out_shape=x_sample.shape)

    y_prompt = yp.reshape(batch, seq, D_MODEL)
    return (y_prompt, y_sample, to_rows(pk), to_rows(pv), to_rows(sk), to_rows(sv), gn[None])
```

```python
import functools

import jax
import jax.numpy as jnp
from jax import lax
from jax.experimental import pallas as pl
from jax.experimental.pallas import tpu as pltpu

F32 = jnp.float32
BF16 = jnp.bfloat16

D_MODEL = 1024
D_FF = 2816
N_HEADS = 8
N_KV_HEADS = 2
GQA_GROUP = 4
HEAD_DIM = 64
ATTN_WIDTH = 512
KV_WIDTH = 128
GMLP_WIDTH = 512
N_GATE_GROUPS = 4
GROUP_CH = 128
U_COL = ATTN_WIDTH + 2 * KV_WIDTH
G_COL = U_COL + GMLP_WIDTH
WINDOW = 128
CHUNK = 128
WINDOW_BITS = 7
HEAD_DIM_BITS = 6
RMS_EPS = 1e-6
FFN_RESIDUAL = 0.5
ATTN_SCALE = 0.125
LOG2E = 1.4426950408889634
ALIBI_SLOPES = tuple(2.0 ** (-(i + 1)) for i in range(N_HEADS))
MASK_VALUE = -1e30

LANES = 128
BF16_SUBLANES = 16
FFN_ROWS = 512
FFN_CHUNK = 256
MIX_ROWS = 1024
SEQ_PER_STEP = 32
VMEM_LIMIT = 56 * 1024 * 1024


def _rms(x, g):
    ms = jnp.mean(x * x, axis=-1, keepdims=True)
    return x * lax.rsqrt(ms + RMS_EPS) * g


def _dot(a, b):
    return jnp.dot(a, b, preferred_element_type=F32)


def _dot_nt(a, b):
    return lax.dot_general(a, b, (((1,), (1,)), ((), ())), preferred_element_type=F32)


def _resident(shape):
    nd = len(shape)
    return pl.BlockSpec(shape, lambda *_: (0,) * nd, pipeline_mode=pl.Buffered(1))


def _smem():
    return pl.BlockSpec(memory_space=pltpu.SMEM)


def _ffn_body(*refs, final_norm, n_cast, subtiles):
    xp_ref, xs_ref, norm_ref, wg_in, wu_in, wd_in, fnorm_ref = refs[:7]
    cast_in = refs[7:7 + n_cast]
    op_ref, os_ref = refs[7 + n_cast:9 + n_cast]
    cast_out = refs[9 + n_cast:9 + 2 * n_cast]
    act_ref, wg_ref, wu_ref, wd_ref, hs_ref, xs_rows = refs[9 + 2 * n_cast:]
    i = pl.program_id(0)
    n_chunks = D_FF // FFN_CHUNK

    @pl.when(i == 0)
    def _sample_input():
        xs_rows[...] = xs_ref[...].reshape(FFN_ROWS, D_MODEL)
        hs_ref[...] = _rms(xs_rows[...], norm_ref[...]).astype(BF16)

    @pl.when(i < n_chunks)
    def _weight_chunk():
        cols = pl.ds(pl.multiple_of(i * FFN_CHUNK, FFN_CHUNK), FFN_CHUNK)
        wg_c = wg_in[...].astype(BF16)
        wu_c = wu_in[...].astype(BF16)
        wg_ref[:, cols] = wg_c
        wu_ref[:, cols] = wu_c
        wd_ref[cols, :] = wd_in[...].astype(BF16)
        a = _dot(hs_ref[...], wg_c)
        b = _dot(hs_ref[...], wu_c)
        act_ref[:, cols] = (a * jax.nn.sigmoid(a) * b).astype(BF16)

    @pl.when(i == n_chunks)
    def _finish_sample_tile():
        y = xs_rows[...] + FFN_RESIDUAL * _dot(act_ref[...], wd_ref[...])
        if final_norm:
            y = _rms(y, fnorm_ref[...])
        os_ref[...] = y.reshape(os_ref.shape)

    for src, dst in zip(cast_in, cast_out):
        dst[...] = src[...].astype(BF16)

    def _sub_tile(k, carry):
        rows = pl.ds(pl.multiple_of(k * FFN_ROWS, FFN_ROWS), FFN_ROWS)
        x = xp_ref[rows, :]
        h = _rms(x, norm_ref[...]).astype(BF16)
        for c in range(n_chunks):
            cols = slice(c * FFN_CHUNK, (c + 1) * FFN_CHUNK)
            a = _dot(h, wg_ref[:, cols])
            b = _dot(h, wu_ref[:, cols])
            act_ref[:, cols] = (a * jax.nn.sigmoid(a) * b).astype(BF16)
        y = x + FFN_RESIDUAL * _dot(act_ref[...], wd_ref[...])
        if final_norm:
            y = _rms(y, fnorm_ref[...])
        op_ref[rows, :] = y
        return carry

    lax.fori_loop(0, jnp.where(i <= n_chunks, 0, subtiles), _sub_tile, 0)


def _block_rows(rows, n_steps):
    br = BF16_SUBLANES
    while rows % br or rows // br > n_steps:
        br += BF16_SUBLANES
    return br


def _ffn_half(xp, xs, norm, wg, wu, wd, fnorm, *, final_norm, subtiles, to_cast=(), sample_out_shape=None):
    block_rows = subtiles * FFN_ROWS
    n_prompt_tiles = xp.shape[0] // block_rows
    n_chunks = D_FF // FFN_CHUNK
    first = n_chunks + 1
    sample_out_shape = sample_out_shape or (FFN_ROWS, D_MODEL)
    assert xp.shape[0] == n_prompt_tiles * block_rows and xs.size == FFN_ROWS * D_MODEL
    prompt_spec = pl.BlockSpec((block_rows, D_MODEL), lambda i: (jnp.maximum(i - first, 0), 0))
    sample_in_spec = pl.BlockSpec(xs.shape, lambda i: (0,) * xs.ndim)
    sample_out_spec = pl.BlockSpec(sample_out_shape, lambda i: (0,) * len(sample_out_shape))
    up_chunk_spec = pl.BlockSpec((D_MODEL, FFN_CHUNK), lambda i: (0, jnp.minimum(i, n_chunks - 1)))
    down_chunk_spec = pl.BlockSpec((FFN_CHUNK, D_MODEL), lambda i: (jnp.minimum(i, n_chunks - 1), 0))

    def row_streamed(w):
        br = _block_rows(w.shape[0], n_prompt_tiles)
        return pl.BlockSpec((br, w.shape[1]), functools.partial(
            lambda i, last: (jnp.clip(i - first, 0, last), 0), last=w.shape[0] // br - 1))

    cast_specs = [row_streamed(w) for w in to_cast]
    cast_shapes = [jax.ShapeDtypeStruct(w.shape, BF16) for w in to_cast]
    return pl.pallas_call(
        functools.partial(_ffn_body, final_norm=final_norm, n_cast=len(to_cast), subtiles=subtiles),
        grid=(first + n_prompt_tiles,),
        in_specs=[prompt_spec, sample_in_spec, _resident((1, D_MODEL)), up_chunk_spec, up_chunk_spec, down_chunk_spec,
                  _resident((1, D_MODEL))] + cast_specs,
        out_specs=[prompt_spec, sample_out_spec] + cast_specs,
        out_shape=[jax.ShapeDtypeStruct(xp.shape, F32), jax.ShapeDtypeStruct(sample_out_shape, F32)] + cast_shapes,
        scratch_shapes=[
            pltpu.VMEM((FFN_ROWS, D_FF), BF16),
            pltpu.VMEM((D_MODEL, D_FF), BF16),
            pltpu.VMEM((D_MODEL, D_FF), BF16),
            pltpu.VMEM((D_FF, D_MODEL), BF16),
            pltpu.VMEM((FFN_ROWS, D_MODEL), BF16),
            pltpu.VMEM((FFN_ROWS, D_MODEL), F32),
        ],
        compiler_params=pltpu.CompilerParams(dimension_semantics=("arbitrary",), vmem_limit_bytes=VMEM_LIMIT),
        name="ffn_half_final" if final_norm else "ffn_half",
    )(xp, xs, norm, wg, wu, wd, fnorm, *to_cast)


def _pair_heads(t):
    swapped = pltpu.roll(t, HEAD_DIM, 1)
    left = lax.broadcasted_iota(jnp.int32, t.shape, 1) < HEAD_DIM
    return jnp.where(left, t, swapped), jnp.where(left, swapped, t)


def _gate_norm(g, vnorm_ref, gi):
    gg = g[:, gi * GROUP_CH:(gi + 1) * GROUP_CH]
    return _rms(gg, vnorm_ref[gi:gi + 1, :])


def _merge(x, attn, gate, anorm_ref, gnorm_ref, wout_ref):
    cat = jnp.concatenate([_rms(attn, anorm_ref[...]), _rms(gate, gnorm_ref[...])], axis=1)
    return x + _dot(cat.astype(BF16), wout_ref[...])


def _mixer_prompt_body(x_ref, mnorm_ref, win_ref, sinks_ref, vnorm_ref, ws_ref, bs_ref, anorm_ref, gnorm_ref,
                       wout_ref, o_ref, pk_ref, pv_ref, kv_s, q_s, attn_s, gate_s, bias_s, wtril_s, bcol_s):
    b = pl.program_id(0)
    t = pl.program_id(1)
    nblk = MIX_ROWS // WINDOW
    band = 2 * WINDOW
    qrows = GQA_GROUP * WINDOW

    @pl.when((b == 0) & (t == 0))
    def _init_tables():
        row = lax.broadcasted_iota(jnp.int32, (qrows, band), 0)
        key = lax.broadcasted_iota(jnp.int32, (qrows, band), 1)
        grp = row >> WINDOW_BITS
        dist = WINDOW + (row & (WINDOW - 1)) - key
        valid = (dist >= 0) & (dist < WINDOW)
        distf = dist.astype(F32)
        for h in range(N_KV_HEADS):
            s = [ALIBI_SLOPES[h * GQA_GROUP + g] for g in range(GQA_GROUP)]
            slope = jnp.where(grp == 0, s[0], jnp.where(grp == 1, s[1], jnp.where(grp == 2, s[2], s[3])))
            bias = jnp.where(valid, -(slope * distf) * LOG2E, MASK_VALUE)
            bias_s[0, h] = bias
            bias_s[1, h] = jnp.where(key < WINDOW, MASK_VALUE, bias)
        r = lax.broadcasted_iota(jnp.int32, (CHUNK, CHUNK), 0)
        c = lax.broadcasted_iota(jnp.int32, (CHUNK, CHUNK), 1)
        for gi in range(N_GATE_GROUPS):
            wtril_s[gi] = jnp.where(r >= c, ws_ref[gi], 0.0).astype(BF16)
            bcol_s[:, gi:gi + 1] = jnp.sum(jnp.where(r == c, bs_ref[gi:gi + 1, :], 0.0), axis=1, keepdims=True)

    @pl.when(t == 0)
    def _no_previous_block():
        kv_s[0:WINDOW, :] = jnp.zeros((WINDOW, 4 * KV_WIDTH), BF16)

    x = x_ref[...]
    h_in = _rms(x, mnorm_ref[...]).astype(BF16)
    q_s[...] = _dot(h_in, win_ref[:, 0:ATTN_WIDTH]) * (ATTN_SCALE * LOG2E)
    wkv = win_ref[:, ATTN_WIDTH:ATTN_WIDTH + 2 * KV_WIDTH]
    pkv = jnp.concatenate([_dot(h_in[0:MIX_ROWS // 2, :], wkv), _dot(h_in[MIX_ROWS // 2:, :], wkv)], axis=0)
    k_new = pkv[:, 0:KV_WIDTH]
    v_new = pkv[:, KV_WIDTH:2 * KV_WIDTH]
    pk_ref[0] = k_new[MIX_ROWS - WINDOW:, :].T
    pv_ref[0] = v_new[MIX_ROWS - WINDOW:, :].T
    k0, k1 = _pair_heads(k_new)
    v0, v1 = _pair_heads(v_new)
    kv_s[WINDOW:, :] = jnp.concatenate([k0, k1, v0, v1], axis=1).astype(BF16)

    table = jnp.where(t == 0, 1, 0)
    lane_grp = lax.broadcasted_iota(jnp.int32, (WINDOW, 2 * KV_WIDTH), 1) >> HEAD_DIM_BITS
    for j in range(nblk):
        rows = slice(j * WINDOW, (j + 1) * WINDOW)
        kvb = kv_s[j * WINDOW:j * WINDOW + band, :]
        for h in range(N_KV_HEADS):
            ka = kvb[:, h * KV_WIDTH:(h + 1) * KV_WIDTH]
            va = kvb[:, (2 + h) * KV_WIDTH:(3 + h) * KV_WIDTH]
            k4 = jnp.concatenate([ka, ka], axis=1)
            v4 = jnp.concatenate([va, va], axis=1)
            q = q_s[rows, h * 2 * KV_WIDTH:(h + 1) * 2 * KV_WIDTH]
            qstack = jnp.concatenate([jnp.where(lane_grp == g, q, 0.0) for g in range(GQA_GROUP)],
                                     axis=0).astype(BF16)
            bias = bias_s[table, h] if j == 0 else bias_s[0, h]
            s = _dot_nt(qstack, k4) + bias
            probs, inv = [], []
            for g in range(GQA_GROUP):
                sg = s[g * WINDOW:(g + 1) * WINDOW, :]
                sink = sinks_ref[h * GQA_GROUP + g] * LOG2E
                m = jnp.maximum(jnp.max(sg, axis=1, keepdims=True), sink)
                p = jnp.exp2(sg - m)
                inv.append(1.0 / (jnp.sum(p, axis=1, keepdims=True) + jnp.exp2(sink - m)))
                probs.append(p.astype(BF16))
            o = _dot(jnp.concatenate(probs, axis=0), v4)
            for g in range(GQA_GROUP):
                lanes = slice(g * HEAD_DIM, (g + 1) * HEAD_DIM)
                out_lanes = slice((h * GQA_GROUP + g) * HEAD_DIM, (h * GQA_GROUP + g + 1) * HEAD_DIM)
                attn_s[rows, out_lanes] = o[g * WINDOW:(g + 1) * WINDOW, lanes] * inv[g]
    kv_s[0:WINDOW, :] = kv_s[MIX_ROWS:MIX_ROWS + WINDOW, :]

    u = jax.nn.gelu(_dot(h_in, win_ref[:, U_COL:G_COL]))
    gact = jax.nn.gelu(_dot(h_in, win_ref[:, G_COL:G_COL + GMLP_WIDTH]))
    for gi in range(N_GATE_GROUPS):
        lanes = slice(gi * GROUP_CH, (gi + 1) * GROUP_CH)
        gn = _gate_norm(gact, vnorm_ref, gi).astype(BF16)
        chunks = jnp.concatenate([gn[c * CHUNK:(c + 1) * CHUNK, :] for c in range(nblk)], axis=1)
        mixed = _dot(wtril_s[gi], chunks) + bcol_s[:, gi:gi + 1]
        for c in range(nblk):
            rows = slice(c * CHUNK, (c + 1) * CHUNK)
            gate_s[rows, lanes] = u[rows, lanes] * mixed[:, c * GROUP_CH:(c + 1) * GROUP_CH]

    o_ref[...] = _merge(x, attn_s[...], gate_s[...], anorm_ref, gnorm_ref, wout_ref)


def _mixer_prompt(x, mnorm, win, sinks, vnorm, ws, bs, anorm, gnorm, wout, *, batch):
    rows = x.shape[0]
    tiles = rows // (batch * MIX_ROWS)
    assert rows == batch * tiles * MIX_ROWS
    band = 2 * WINDOW
    row_spec = pl.BlockSpec((MIX_ROWS, D_MODEL), lambda b, t: (b * tiles + t, 0))
    kv_spec = pl.BlockSpec((1, KV_WIDTH, WINDOW), lambda b, t: (b, 0, 0))
    kv_shape = jax.ShapeDtypeStruct((batch, KV_WIDTH, WINDOW), F32)
    return pl.pallas_call(
        _mixer_prompt_body,
        grid=(batch, tiles),
        in_specs=[row_spec, _resident((1, D_MODEL)), _resident(win.shape), _smem(), _resident(vnorm.shape),
                  _resident(ws.shape), _resident(bs.shape), _resident(anorm.shape), _resident(gnorm.shape),
                  _resident(wout.shape)],
        out_specs=[row_spec, kv_spec, kv_spec],
        out_shape=[jax.ShapeDtypeStruct(x.shape, F32), kv_shape, kv_shape],
        scratch_shapes=[
            pltpu.VMEM((WINDOW + MIX_ROWS, 4 * KV_WIDTH), BF16),
            pltpu.VMEM((MIX_ROWS, ATTN_WIDTH), F32),
            pltpu.VMEM((MIX_ROWS, ATTN_WIDTH), F32),
            pltpu.VMEM((MIX_ROWS, GMLP_WIDTH), F32),
            pltpu.VMEM((2, N_KV_HEADS, GQA_GROUP * WINDOW, band), F32),
            pltpu.VMEM((N_GATE_GROUPS, CHUNK, CHUNK), BF16),
            pltpu.VMEM((CHUNK, N_GATE_GROUPS), F32),
        ],
        compiler_params=pltpu.CompilerParams(dimension_semantics=("arbitrary", "arbitrary"),
                                             vmem_limit_bytes=VMEM_LIMIT),
        name="mixer_prompt",
    )(x, mnorm, win, sinks, vnorm, ws, bs, anorm, gnorm, wout)


def _mixer_sample_body(x_ref, mnorm_ref, win_ref, sinks_ref, vnorm_ref, ws_ref, bs_ref, anorm_ref, gnorm_ref,
                       wout_ref, ck_ref, cv_ref, o_ref, gn_ref, sk_ref, sv_ref,
                       perm_s, qall_s, oall_s, kvt_s, gate_s, bias_s, sink_s, s_s, p_s, *, n_seq, n_tok):
    step = pl.program_id(0)
    n_steps = pl.num_programs(0)
    n_rows = n_seq * n_tok
    q_rows = N_HEADS * n_tok
    keys = 2 * WINDOW
    step_rows = SEQ_PER_STEP * q_rows
    tok_bits = n_tok.bit_length() - 1

    def by_token(val):
        for c in range(val.shape[1] // LANES):
            perm_s[c] = val[:, c * LANES:(c + 1) * LANES]
        return jnp.concatenate(
            [jnp.concatenate([perm_s[c, pl.ds(i, n_seq, stride=n_tok), :] for i in range(n_tok)], axis=0)
             for c in range(val.shape[1] // LANES)], axis=1)

    def by_sequence(val):
        for c in range(val.shape[1] // LANES):
            for i in range(n_tok):
                perm_s[c, pl.ds(i, n_seq, stride=n_tok), :] = val[i * n_seq:(i + 1) * n_seq, c * LANES:(c + 1) * LANES]
        return jnp.concatenate([perm_s[c] for c in range(val.shape[1] // LANES)], axis=1)

    @pl.when(step == 0)
    def _project():
        h = _rms(x_ref[...], mnorm_ref[...])
        pkv = _dot(h.astype(BF16), win_ref[:, ATTN_WIDTH:ATTN_WIDTH + 2 * KV_WIDTH])
        kvt_s[...] = pkv.T
        h_in = by_token(h).astype(BF16)
        pq = _dot(h_in, win_ref[:, 0:ATTN_WIDTH]) * (ATTN_SCALE * LOG2E)
        half = lax.broadcasted_iota(jnp.int32, (n_rows, LANES), 1) >> HEAD_DIM_BITS
        for hg in range(N_HEADS):
            kvh = hg // GQA_GROUP
            slab = pq[:, (hg // 2) * LANES:(hg // 2 + 1) * LANES]
            if (hg % 2) != kvh:
                slab = pltpu.roll(slab, HEAD_DIM, 1)
            qall_s[hg * n_rows:(hg + 1) * n_rows, :] = jnp.where(half == kvh, slab, 0.0)

        u = jax.nn.gelu(_dot(h_in, win_ref[:, U_COL:G_COL]))
        gact = jax.nn.gelu(_dot(h_in, win_ref[:, G_COL:G_COL + GMLP_WIDTH]))
        gns = []
        for gi in range(N_GATE_GROUPS):
            lanes = slice(gi * GROUP_CH, (gi + 1) * GROUP_CH)
            gn = _gate_norm(gact, vnorm_ref, gi)
            gns.append(gn)
            for i in range(n_tok):
                rows = slice(i * n_seq, (i + 1) * n_seq)
                w_row = ws_ref[gi, i:i + 1, :]
                mixed = bs_ref[gi:gi + 1, i:i + 1]
                for j in range(i + 1):
                    mixed = mixed + w_row[:, j:j + 1] * gn[j * n_seq:(j + 1) * n_seq, :]
                gate_s[rows, lanes] = u[rows, lanes] * mixed
        gn_ref[...] = by_sequence(jnp.concatenate(gns, axis=1)).reshape(gn_ref.shape)

        row = lax.broadcasted_iota(jnp.int32, (step_rows, keys), 0)
        col = lax.broadcasted_iota(jnp.int32, (step_rows, keys), 1)
        tok = row & (n_tok - 1)
        head = (row >> tok_bits) & (N_HEADS - 1)
        slope = jnp.zeros((step_rows, keys), F32)
        for hg in range(N_HEADS):
            slope = jnp.where(head == hg, ALIBI_SLOPES[hg], slope)
        shifted = col < WINDOW
        pos = jnp.where(shifted, col + n_tok, col - WINDOW)
        dist = WINDOW + tok - pos
        valid = (dist >= 0) & (dist < WINDOW) & (shifted | (pos < n_tok))
        bias_s[...] = jnp.where(valid, -(slope * dist.astype(F32)) * LOG2E, MASK_VALUE)
        head1 = (lax.broadcasted_iota(jnp.int32, (step_rows, 1), 0) >> tok_bits) & (N_HEADS - 1)
        sink = jnp.zeros((step_rows, 1), F32)
        for hg in range(N_HEADS):
            sink = jnp.where(head1 == hg, sinks_ref[hg] * LOG2E, sink)
        sink_s[...] = sink

    col0 = pl.multiple_of(step * LANES, LANES)
    k_cols = kvt_s[0:KV_WIDTH, pl.ds(col0, LANES)]
    v_cols = kvt_s[KV_WIDTH:2 * KV_WIDTH, pl.ds(col0, LANES)]
    is_new = lax.broadcasted_iota(jnp.int32, (KV_WIDTH, WINDOW), 1) >= WINDOW - n_tok
    for bl in range(SEQ_PER_STEP):
        to_tail = (WINDOW - n_tok - n_tok * bl) % LANES
        sk_ref[bl] = jnp.where(is_new, pltpu.roll(k_cols, to_tail, 1), pltpu.roll(ck_ref[bl], WINDOW - n_tok, 1))
        sv_ref[bl] = jnp.where(is_new, pltpu.roll(v_cols, to_tail, 1), pltpu.roll(cv_ref[bl], WINDOW - n_tok, 1))

    seq0 = step * SEQ_PER_STEP
    for bl in range(SEQ_PER_STEP):
        qb = qall_s[pl.ds(seq0 + bl, q_rows, stride=n_seq), :].astype(BF16)
        k2 = jnp.concatenate([sk_ref[bl], ck_ref[bl]], axis=1).astype(BF16)
        s_s[bl * q_rows:(bl + 1) * q_rows, :] = _dot(qb, k2)
    s = s_s[...] + bias_s[...]
    sink = sink_s[...]
    m = jnp.maximum(jnp.max(s, axis=1, keepdims=True), sink)
    p = jnp.exp2(s - m)
    inv = 1.0 / (jnp.sum(p, axis=1, keepdims=True) + jnp.exp2(sink - m))
    p_s[...] = p.astype(BF16)
    for bl in range(SEQ_PER_STEP):
        rows = slice(bl * q_rows, (bl + 1) * q_rows)
        v2 = jnp.concatenate([sv_ref[bl], cv_ref[bl]], axis=1).astype(BF16)
        oall_s[pl.ds(seq0 + bl, q_rows, stride=n_seq), :] = _dot_nt(p_s[rows, :], v2) * inv[rows, :]

    @pl.when(step == n_steps - 1)
    def _merge_rows():
        left = lax.broadcasted_iota(jnp.int32, (n_rows, LANES), 1) < HEAD_DIM
        slabs = []
        for pair in range(N_HEADS // 2):
            halves = []
            for hg in (2 * pair, 2 * pair + 1):
                o = oall_s[hg * n_rows:(hg + 1) * n_rows, :]
                if (hg % 2) != (hg // GQA_GROUP):
                    o = pltpu.roll(o, HEAD_DIM, 1)
                halves.append(o)
            slabs.append(jnp.where(left, halves[0], halves[1]))
        attn = jnp.concatenate(slabs, axis=1)
        o_ref[...] = by_sequence(_merge(by_token(x_ref[...]), attn, gate_s[...], anorm_ref, gnorm_ref, wout_ref))


def _mixer_sample(x, mnorm, win, sinks, vnorm, ws, bs, anorm, gnorm, wout, ck, cv, *, n_seq, n_tok):
    n_rows = n_seq * n_tok
    q_rows = N_HEADS * n_tok
    assert x.shape[0] == n_rows and n_seq % SEQ_PER_STEP == 0 and n_tok == 4 and n_seq == LANES
    assert SEQ_PER_STEP * n_tok == LANES
    cache_spec = pl.BlockSpec((SEQ_PER_STEP, KV_WIDTH, WINDOW), lambda s: (s, 0, 0))
    cache_shape = jax.ShapeDtypeStruct(ck.shape, F32)
    rows_spec = pl.BlockSpec((n_rows, D_MODEL), lambda s: (0, 0))
    gn_spec = pl.BlockSpec((n_seq, n_tok, GMLP_WIDTH), lambda s: (0, 0, 0))
    return pl.pallas_call(
        functools.partial(_mixer_sample_body, n_seq=n_seq, n_tok=n_tok),
        grid=(n_seq // SEQ_PER_STEP,),
        in_specs=[rows_spec, _resident((1, D_MODEL)), _resident(win.shape), _smem(), _resident(vnorm.shape),
                  _resident(ws.shape), _resident(bs.shape), _resident(anorm.shape), _resident(gnorm.shape), _resident(wout.shape),
                  cache_spec, cache_spec],
        out_specs=[rows_spec, gn_spec, cache_spec, cache_spec],
        out_shape=[jax.ShapeDtypeStruct(x.shape, F32), jax.ShapeDtypeStruct((n_seq, n_tok, GMLP_WIDTH), F32),
                   cache_shape, cache_shape],
        scratch_shapes=[
            pltpu.VMEM((D_MODEL // LANES, n_rows, LANES), F32),
            pltpu.VMEM((N_HEADS * n_rows, LANES), F32),
            pltpu.VMEM((N_HEADS * n_rows, LANES), F32),
            pltpu.VMEM((2 * KV_WIDTH, n_rows), F32),
            pltpu.VMEM((n_rows, GMLP_WIDTH), F32),
            pltpu.VMEM((SEQ_PER_STEP * q_rows, 2 * WINDOW), F32),
            pltpu.VMEM((SEQ_PER_STEP * q_rows, 1), F32),
            pltpu.VMEM((SEQ_PER_STEP * q_rows, 2 * WINDOW), F32),
            pltpu.VMEM((SEQ_PER_STEP * q_rows, 2 * WINDOW), BF16),
        ],
        compiler_params=pltpu.CompilerParams(dimension_semantics=("arbitrary",), vmem_limit_bytes=VMEM_LIMIT),
        name="mixer_sample",
    )(x, mnorm, win, sinks, vnorm, ws, bs, anorm, gnorm, wout, ck, cv)


def kernel(x_prompt, x_sample, cache_k, cache_v, ffn1_norm, ffn1_w_gate, ffn1_w_up, ffn1_w_down, mix_norm, w_in,
           attn_sinks, gmlp_v_norm, gmlp_w_spatial, gmlp_b_spatial, attn_out_norm, gmlp_out_norm, w_out, ffn2_norm,
           ffn2_w_gate, ffn2_w_up, ffn2_w_down, final_norm):
    batch, seq, _ = x_prompt.shape
    n_seq, n_tok, _ = x_sample.shape
    depth = cache_k.shape[0]
    assert depth == 1

    xp = x_prompt.reshape(batch * seq, D_MODEL)
    row = lambda v: v.reshape(1, -1)
    fnorm = row(final_norm)
    l = 0
    sinks = attn_sinks[l]
    vnorm = gmlp_v_norm[l]
    ws = gmlp_w_spatial[l]
    bs = gmlp_b_spatial[l]
    mnorm, anorm, gnorm = row(mix_norm[l]), row(attn_out_norm[l]), row(gmlp_out_norm[l])

    later = (ffn2_w_gate[l], ffn2_w_up[l], ffn2_w_down[l], w_in[l], w_out[l])
    xp, xs, wg2, wu2, wd2, win, wout = _ffn_half(xp, x_sample, row(ffn1_norm[l]), ffn1_w_gate[l], ffn1_w_up[l],
                                                 ffn1_w_down[l], fnorm, final_norm=False, subtiles=1, to_cast=later)
    xp, pk, pv = _mixer_prompt(xp, mnorm, win, sinks, vnorm, ws, bs, anorm, gnorm, wout, batch=batch)
    to_cols = lambda c: jnp.transpose(c, (0, 2, 3, 1)).reshape(n_seq, KV_WIDTH, WINDOW)
    to_rows = lambda c: jnp.transpose(c.reshape(-1, N_KV_HEADS, HEAD_DIM, WINDOW), (0, 3, 1, 2))[None]
    xs, gn, sk, sv = _mixer_sample(xs, mnorm, win, sinks, vnorm, ws, bs, anorm, gnorm, wout, to_cols(cache_k[l]),
                                   to_cols(cache_v[l]), n_seq=n_seq, n_tok=n_tok)
    yp, y_sample = _ffn_half(xp, xs, row(ffn2_norm[l]), wg2, wu2, wd2, fnorm, final_norm=True, subtiles=2,
                             sample_out_shape=x_sample.shape)

    y_prompt = yp.reshape(batch, seq, D_MODEL)
    return (y_prompt, y_sample, to_rows(pk), to_rows(pv), to_rows(sk), to_rows(sv), gn[None])
```

```python
import functools

import jax
import jax.numpy as jnp
from jax import lax
from jax.experimental import pallas as pl
from jax.experimental.pallas import tpu as pltpu

F32 = jnp.float32
BF16 = jnp.bfloat16

D_MODEL = 1024
D_FF = 2816
N_HEADS = 8
N_KV_HEADS = 2
GQA_GROUP = 4
HEAD_DIM = 64
ATTN_WIDTH = 512
KV_WIDTH = 128
GMLP_WIDTH = 512
N_GATE_GROUPS = 4
GROUP_CH = 128
U_COL = ATTN_WIDTH + 2 * KV_WIDTH
G_COL = U_COL + GMLP_WIDTH
WINDOW = 128
CHUNK = 128
WINDOW_BITS = 7
HEAD_DIM_BITS = 6
RMS_EPS = 1e-6
FFN_RESIDUAL = 0.5
ATTN_SCALE = 0.125
LOG2E = 1.4426950408889634
ALIBI_SLOPES = tuple(2.0 ** (-(i + 1)) for i in range(N_HEADS))
MASK_VALUE = -1e30

LANES = 128
BF16_SUBLANES = 16
FFN_ROWS = 512
FFN_CHUNK = 256
MIX_ROWS = 1024
SEQ_PER_STEP = 32
VMEM_LIMIT = 56 * 1024 * 1024


def _rms(x, g):
    ms = jnp.mean(x * x, axis=-1, keepdims=True)
    return x * lax.rsqrt(ms + RMS_EPS) * g


def _dot(a, b):
    return jnp.dot(a, b, preferred_element_type=F32)


def _dot_nt(a, b):
    return lax.dot_general(a, b, (((1,), (1,)), ((), ())), preferred_element_type=F32)


def _resident(shape):
    nd = len(shape)
    return pl.BlockSpec(shape, lambda *_: (0,) * nd, pipeline_mode=pl.Buffered(1))


def _smem():
    return pl.BlockSpec(memory_space=pltpu.SMEM)


def _ffn_body(*refs, final_norm, n_cast, subtiles):
    xp_ref, xs_ref, norm_ref, wg_in, wu_in, wd_in, fnorm_ref = refs[:7]
    cast_in = refs[7:7 + n_cast]
    op_ref, os_ref = refs[7 + n_cast:9 + n_cast]
    cast_out = refs[9 + n_cast:9 + 2 * n_cast]
    act_ref, wg_ref, wu_ref, wd_ref, hs_ref, xs_rows = refs[9 + 2 * n_cast:]
    i = pl.program_id(0)
    n_chunks = D_FF // FFN_CHUNK

    @pl.when(i == 0)
    def _sample_input():
        xs_rows[...] = xs_ref[...].reshape(FFN_ROWS, D_MODEL)
        hs_ref[...] = _rms(xs_rows[...], norm_ref[...]).astype(BF16)

    @pl.when(i < n_chunks)
    def _weight_chunk():
        cols = pl.ds(pl.multiple_of(i * FFN_CHUNK, FFN_CHUNK), FFN_CHUNK)
        wg_c = wg_in[...].astype(BF16)
        wu_c = wu_in[...].astype(BF16)
        wg_ref[:, cols] = wg_c
        wu_ref[:, cols] = wu_c
        wd_ref[cols, :] = wd_in[...].astype(BF16)
        a = _dot(hs_ref[...], wg_c)
        b = _dot(hs_ref[...], wu_c)
        act_ref[:, cols] = (a * jax.nn.sigmoid(a) * b).astype(BF16)

    @pl.when(i == n_chunks)
    def _finish_sample_tile():
        y = xs_rows[...] + FFN_RESIDUAL * _dot(act_ref[...], wd_ref[...])
        if final_norm:
            y = _rms(y, fnorm_ref[...])
        os_ref[...] = y.reshape(os_ref.shape)

    for src, dst in zip(cast_in, cast_out):
        dst[...] = src[...].astype(BF16)

    def _sub_tile(k, carry):
        rows = pl.ds(pl.multiple_of(k * FFN_ROWS, FFN_ROWS), FFN_ROWS)
        x = xp_ref[rows, :]
        h = _rms(x, norm_ref[...]).astype(BF16)
        for c in range(n_chunks):
            cols = slice(c * FFN_CHUNK, (c + 1) * FFN_CHUNK)
            a = _dot(h, wg_ref[:, cols])
            b = _dot(h, wu_ref[:, cols])
            act_ref[:, cols] = (a * jax.nn.sigmoid(a) * b).astype(BF16)
        y = x + FFN_RESIDUAL * _dot(act_ref[...], wd_ref[...])
        if final_norm:
            y = _rms(y, fnorm_ref[...])
        op_ref[rows, :] = y
        return carry

    lax.fori_loop(0, jnp.where(i <= n_chunks, 0, subtiles), _sub_tile, 0)


def _block_rows(rows, n_steps):
    br = BF16_SUBLANES
    while rows % br or rows // br > n_steps:
        br += BF16_SUBLANES
    return br


def _ffn_half(xp, xs, norm, wg, wu, wd, fnorm, *, final_norm, subtiles, to_cast=(), sample_out_shape=None):
    block_rows = subtiles * FFN_ROWS
    n_prompt_tiles = xp.shape[0] // block_rows
    n_chunks = D_FF // FFN_CHUNK
    first = n_chunks + 1
    sample_out_shape = sample_out_shape or (FFN_ROWS, D_MODEL)
    assert xp.shape[0] == n_prompt_tiles * block_rows and xs.size == FFN_ROWS * D_MODEL
    prompt_spec = pl.BlockSpec((block_rows, D_MODEL), lambda i: (jnp.maximum(i - first, 0), 0))
    sample_in_spec = pl.BlockSpec(xs.shape, lambda i: (0,) * xs.ndim)
    sample_out_spec = pl.BlockSpec(sample_out_shape, lambda i: (0,) * len(sample_out_shape))
    up_chunk_spec = pl.BlockSpec((D_MODEL, FFN_CHUNK), lambda i: (0, jnp.minimum(i, n_chunks - 1)))
    down_chunk_spec = pl.BlockSpec((FFN_CHUNK, D_MODEL), lambda i: (jnp.minimum(i, n_chunks - 1), 0))

    def row_streamed(w):
        br = _block_rows(w.shape[0], n_prompt_tiles)
        return pl.BlockSpec((br, w.shape[1]), functools.partial(
            lambda i, last: (jnp.clip(i - first, 0, last), 0), last=w.shape[0] // br - 1))

    cast_specs = [row_streamed(w) for w in to_cast]
    cast_shapes = [jax.ShapeDtypeStruct(w.shape, BF16) for w in to_cast]
    return pl.pallas_call(
        functools.partial(_ffn_body, final_norm=final_norm, n_cast=len(to_cast), subtiles=subtiles),
        grid=(first + n_prompt_tiles,),
        in_specs=[prompt_spec, sample_in_spec, _resident((1, D_MODEL)), up_chunk_spec, up_chunk_spec, down_chunk_spec,
                  _resident((1, D_MODEL))] + cast_specs,
        out_specs=[prompt_spec, sample_out_spec] + cast_specs,
        out_shape=[jax.ShapeDtypeStruct(xp.shape, F32), jax.ShapeDtypeStruct(sample_out_shape, F32)] + cast_shapes,
        scratch_shapes=[
            pltpu.VMEM((FFN_ROWS, D_FF), BF16),
            pltpu.VMEM((D_MODEL, D_FF), BF16),
            pltpu.VMEM((D_MODEL, D_FF), BF16),
            pltpu.VMEM((D_FF, D_MODEL), BF16),
            pltpu.VMEM((FFN_ROWS, D_MODEL), BF16),
            pltpu.VMEM((FFN_ROWS, D_MODEL), F32),
        ],
        compiler_params=pltpu.CompilerParams(dimension_semantics=("arbitrary",), vmem_limit_bytes=VMEM_LIMIT),
        name="ffn_half_final" if final_norm else "ffn_half",
    )(xp, xs, norm, wg, wu, wd, fnorm, *to_cast)


def _pair_heads(t):
    swapped = pltpu.roll(t, HEAD_DIM, 1)
    left = lax.broadcasted_iota(jnp.int32, t.shape, 1) < HEAD_DIM
    return jnp.where(left, t, swapped), jnp.where(left, swapped, t)


def _gate_norm(g, vnorm_ref, gi):
    gg = g[:, gi * GROUP_CH:(gi + 1) * GROUP_CH]
    return _rms(gg, vnorm_ref[gi:gi + 1, :])


def _merge(x, attn, gate, anorm_ref, gnorm_ref, wout_ref):
    cat = jnp.concatenate([_rms(attn, anorm_ref[...]), _rms(gate, gnorm_ref[...])], axis=1)
    return x + _dot(cat.astype(BF16), wout_ref[...])


def _mixer_prompt_body(x_ref, mnorm_ref, win_ref, sinks_ref, vnorm_ref, ws_ref, bs_ref, anorm_ref, gnorm_ref,
                       wout_ref, o_ref, pk_ref, pv_ref, kv_s, q_s, attn_s, gate_s, bias_s, wtril_s, bcol_s):
    b = pl.program_id(0)
    t = pl.program_id(1)
    nblk = MIX_ROWS // WINDOW
    band = 2 * WINDOW
    qrows = GQA_GROUP * WINDOW

    @pl.when((b == 0) & (t == 0))
    def _init_tables():
        row = lax.broadcasted_iota(jnp.int32, (qrows, band), 0)
        key = lax.broadcasted_iota(jnp.int32, (qrows, band), 1)
        grp = row >> WINDOW_BITS
        dist = WINDOW + (row & (WINDOW - 1)) - key
        valid = (dist >= 0) & (dist < WINDOW)
        distf = dist.astype(F32)
        for h in range(N_KV_HEADS):
            s = [ALIBI_SLOPES[h * GQA_GROUP + g] for g in range(GQA_GROUP)]
            slope = jnp.where(grp == 0, s[0], jnp.where(grp == 1, s[1], jnp.where(grp == 2, s[2], s[3])))
            bias = jnp.where(valid, -(slope * distf) * LOG2E, MASK_VALUE)
            bias_s[0, h] = bias
            bias_s[1, h] = jnp.where(key < WINDOW, MASK_VALUE, bias)
        r = lax.broadcasted_iota(jnp.int32, (CHUNK, CHUNK), 0)
        c = lax.broadcasted_iota(jnp.int32, (CHUNK, CHUNK), 1)
        for gi in range(N_GATE_GROUPS):
            wtril_s[gi] = jnp.where(r >= c, ws_ref[gi], 0.0).astype(BF16)
            bcol_s[:, gi:gi + 1] = jnp.sum(jnp.where(r == c, bs_ref[gi:gi + 1, :], 0.0), axis=1, keepdims=True)

    @pl.when(t == 0)
    def _no_previous_block():
        kv_s[0:WINDOW, :] = jnp.zeros((WINDOW, 4 * KV_WIDTH), BF16)

    x = x_ref[...]
    h_in = _rms(x, mnorm_ref[...]).astype(BF16)
    wkv = win_ref[:, ATTN_WIDTH:ATTN_WIDTH + 2 * KV_WIDTH]
    pkv = jnp.concatenate([_dot(h_in[0:MIX_ROWS // 2, :], wkv), _dot(h_in[MIX_ROWS // 2:, :], wkv)], axis=0)
    k_new = pkv[:, 0:KV_WIDTH]
    v_new = pkv[:, KV_WIDTH:2 * KV_WIDTH]
    pk_ref[0] = k_new[MIX_ROWS - WINDOW:, :].T
    pv_ref[0] = v_new[MIX_ROWS - WINDOW:, :].T
    k0, k1 = _pair_heads(k_new)
    v0, v1 = _pair_heads(v_new)
    kv_s[WINDOW:, :] = jnp.concatenate([k0, k1, v0, v1], axis=1).astype(BF16)
    q_s[...] = _dot(h_in, win_ref[:, 0:ATTN_WIDTH]) * (ATTN_SCALE * LOG2E)

    table = jnp.where(t == 0, 1, 0)
    lane_grp = lax.broadcasted_iota(jnp.int32, (WINDOW, 2 * KV_WIDTH), 1) >> HEAD_DIM_BITS
    for h in range(N_KV_HEADS):
        for j in range(nblk):
            rows = slice(j * WINDOW, (j + 1) * WINDOW)
            kvb = kv_s[j * WINDOW:j * WINDOW + band, :]
            ka = kvb[:, h * KV_WIDTH:(h + 1) * KV_WIDTH]
            va = kvb[:, (2 + h) * KV_WIDTH:(3 + h) * KV_WIDTH]
            k4 = jnp.concatenate([ka, ka], axis=1)
            v4 = jnp.concatenate([va, va], axis=1)
            q = q_s[rows, h * 2 * KV_WIDTH:(h + 1) * 2 * KV_WIDTH]
            qstack = jnp.concatenate([jnp.where(lane_grp == g, q, 0.0) for g in range(GQA_GROUP)],
                                     axis=0).astype(BF16)
            bias = bias_s[table, h] if j == 0 else bias_s[0, h]
            s = _dot_nt(qstack, k4) + bias
            probs, inv = [], []
            for g in range(GQA_GROUP):
                sg = s[g * WINDOW:(g + 1) * WINDOW, :]
                sink = sinks_ref[h * GQA_GROUP + g] * LOG2E
                m = jnp.maximum(jnp.max(sg, axis=1, keepdims=True), sink)
                p = jnp.exp2(sg - m)
                inv.append(1.0 / (jnp.sum(p, axis=1, keepdims=True) + jnp.exp2(sink - m)))
                probs.append(p.astype(BF16))
            o = _dot(jnp.concatenate(probs, axis=0), v4)
            for g in range(GQA_GROUP):
                lanes = slice(g * HEAD_DIM, (g + 1) * HEAD_DIM)
                out_lanes = slice((h * GQA_GROUP + g) * HEAD_DIM, (h * GQA_GROUP + g + 1) * HEAD_DIM)
                attn_s[rows, out_lanes] = o[g * WINDOW:(g + 1) * WINDOW, lanes] * inv[g]
    kv_s[0:WINDOW, :] = kv_s[MIX_ROWS:MIX_ROWS + WINDOW, :]

    u = jax.nn.gelu(_dot(h_in, win_ref[:, U_COL:G_COL]))
    gact = jax.nn.gelu(_dot(h_in, win_ref[:, G_COL:G_COL + GMLP_WIDTH]))
    for gi in range(N_GATE_GROUPS):
        lanes = slice(gi * GROUP_CH, (gi + 1) * GROUP_CH)
        gn = _gate_norm(gact, vnorm_ref, gi).astype(BF16)
        chunks = jnp.concatenate([gn[c * CHUNK:(c + 1) * CHUNK, :] for c in range(nblk)], axis=1)
        mixed = _dot(wtril_s[gi], chunks) + bcol_s[:, gi:gi + 1]
        for c in range(nblk):
            rows = slice(c * CHUNK, (c + 1) * CHUNK)
            gate_s[rows, lanes] = u[rows, lanes] * mixed[:, c * GROUP_CH:(c + 1) * GROUP_CH]

    o_ref[...] = _merge(x, attn_s[...], gate_s[...], anorm_ref, gnorm_ref, wout_ref)


def _mixer_prompt(x, mnorm, win, sinks, vnorm, ws, bs, anorm, gnorm, wout, *, batch):
    rows = x.shape[0]
    tiles = rows // (batch * MIX_ROWS)
    assert rows == batch * tiles * MIX_ROWS
    band = 2 * WINDOW
    row_spec = pl.BlockSpec((MIX_ROWS, D_MODEL), lambda b, t: (b * tiles + t, 0))
    kv_spec = pl.BlockSpec((1, KV_WIDTH, WINDOW), lambda b, t: (b, 0, 0))
    kv_shape = jax.ShapeDtypeStruct((batch, KV_WIDTH, WINDOW), F32)
    return pl.pallas_call(
        _mixer_prompt_body,
        grid=(batch, tiles),
        in_specs=[row_spec, _resident((1, D_MODEL)), _resident(win.shape), _smem(), _resident(vnorm.shape),
                  _resident(ws.shape), _resident(bs.shape), _resident(anorm.shape), _resident(gnorm.shape),
                  _resident(wout.shape)],
        out_specs=[row_spec, kv_spec, kv_spec],
        out_shape=[jax.ShapeDtypeStruct(x.shape, F32), kv_shape, kv_shape],
        scratch_shapes=[
            pltpu.VMEM((WINDOW + MIX_ROWS, 4 * KV_WIDTH), BF16),
            pltpu.VMEM((MIX_ROWS, ATTN_WIDTH), F32),
            pltpu.VMEM((MIX_ROWS, ATTN_WIDTH), F32),
            pltpu.VMEM((MIX_ROWS, GMLP_WIDTH), F32),
            pltpu.VMEM((2, N_KV_HEADS, GQA_GROUP * WINDOW, band), F32),
            pltpu.VMEM((N_GATE_GROUPS, CHUNK, CHUNK), BF16),
            pltpu.VMEM((CHUNK, N_GATE_GROUPS), F32),
        ],
        compiler_params=pltpu.CompilerParams(dimension_semantics=("arbitrary", "arbitrary"),
                                             vmem_limit_bytes=VMEM_LIMIT),
        name="mixer_prompt",
    )(x, mnorm, win, sinks, vnorm, ws, bs, anorm, gnorm, wout)


def _mixer_sample_body(x_ref, mnorm_ref, win_ref, sinks_ref, vnorm_ref, ws_ref, bs_ref, anorm_ref, gnorm_ref,
                       wout_ref, ck_ref, cv_ref, o_ref, gn_ref, sk_ref, sv_ref,
                       perm_s, qall_s, oall_s, kvt_s, gate_s, bias_s, sink_s, s_s, p_s, *, n_seq, n_tok):
    step = pl.program_id(0)
    n_steps = pl.num_programs(0)
    n_rows = n_seq * n_tok
    q_rows = N_HEADS * n_tok
    keys = 2 * WINDOW
    step_rows = SEQ_PER_STEP * q_rows
    tok_bits = n_tok.bit_length() - 1

    def by_token(val):
        for c in range(val.shape[1] // LANES):
            perm_s[c] = val[:, c * LANES:(c + 1) * LANES]
        return jnp.concatenate(
            [jnp.concatenate([perm_s[c, pl.ds(i, n_seq, stride=n_tok), :] for i in range(n_tok)], axis=0)
             for c in range(val.shape[1] // LANES)], axis=1)

    def by_sequence(val):
        for c in range(val.shape[1] // LANES):
            for i in range(n_tok):
                perm_s[c, pl.ds(i, n_seq, stride=n_tok), :] = val[i * n_seq:(i + 1) * n_seq, c * LANES:(c + 1) * LANES]
        return jnp.concatenate([perm_s[c] for c in range(val.shape[1] // LANES)], axis=1)

    @pl.when(step == 0)
    def _project():
        h = _rms(x_ref[...], mnorm_ref[...])
        pkv = _dot(h.astype(BF16), win_ref[:, ATTN_WIDTH:ATTN_WIDTH + 2 * KV_WIDTH])
        kvt_s[...] = pkv.T
        h_in = by_token(h).astype(BF16)
        pq = _dot(h_in, win_ref[:, 0:ATTN_WIDTH]) * (ATTN_SCALE * LOG2E)
        half = lax.broadcasted_iota(jnp.int32, (n_rows, LANES), 1) >> HEAD_DIM_BITS
        for hg in range(N_HEADS):
            kvh = hg // GQA_GROUP
            slab = pq[:, (hg // 2) * LANES:(hg // 2 + 1) * LANES]
            if (hg % 2) != kvh:
                slab = pltpu.roll(slab, HEAD_DIM, 1)
            qall_s[hg * n_rows:(hg + 1) * n_rows, :] = jnp.where(half == kvh, slab, 0.0)

        u = jax.nn.gelu(_dot(h_in, win_ref[:, U_COL:G_COL]))
        gact = jax.nn.gelu(_dot(h_in, win_ref[:, G_COL:G_COL + GMLP_WIDTH]))
        gns = []
        for gi in range(N_GATE_GROUPS):
            lanes = slice(gi * GROUP_CH, (gi + 1) * GROUP_CH)
            gn = _gate_norm(gact, vnorm_ref, gi)
            gns.append(gn)
            for i in range(n_tok):
                rows = slice(i * n_seq, (i + 1) * n_seq)
                w_row = ws_ref[gi, i:i + 1, :]
                mixed = bs_ref[gi:gi + 1, i:i + 1]
                for j in range(i + 1):
                    mixed = mixed + w_row[:, j:j + 1] * gn[j * n_seq:(j + 1) * n_seq, :]
                gate_s[rows, lanes] = u[rows, lanes] * mixed
        gn_ref[...] = by_sequence(jnp.concatenate(gns, axis=1)).reshape(gn_ref.shape)

        row = lax.broadcasted_iota(jnp.int32, (step_rows, keys), 0)
        col = lax.broadcasted_iota(jnp.int32, (step_rows, keys), 1)
        tok = row & (n_tok - 1)
        head = (row >> tok_bits) & (N_HEADS - 1)
        slope = jnp.zeros((step_rows, keys), F32)
        for hg in range(N_HEADS):
            slope = jnp.where(head == hg, ALIBI_SLOPES[hg], slope)
        shifted = col < WINDOW
        pos = jnp.where(shifted, col + n_tok, col - WINDOW)
        dist = WINDOW + tok - pos
        valid = (dist >= 0) & (dist < WINDOW) & (shifted | (pos < n_tok))
        bias_s[...] = jnp.where(valid, -(slope * dist.astype(F32)) * LOG2E, MASK_VALUE)
        head1 = (lax.broadcasted_iota(jnp.int32, (step_rows, 1), 0) >> tok_bits) & (N_HEADS - 1)
        sink = jnp.zeros((step_rows, 1), F32)
        for hg in range(N_HEADS):
            sink = jnp.where(head1 == hg, sinks_ref[hg] * LOG2E, sink)
        sink_s[...] = sink

    col0 = pl.multiple_of(step * LANES, LANES)
    k_cols = kvt_s[0:KV_WIDTH, pl.ds(col0, LANES)]
    v_cols = kvt_s[KV_WIDTH:2 * KV_WIDTH, pl.ds(col0, LANES)]
    is_new = lax.broadcasted_iota(jnp.int32, (KV_WIDTH, WINDOW), 1) >= WINDOW - n_tok
    for bl in range(SEQ_PER_STEP):
        to_tail = (WINDOW - n_tok - n_tok * bl) % LANES
        sk_ref[bl] = jnp.where(is_new, pltpu.roll(k_cols, to_tail, 1), pltpu.roll(ck_ref[bl], WINDOW - n_tok, 1))
        sv_ref[bl] = jnp.where(is_new, pltpu.roll(v_cols, to_tail, 1), pltpu.roll(cv_ref[bl], WINDOW - n_tok, 1))

    seq0 = step * SEQ_PER_STEP
    for bl in range(SEQ_PER_STEP):
        qb = qall_s[pl.ds(seq0 + bl, q_rows, stride=n_seq), :].astype(BF16)
        k2 = jnp.concatenate([sk_ref[bl], ck_ref[bl]], axis=1).astype(BF16)
        s_s[bl * q_rows:(bl + 1) * q_rows, :] = _dot(qb, k2)
    s = s_s[...] + bias_s[...]
    sink = sink_s[...]
    m = jnp.maximum(jnp.max(s, axis=1, keepdims=True), sink)
    p = jnp.exp2(s - m)
    inv = 1.0 / (jnp.sum(p, axis=1, keepdims=True) + jnp.exp2(sink - m))
    p_s[...] = p.astype(BF16)
    for bl in range(SEQ_PER_STEP):
        rows = slice(bl * q_rows, (bl + 1) * q_rows)
        v2 = jnp.concatenate([sv_ref[bl], cv_ref[bl]], axis=1).astype(BF16)
        oall_s[pl.ds(seq0 + bl, q_rows, stride=n_seq), :] = _dot_nt(p_s[rows, :], v2) * inv[rows, :]

    @pl.when(step == n_steps - 1)
    def _merge_rows():
        left = lax.broadcasted_iota(jnp.int32, (n_rows, LANES), 1) < HEAD_DIM
        slabs = []
        for pair in range(N_HEADS // 2):
            halves = []
            for hg in (2 * pair, 2 * pair + 1):
                o = oall_s[hg * n_rows:(hg + 1) * n_rows, :]
                if (hg % 2) != (hg // GQA_GROUP):
                    o = pltpu.roll(o, HEAD_DIM, 1)
                halves.append(o)
            slabs.append(jnp.where(left, halves[0], halves[1]))
        attn = jnp.concatenate(slabs, axis=1)
        o_ref[...] = by_sequence(_merge(by_token(x_ref[...]), attn, gate_s[...], anorm_ref, gnorm_ref, wout_ref))


def _mixer_sample(x, mnorm, win, sinks, vnorm, ws, bs, anorm, gnorm, wout, ck, cv, *, n_seq, n_tok):
    n_rows = n_seq * n_tok
    q_rows = N_HEADS * n_tok
    assert x.shape[0] == n_rows and n_seq % SEQ_PER_STEP == 0 and n_tok == 4 and n_seq == LANES
    assert SEQ_PER_STEP * n_tok == LANES
    cache_spec = pl.BlockSpec((SEQ_PER_STEP, KV_WIDTH, WINDOW), lambda s: (s, 0, 0))
    cache_shape = jax.ShapeDtypeStruct(ck.shape, F32)
    rows_spec = pl.BlockSpec((n_rows, D_MODEL), lambda s: (0, 0))
    gn_spec = pl.BlockSpec((n_seq, n_tok, GMLP_WIDTH), lambda s: (0, 0, 0))
    return pl.pallas_call(
        functools.partial(_mixer_sample_body, n_seq=n_seq, n_tok=n_tok),
        grid=(n_seq // SEQ_PER_STEP,),
        in_specs=[rows_spec, _resident((1, D_MODEL)), _resident(win.shape), _smem(), _resident(vnorm.shape),
                  _resident(ws.shape), _resident(bs.shape), _resident(anorm.shape), _resident(gnorm.shape), _resident(wout.shape),
                  cache_spec, cache_spec],
        out_specs=[rows_spec, gn_spec, cache_spec, cache_spec],
        out_shape=[jax.ShapeDtypeStruct(x.shape, F32), jax.ShapeDtypeStruct((n_seq, n_tok, GMLP_WIDTH), F32),
                   cache_shape, cache_shape],
        scratch_shapes=[
            pltpu.VMEM((D_MODEL // LANES, n_rows, LANES), F32),
            pltpu.VMEM((N_HEADS * n_rows, LANES), F32),
            pltpu.VMEM((N_HEADS * n_rows, LANES), F32),
            pltpu.VMEM((2 * KV_WIDTH, n_rows), F32),
            pltpu.VMEM((n_rows, GMLP_WIDTH), F32),
            pltpu.VMEM((SEQ_PER_STEP * q_rows, 2 * WINDOW), F32),
            pltpu.VMEM((SEQ_PER_STEP * q_rows, 1), F32),
            pltpu.VMEM((SEQ_PER_STEP * q_rows, 2 * WINDOW), F32),
            pltpu.VMEM((SEQ_PER_STEP * q_rows, 2 * WINDOW), BF16),
        ],
        compiler_params=pltpu.CompilerParams(dimension_semantics=("arbitrary",), vmem_limit_bytes=VMEM_LIMIT),
        name="mixer_sample",
    )(x, mnorm, win, sinks, vnorm, ws, bs, anorm, gnorm, wout, ck, cv)


def kernel(x_prompt, x_sample, cache_k, cache_v, ffn1_norm, ffn1_w_gate, ffn1_w_up, ffn1_w_down, mix_norm, w_in,
           attn_sinks, gmlp_v_norm, gmlp_w_spatial, gmlp_b_spatial, attn_out_norm, gmlp_out_norm, w_out, ffn2_norm,
           ffn2_w_gate, ffn2_w_up, ffn2_w_down, final_norm):
    batch, seq, _ = x_prompt.shape
    n_seq, n_tok, _ = x_sample.shape
    depth = cache_k.shape[0]
    assert depth == 1

    xp = x_prompt.reshape(batch * seq, D_MODEL)
    row = lambda v: v.reshape(1, -1)
    fnorm = row(final_norm)
    l = 0
    sinks = attn_sinks[l]
    vnorm = gmlp_v_norm[l]
    ws = gmlp_w_spatial[l]
    bs = gmlp_b_spatial[l]
    mnorm, anorm, gnorm = row(mix_norm[l]), row(attn_out_norm[l]), row(gmlp_out_norm[l])

    later = (ffn2_w_gate[l], ffn2_w_up[l], ffn2_w_down[l], w_in[l], w_out[l])
    xp, xs, wg2, wu2, wd2, win, wout = _ffn_half(xp, x_sample, row(ffn1_norm[l]), ffn1_w_gate[l], ffn1_w_up[l],
                                                 ffn1_w_down[l], fnorm, final_norm=False, subtiles=1, to_cast=later)
    xp, pk, pv = _mixer_prompt(xp, mnorm, win, sinks, vnorm, ws, bs, anorm, gnorm, wout, batch=batch)
    to_cols = lambda c: jnp.transpose(c, (0, 2, 3, 1)).reshape(n_seq, KV_WIDTH, WINDOW)
    to_rows = lambda c: jnp.transpose(c.reshape(-1, N_KV_HEADS, HEAD_DIM, WINDOW), (0, 3, 1, 2))[None]
    xs, gn, sk, sv = _mixer_sample(xs, mnorm, win, sinks, vnorm, ws, bs, anorm, gnorm, wout, to_cols(cache_k[l]),
                                   to_cols(cache_v[l]), n_seq=n_seq, n_tok=n_tok)
    yp, y_sample = _ffn_half(xp, xs, row(ffn2_norm[l]), wg2, wu2, wd2, fnorm, final_norm=True, subtiles=2,
                             sample_out_shape=x_sample.shape)

    y_prompt = yp.reshape(batch, seq, D_MODEL)
    return (y_prompt, y_sample, to_rows(pk), to_rows(pv), to_rows(sk), to_rows(sv), gn[None])
```
